```python
import jax, jax.numpy as jnp
from jax import lax
import numpy as np

D_MODEL = 1024
BATCH = 16
SEQ = 2048
DEPTH = 1

HEAD_DIM = 64
ATTN_GROUPS = ((128, 1), (512, 4), (2048, 16))
HEADS_PER_GROUP = 4
N_ATTN_HEADS = HEADS_PER_GROUP * len(ATTN_GROUPS)
ATTN_WIDTH = N_ATTN_HEADS * HEAD_DIM
BLK = 128
ROPE_THETA = 10000.0
SGU_CHUNK = 128
SGU_GROUPS = 12
SGU_GROUP_DIM = 64
SGU_WIDTH = SGU_GROUPS * SGU_GROUP_DIM
IN_WIDTH = 3 * ATTN_WIDTH + 2 * SGU_WIDTH + 2 * D_MODEL
SPLITS = (ATTN_WIDTH, 2 * ATTN_WIDTH, 3 * ATTN_WIDTH, 3 * ATTN_WIDTH + 2 * SGU_WIDTH, 3 * ATTN_WIDTH + 2 * SGU_WIDTH + D_MODEL)
N_EXPERTS = 32
TOP_K = 4
D_EXPERT = D_MODEL
SWIGLU_LIMIT = 7.0
SWIGLU_ALPHA = 1.702
MOE_BLOCK = 128
EPS = 1e-6
NEG_INF = -1e30

kernel_name = "hybrid_dilated_attn_sgu_moe"


def rms_norm(x, g):
    x32 = x.astype(jnp.float32)
    y = x32 * lax.rsqrt(jnp.mean(x32 * x32, axis=-1, keepdims=True) + EPS)
    return (y * g.astype(jnp.float32)).astype(x.dtype)


def layer_norm(x, g, b):
    x32 = x.astype(jnp.float32)
    mu = jnp.mean(x32, axis=-1, keepdims=True)
    xc = x32 - mu
    y = xc * lax.rsqrt(jnp.mean(xc * xc, axis=-1, keepdims=True) + EPS)
    return (y * g.astype(jnp.float32) + b.astype(jnp.float32)).astype(x.dtype)


def rope(x, positions):
    half = HEAD_DIM // 2
    inv_freq = ROPE_THETA ** (-jnp.arange(half, dtype=jnp.float32) / half)
    ang = positions.astype(jnp.float32)[..., None] * inv_freq
    cos = jnp.cos(ang)[:, :, None, :]
    sin = jnp.sin(ang)[:, :, None, :]
    x32 = x.astype(jnp.float32)
    x1, x2 = x32[..., :half], x32[..., half:]
    out = jnp.concatenate([x1 * cos - x2 * sin, x2 * cos + x1 * sin], axis=-1)
    return out.astype(x.dtype)


def dilated_window_attention(q, k, v, window, dilation):
    B, S, H, hd = q.shape
    n_back = window // dilation
    assert n_back <= BLK
    L = S // dilation
    nb = -(-L // BLK)
    Lp = nb * BLK

    def to_res(t):
        return t.reshape(B, L, dilation, H, hd).transpose(0, 2, 3, 1, 4)

    qr = jnp.pad(to_res(q), ((0, 0), (0, 0), (0, 0), (0, Lp - L), (0, 0)))
    qr = qr.reshape(B, dilation, H, nb, BLK, hd)

    def windows(t):
        tp = jnp.pad(to_res(t), ((0, 0), (0, 0), (0, 0), (BLK, Lp - L), (0, 0)))
        tp = tp.reshape(B, dilation, H, nb + 1, BLK, hd)
        return jnp.concatenate([tp[:, :, :, :-1], tp[:, :, :, 1:]], axis=4)

    kw = windows(k)
    vw = windows(v)
    s = jnp.einsum('brhnqc,brhnkc->brhnqk', qr.astype(jnp.float32), kw.astype(jnp.float32)) * (hd ** -0.5)
    qi = jnp.arange(BLK)[:, None]
    kj = jnp.arange(2 * BLK)[None, :]
    dist = BLK + qi - kj
    key_m = jnp.arange(nb)[:, None, None] * BLK + kj[None] - BLK
    valid = (dist >= 0)[None] & (dist <= n_back)[None] & (key_m >= 0)
    s = jnp.where(valid, s, NEG_INF)
    m = jnp.max(s, axis=-1, keepdims=True)
    p = jnp.exp(s - m)
    l = jnp.sum(p, axis=-1, keepdims=True)
    o = jnp.einsum('brhnqk,brhnkc->brhnqc', p, vw.astype(jnp.float32)) / l
    lse = (m + jnp.log(l))[..., 0]
    o = o.reshape(B, dilation, H, Lp, hd)[:, :, :, :L].transpose(0, 3, 1, 2, 4).reshape(B, S, H, hd)
    lse = lse.reshape(B, dilation, H, Lp)[:, :, :, :L].transpose(0, 3, 1, 2).reshape(B, S, H)
    return o, lse


def spatial_gating(z, ln_g, ln_b, w_s, b_s):
    B, S, _ = z.shape
    u, v = z[..., :SGU_WIDTH], z[..., SGU_WIDTH:]
    v = layer_norm(v, ln_g, ln_b)
    vg = v.reshape(B, S // SGU_CHUNK, SGU_CHUNK, SGU_GROUPS, SGU_GROUP_DIM)
    causal = jnp.tril(jnp.ones((SGU_CHUNK, SGU_CHUNK), dtype=bool))
    w = jnp.where(causal[None], w_s, jnp.zeros_like(w_s))
    mixed = jnp.einsum('gts,bcsgd->bctgd', w, vg) + b_s.T[None, None, :, :, None]
    return u * mixed.reshape(B, S, SGU_WIDTH)


def mixer_block(x, positions, norm1_g, w_in, q_norm_g, k_norm_g, sgu_ln_g, sgu_ln_b,
                w_spatial, b_spatial, w_branch_a, w_branch_b, w_out):
    B, S, _ = x.shape
    h = rms_norm(x, norm1_g)
    proj = h @ w_in
    q, k, v, z, g_a, g_b = jnp.split(proj, SPLITS, axis=-1)
    q = rope(rms_norm(q.reshape(B, S, N_ATTN_HEADS, HEAD_DIM), q_norm_g), positions)
    k = rope(rms_norm(k.reshape(B, S, N_ATTN_HEADS, HEAD_DIM), k_norm_g), positions)
    v = v.reshape(B, S, N_ATTN_HEADS, HEAD_DIM)
    outs, lses = [], []
    for gi, (window, dilation) in enumerate(ATTN_GROUPS):
        sl = slice(gi * HEADS_PER_GROUP, (gi + 1) * HEADS_PER_GROUP)
        o, lse = dilated_window_attention(q[:, :, sl], k[:, :, sl], v[:, :, sl], window, dilation)
        outs.append(o)
        lses.append(lse)
    alpha = jax.nn.softmax(jnp.stack(lses, axis=0), axis=0)
    attn = jnp.concatenate([outs[i] * alpha[i][..., None] for i in range(len(ATTN_GROUPS))], axis=2)
    attn = attn.reshape(B, S, ATTN_WIDTH).astype(x.dtype)
    y_a = attn @ w_branch_a
    sgu = spatial_gating(jax.nn.gelu(z, approximate=False), sgu_ln_g, sgu_ln_b, w_spatial, b_spatial)
    y_b = sgu @ w_branch_b
    merged = jax.nn.sigmoid(g_a) * y_a + jax.nn.sigmoid(g_b) * y_b
    return x + merged @ w_out


def moe_block(x, norm2_g, w_router, b_router, w_gate_up, b_gate_up, w_down, b_down):
    B, S, D = x.shape
    T = B * S
    hf = rms_norm(x, norm2_g).reshape(T, D)
    logits = (hf @ w_router + b_router).astype(jnp.float32)
    top_v, top_e = lax.top_k(logits, TOP_K)
    gates = jax.nn.softmax(top_v, axis=-1)
    n_assign = T * TOP_K
    e_flat = top_e.reshape(-1)
    w_flat = gates.reshape(-1)
    tok_flat = jnp.arange(n_assign, dtype=jnp.int32) // TOP_K
    order = jnp.argsort(e_flat)
    e_sorted = e_flat[order]
    counts = jnp.bincount(e_flat, length=N_EXPERTS)
    padded = (counts + MOE_BLOCK - 1) // MOE_BLOCK * MOE_BLOCK
    start = jnp.cumsum(counts) - counts
    pend = jnp.cumsum(padded)
    pstart = pend - padded
    dest = pstart[e_sorted] + (jnp.arange(n_assign) - start[e_sorted])
    cap = (-(-n_assign // MOE_BLOCK) + N_EXPERTS) * MOE_BLOCK
    n_blocks = cap // MOE_BLOCK
    buf_tok = jnp.full((cap,), T, dtype=jnp.int32).at[dest].set(tok_flat[order])
    buf_w = jnp.zeros((cap,), jnp.float32).at[dest].set(w_flat[order])
    blk_e = jnp.sum(jnp.arange(n_blocks)[:, None] * MOE_BLOCK >= pend[None, :], axis=1)
    blk_e = jnp.minimum(blk_e, N_EXPERTS - 1)
    hpad = jnp.concatenate([hf, jnp.zeros((1, D), hf.dtype)], axis=0)
    xb = hpad[buf_tok].reshape(n_blocks, MOE_BLOCK, D)

    def expert_rows(args):
        xblk, e = args
        gu = xblk @ w_gate_up[e] + b_gate_up[e]
        gate, up = gu[:, :D_EXPERT], gu[:, D_EXPERT:]
        gate = jnp.minimum(gate, SWIGLU_LIMIT)
        up = jnp.clip(up, -SWIGLU_LIMIT, SWIGLU_LIMIT)
        act = (up + 1.0) * (gate * jax.nn.sigmoid(SWIGLU_ALPHA * gate))
        return act @ w_down[e] + b_down[e]

    yb = lax.map(expert_rows, (xb, blk_e)).reshape(cap, D)
    y = jnp.zeros((T + 1, D), jnp.float32).at[buf_tok].add(yb.astype(jnp.float32) * buf_w[:, None])[:T]
    return x + y.reshape(B, S, D).astype(x.dtype)


def setup_inputs(seed: int = 0) -> dict:
    key = jax.random.key(seed)
    ks = jax.random.split(key, 20)
    f32 = jnp.float32
    nrm = lambda k, shape, scale: jax.random.normal(k, shape, f32) * scale
    x = jax.random.normal(ks[0], (BATCH, SEQ, D_MODEL), f32)
    offset = jax.random.randint(ks[1], (BATCH, 1), 0, 1024, dtype=jnp.int32)
    positions = offset + jnp.arange(SEQ, dtype=jnp.int32)[None, :]
    return {
        "x": x,
        "positions": positions,
        "norm1_g": 1.0 + nrm(ks[2], (DEPTH, D_MODEL), 0.02),
        "w_in": nrm(ks[3], (DEPTH, D_MODEL, IN_WIDTH), D_MODEL ** -0.5),
        "q_norm_g": 1.0 + nrm(ks[4], (DEPTH, HEAD_DIM), 0.02),
        "k_norm_g": 1.0 + nrm(ks[5], (DEPTH, HEAD_DIM), 0.02),
        "sgu_ln_g": 1.0 + nrm(ks[6], (DEPTH, SGU_WIDTH), 0.02),
        "sgu_ln_b": nrm(ks[7], (DEPTH, SGU_WIDTH), 0.02),
        "w_spatial": nrm(ks[8], (DEPTH, SGU_GROUPS, SGU_CHUNK, SGU_CHUNK), SGU_CHUNK ** -0.5),
        "b_spatial": 1.0 + nrm(ks[9], (DEPTH, SGU_GROUPS, SGU_CHUNK), 0.1),
        "w_branch_a": nrm(ks[10], (DEPTH, ATTN_WIDTH, D_MODEL), ATTN_WIDTH ** -0.5),
        "w_branch_b": nrm(ks[11], (DEPTH, SGU_WIDTH, D_MODEL), SGU_WIDTH ** -0.5),
        "w_out": nrm(ks[12], (DEPTH, D_MODEL, D_MODEL), D_MODEL ** -0.5),
        "norm2_g": 1.0 + nrm(ks[13], (DEPTH, D_MODEL), 0.02),
        "w_router": nrm(ks[14], (DEPTH, D_MODEL, N_EXPERTS), D_MODEL ** -0.5),
        "b_router": nrm(ks[15], (DEPTH, N_EXPERTS), 0.01),
        "w_gate_up": nrm(ks[16], (DEPTH, N_EXPERTS, D_MODEL, 2 * D_EXPERT), D_MODEL ** -0.5),
        "b_gate_up": nrm(ks[17], (DEPTH, N_EXPERTS, 2 * D_EXPERT), 0.01),
        "w_down": nrm(ks[18], (DEPTH, N_EXPERTS, D_EXPERT, D_MODEL), D_EXPERT ** -0.5),
        "b_down": nrm(ks[19], (DEPTH, N_EXPERTS, D_MODEL), 0.01),
    }


def reference(x, positions, norm1_g, w_in, q_norm_g, k_norm_g, sgu_ln_g, sgu_ln_b,
              w_spatial, b_spatial, w_branch_a, w_branch_b, w_out, norm2_g,
              w_router, b_router, w_gate_up, b_gate_up, w_down, b_down):
    for layer in range(DEPTH):
        x = mixer_block(x, positions, norm1_g[layer], w_in[layer], q_norm_g[layer], k_norm_g[layer],
                        sgu_ln_g[layer], sgu_ln_b[layer], w_spatial[layer], b_spatial[layer],
                        w_branch_a[layer], w_branch_b[layer], w_out[layer])
        x = moe_block(x, norm2_g[layer], w_router[layer], b_router[layer], w_gate_up[layer],
                      b_gate_up[layer], w_down[layer], b_down[layer])
    return x
```

```python
import functools

import jax
import jax.numpy as jnp
from jax import lax
from jax.experimental import pallas as pl
from jax.experimental.pallas import tpu as pltpu

F32 = jnp.float32
BF16 = jnp.bfloat16
I32 = jnp.int32

D_MODEL = 1024
HEAD_DIM = 64
ATTN_GROUPS = ((128, 1), (512, 4), (2048, 16))
N_GROUPS = len(ATTN_GROUPS)
GROUP_WIDTH = 256
ATTN_WIDTH = N_GROUPS * GROUP_WIDTH
BLK = 128
ROPE_THETA = 10000.0
SGU_CHUNK = 128
SGU_GROUP_DIM = 64
SGU_WIDTH = 768
QKVZ_WIDTH = 3 * ATTN_WIDTH + 2 * SGU_WIDTH
GATE_WIDTH = 2 * D_MODEL
N_EXPERTS = 32
TOP_K = 4
D_EXPERT = D_MODEL
SWIGLU_LIMIT = 7.0
SWIGLU_ALPHA = 1.702
EPS = 1e-6
NEG_INF = -1e30
LANES = 128

TM_INPROJ = 512
TM_POST = 512
TM_DISPATCH = 256
TM_COMBINE = 256
EXPERT_TILE = 512
ROW_CHUNK = 256

_NT = (((1,), (1,)), ((), ()))


def _dot(a, b):
    return jnp.dot(a, b, preferred_element_type=F32)


def _dot_nt(a, b):
    return lax.dot_general(a, b, _NT, preferred_element_type=F32)


def _params(vmem_mb, sem):
    return pltpu.CompilerParams(dimension_semantics=sem, vmem_limit_bytes=vmem_mb * 1024 * 1024)


def _inproj_kernel(x_ref, g_ref, w_ref, oa_ref, og_ref):
    x = x_ref[...]
    y = x * lax.rsqrt(jnp.mean(x * x, axis=-1, keepdims=True) + EPS)
    h = (y * g_ref[...]).astype(BF16)
    for lo in range(0, QKVZ_WIDTH, 768):
        oa_ref[:, lo:lo + 768] = _dot(h, w_ref[:, lo:lo + 768]).astype(BF16)
    for lo in range(0, GATE_WIDTH, 1024):
        og_ref[:, lo:lo + 1024] = _dot(h, w_ref[:, QKVZ_WIDTH + lo:QKVZ_WIDTH + lo + 1024]).astype(BF16)


def _inproj(x2, g, w):
    t = x2.shape[0]
    n = w.shape[1]
    return pl.pallas_call(
        _inproj_kernel,
        grid=(t // TM_INPROJ,),
        in_specs=[
            pl.BlockSpec((TM_INPROJ, D_MODEL), lambda i: (i, 0)),
            pl.BlockSpec((1, D_MODEL), lambda i: (0, 0)),
            pl.BlockSpec((D_MODEL, n), lambda i: (0, 0), pipeline_mode=pl.Buffered(1)),
        ],
        out_specs=[
            pl.BlockSpec((TM_INPROJ, QKVZ_WIDTH), lambda i: (i, 0)),
            pl.BlockSpec((TM_INPROJ, GATE_WIDTH), lambda i: (i, 0)),
        ],
        out_shape=[
            jax.ShapeDtypeStruct((t, QKVZ_WIDTH), BF16),
            jax.ShapeDtypeStruct((t, GATE_WIDTH), BF16),
        ],
        compiler_params=_params(48, ("arbitrary",)),
        name="inproj",
    )(x2, g, w)


def _attn_prep(src_ref, dst_ref, gain_ref, cos_s, sin_s, nat_s, seq, d, mode):
    lane = lax.broadcasted_iota(I32, (ROW_CHUNK, LANES), 1)
    first_half = (lane % HEAD_DIM) < (HEAD_DIM // 2)
    bi = lax.broadcasted_iota(I32, (LANES, LANES), 0) // HEAD_DIM
    bj = lax.broadcasted_iota(I32, (LANES, LANES), 1) // HEAD_DIM
    head_sum = jnp.where(bi == bj, 1.0, 0.0).astype(BF16)

    def body(i, carry):
        r0 = pl.multiple_of(i * ROW_CHUNK, ROW_CHUNK)
        for c in range(GROUP_WIDTH // LANES):
            x = src_ref[0, pl.ds(r0, ROW_CHUNK), c * LANES:(c + 1) * LANES].astype(F32)
            if mode == "v":
                y = x
            else:
                ss = x * x
                hi = ss.astype(BF16)
                lo = (ss - hi.astype(F32)).astype(BF16)
                ms = (_dot(hi, head_sum) + _dot(lo, head_sum)) * (1.0 / HEAD_DIM)
                y = x * lax.rsqrt(ms + EPS) * gain_ref[...]
                partner = jnp.where(first_half, pltpu.roll(y, LANES - HEAD_DIM // 2, 1), pltpu.roll(y, HEAD_DIM // 2, 1))
                y = y * cos_s[pl.ds(r0, ROW_CHUNK), :] + partner * sin_s[pl.ds(r0, ROW_CHUNK), :]
                if mode == "q":
                    y = y * (HEAD_DIM ** -0.5)
            if d == 1:
                dst_ref[pl.ds(r0, ROW_CHUNK), c * LANES:(c + 1) * LANES] = y.astype(BF16)
            else:
                nat_s[c, pl.ds(r0, ROW_CHUNK), :] = y
        return carry

    lax.fori_loop(0, seq // ROW_CHUNK, body, 0)
    if d > 1:
        sub = seq // d
        for r in range(d):
            for c in range(GROUP_WIDTH // LANES):
                dst_ref[r * sub:(r + 1) * sub, c * LANES:(c + 1) * LANES] = nat_s[c, pl.ds(r, sub, stride=d), :].astype(BF16)


def _attn_blocks(qr_s, kr_s, vr_s, ores_s, lres_s, seq, d):
    sub = seq // d
    nb = sub // BLK
    win = 2 * BLK if nb > 1 else BLK
    log2d = d.bit_length() - 1
    lane_head = lax.broadcasted_iota(I32, (BLK, GROUP_WIDTH), 1) // HEAD_DIM
    qi = lax.broadcasted_iota(I32, (BLK, win), 0)
    kj = lax.broadcasted_iota(I32, (BLK, win), 1)
    rel = qi - kj

    def body(idx, carry):
        r = idx & (d - 1)
        n = idx >> log2d
        row0 = pl.multiple_of(r * sub + n * BLK, BLK)
        if nb > 1:
            kn = jnp.maximum(n - 1, 0)
            k0 = pl.multiple_of(r * sub + kn * BLK, BLK)
            dist = rel + (n - kn) * BLK
        else:
            k0 = row0
            dist = rel
        valid = lax.bitcast_convert_type(dist, jnp.uint32) <= jnp.uint32(BLK)
        q = qr_s[pl.ds(row0, BLK), :]
        kw = kr_s[pl.ds(k0, win), :]
        vw = vr_s[pl.ds(k0, win), :]
        zero = jnp.zeros_like(q)
        qm = jnp.concatenate([jnp.where(lane_head == h, q, zero) for h in range(4)], axis=0)
        s = _dot_nt(qm, kw)
        ps, ms, ls = [], [], []
        for h in range(4):
            sh = jnp.where(valid, s[h * BLK:(h + 1) * BLK], NEG_INF)
            m = jnp.max(sh, axis=-1, keepdims=True)
            p = jnp.exp(sh - m)
            ls.append(jnp.sum(p, axis=-1, keepdims=True))
            ms.append(m)
            ps.append(p.astype(BF16))
        pv = _dot(jnp.concatenate(ps, axis=0), vw)
        o = jnp.zeros((BLK, GROUP_WIDTH), F32)
        lse = jnp.zeros((BLK, GROUP_WIDTH), F32)
        for h in range(4):
            sel = lane_head == h
            o = jnp.where(sel, pv[h * BLK:(h + 1) * BLK] * (1.0 / ls[h]), o)
            lse = jnp.where(sel, ms[h] + jnp.log(ls[h]), lse)
        ores_s[pl.ds(row0, BLK), :] = o
        lres_s[pl.ds(row0, BLK), :] = lse
        return carry

    lax.fori_loop(0, seq // BLK, body, 0)


def _attn_kernel(q_ref, k_ref, v_ref, pos_ref, invf_ref, qg_ref, kg_ref, o_ref,
                 cos_s, sin_s, nat_s, qr_s, kr_s, vr_s, ores_s, lres_s, onat_s, lnat_s, *, seq):
    g = pl.program_id(1)

    @pl.when(g == 0)
    def _():
        lane = lax.broadcasted_iota(I32, (1, LANES), 1)
        sign = jnp.where((lane % HEAD_DIM) < (HEAD_DIM // 2), -1.0, 1.0)

        def body(i, carry):
            r0 = pl.multiple_of(i * ROW_CHUNK, ROW_CHUNK)
            ang = pos_ref[0, pl.ds(r0, ROW_CHUNK), :].astype(F32) * invf_ref[...]
            cos_s[pl.ds(r0, ROW_CHUNK), :] = jnp.cos(ang)
            sin_s[pl.ds(r0, ROW_CHUNK), :] = jnp.sin(ang) * sign
            return carry

        lax.fori_loop(0, seq // ROW_CHUNK, body, 0)

    for gi, (_, d) in enumerate(ATTN_GROUPS):

        @pl.when(g == gi)
        def _(gi=gi, d=d):
            _attn_prep(q_ref, qr_s, qg_ref, cos_s, sin_s, nat_s, seq, d, "q")
            _attn_prep(k_ref, kr_s, kg_ref, cos_s, sin_s, nat_s, seq, d, "k")
            _attn_prep(v_ref, vr_s, None, cos_s, sin_s, nat_s, seq, d, "v")
            _attn_blocks(qr_s, kr_s, vr_s, ores_s, lres_s, seq, d)
            sub = seq // d
            for r in range(d):
                for c in range(GROUP_WIDTH // LANES):
                    t = gi * (GROUP_WIDTH // LANES) + c
                    onat_s[t, pl.ds(r, sub, stride=d), :] = ores_s[r * sub:(r + 1) * sub, c * LANES:(c + 1) * LANES]
                    lnat_s[t, pl.ds(r, sub, stride=d), :] = lres_s[r * sub:(r + 1) * sub, c * LANES:(c + 1) * LANES]

    @pl.when(g == N_GROUPS - 1)
    def _():
        tiles = GROUP_WIDTH // LANES

        def body(i, carry):
            r0 = pl.multiple_of(i * ROW_CHUNK, ROW_CHUNK)
            for c in range(tiles):
                ls = [lnat_s[gi * tiles + c, pl.ds(r0, ROW_CHUNK), :] for gi in range(N_GROUPS)]
                m = jnp.maximum(jnp.maximum(ls[0], ls[1]), ls[2])
                es = [jnp.exp(l - m) for l in ls]
                inv = 1.0 / (es[0] + es[1] + es[2])
                for gi in range(N_GROUPS):
                    t = gi * tiles + c
                    o_ref[0, pl.ds(r0, ROW_CHUNK), t * LANES:(t + 1) * LANES] = (
                        onat_s[t, pl.ds(r0, ROW_CHUNK), :] * (es[gi] * inv)).astype(BF16)
            return carry

        lax.fori_loop(0, seq // ROW_CHUNK, body, 0)


def _attention(qkvz3, pos3, invf, qg, kg):
    b, seq, _ = qkvz3.shape
    tiles = GROUP_WIDTH // LANES
    nq = ATTN_WIDTH // GROUP_WIDTH
    return pl.pallas_call(
        functools.partial(_attn_kernel, seq=seq),
        grid=(b, N_GROUPS),
        in_specs=[
            pl.BlockSpec((1, seq, GROUP_WIDTH), lambda i, g: (i, 0, g)),
            pl.BlockSpec((1, seq, GROUP_WIDTH), lambda i, g: (i, 0, nq + g)),
            pl.BlockSpec((1, seq, GROUP_WIDTH), lambda i, g: (i, 0, 2 * nq + g)),
            pl.BlockSpec((1, seq, 1), lambda i, g: (i, 0, 0)),
            pl.BlockSpec((1, LANES), lambda i, g: (0, 0)),
            pl.BlockSpec((1, LANES), lambda i, g: (0, 0)),
            pl.BlockSpec((1, LANES), lambda i, g: (0, 0)),
        ],
        out_specs=pl.BlockSpec((1, seq, ATTN_WIDTH), lambda i, g: (i, 0, 0)),
        out_shape=jax.ShapeDtypeStruct((b, seq, ATTN_WIDTH), BF16),
        scratch_shapes=[
            pltpu.VMEM((seq, LANES), F32),
            pltpu.VMEM((seq, LANES), F32),
            pltpu.VMEM((tiles, seq, LANES), F32),
            pltpu.VMEM((seq, GROUP_WIDTH), BF16),
            pltpu.VMEM((seq, GROUP_WIDTH), BF16),
            pltpu.VMEM((seq, GROUP_WIDTH), BF16),
            pltpu.VMEM((seq, GROUP_WIDTH), F32),
            pltpu.VMEM((seq, GROUP_WIDTH), F32),
            pltpu.VMEM((N_GROUPS * tiles, seq, LANES), F32),
            pltpu.VMEM((N_GROUPS * tiles, seq, LANES), F32),
        ],
        compiler_params=_params(48, ("arbitrary", "arbitrary")),
        name="attention",
    )(qkvz3, qkvz3, qkvz3, pos3, invf, qg, kg)


def _gelu(x):
    return 0.5 * x * (1.0 + lax.erf(x * 0.7071067811865476))


def _post_kernel(attn_ref, u_ref, vz_ref, ga_ref, gb_ref, x_ref, wa_ref, wb_ref, wo_ref,
                 lng_ref, lnb_ref, wcat_ref, bcat_ref, n2g_ref, wrh_ref, wrl_ref, br_ref,
                 x1_ref, hf_ref, e_ref, rank_ref, gt_ref, cnt_ref, sgu_s, carry_s):
    tm = x_ref.shape[0]
    step = pl.program_id(0)

    @pl.when(step == 0)
    def _():
        carry_s[...] = jnp.zeros_like(carry_s)

    lane = lax.broadcasted_iota(I32, (SGU_CHUNK, LANES), 1)
    low = lane < SGU_GROUP_DIM
    trow = lax.broadcasted_iota(I32, (SGU_CHUNK, 2 * SGU_CHUNK), 0)
    tcol = lax.broadcasted_iota(I32, (SGU_CHUNK, 2 * SGU_CHUNK), 1) % SGU_CHUNK
    causal = tcol <= trow

    def chunk(c, carry):
        r0 = pl.multiple_of(c * SGU_CHUNK, SGU_CHUNK)
        u = _gelu(u_ref[pl.ds(r0, SGU_CHUNK), :].astype(F32))
        v = _gelu(vz_ref[pl.ds(r0, SGU_CHUNK), :].astype(F32))
        mu = jnp.mean(v, axis=-1, keepdims=True)
        vc = v - mu
        vn = vc * lax.rsqrt(jnp.mean(vc * vc, axis=-1, keepdims=True) + EPS)
        vn = (vn * lng_ref[...] + lnb_ref[...]).astype(BF16)
        zero = jnp.zeros((SGU_CHUNK, LANES), BF16)
        for j in range(SGU_WIDTH // LANES):
            vt = vn[:, j * LANES:(j + 1) * LANES]
            rhs = jnp.concatenate([jnp.where(low, vt, zero), jnp.where(low, zero, vt)], axis=0)
            wj = jnp.where(causal, wcat_ref[j], jnp.zeros((), BF16))
            mixed = _dot(wj, rhs) + bcat_ref[j]
            sgu_s[pl.ds(r0, SGU_CHUNK), j * LANES:(j + 1) * LANES] = (u[:, j * LANES:(j + 1) * LANES] * mixed).astype(BF16)
        return carry

    lax.fori_loop(0, tm // SGU_CHUNK, chunk, 0)

    y_a = _dot(attn_ref[...], wa_ref[...])
    y_b = _dot(sgu_s[...], wb_ref[...])
    merged = jax.nn.sigmoid(ga_ref[...].astype(F32)) * y_a + jax.nn.sigmoid(gb_ref[...].astype(F32)) * y_b
    x1 = x_ref[...] + _dot(merged.astype(BF16), wo_ref[...])
    x1_ref[...] = x1

    hf = x1 * lax.rsqrt(jnp.mean(x1 * x1, axis=-1, keepdims=True) + EPS) * n2g_ref[...]
    hf_ref[...] = hf
    hi = hf.astype(BF16)
    lo = (hf - hi.astype(F32)).astype(BF16)
    logits = _dot_nt(wrh_ref[...], hi) + _dot_nt(wrl_ref[...], hi) + _dot_nt(wrh_ref[...], lo) + br_ref[...]

    eio = lax.broadcasted_iota(I32, (N_EXPERTS, tm), 0)
    work = logits
    top_v, onehots = [], []
    for k in range(TOP_K):
        m = jnp.max(work, axis=0, keepdims=True)
        idx = jnp.min(jnp.where(work == m, eio, N_EXPERTS), axis=0, keepdims=True)
        oh = eio == idx
        work = jnp.where(oh, -jnp.inf, work)
        top_v.append(m)
        onehots.append(oh)
        e_ref[k:k + 1, :] = idx
    ex = [jnp.exp(v - top_v[0]) for v in top_v]
    inv = 1.0 / (ex[0] + ex[1] + ex[2] + ex[3])
    gates = jnp.concatenate([e * inv for e in ex] + [jnp.zeros((8 - TOP_K, tm), F32)], axis=0)
    gt_ref[...] = gates.T

    sel = jnp.zeros((N_EXPERTS, tm), F32)
    for oh in onehots:
        sel = jnp.where(oh, 1.0, sel)
    ti = lax.broadcasted_iota(I32, (tm, tm), 0)
    tj = lax.broadcasted_iota(I32, (tm, tm), 1)
    before = jnp.where(ti < tj, 1.0, 0.0).astype(BF16)
    cum = _dot(sel.astype(BF16), before) + carry_s[:, 0:1]
    for k, oh in enumerate(onehots):
        rank_ref[k:k + 1, :] = jnp.sum(jnp.where(oh, cum, 0.0), axis=0, keepdims=True).astype(I32)
    carry_s[...] = carry_s[...] + jnp.sum(sel, axis=1, keepdims=True)
    cnt_ref[...] = carry_s[...].astype(I32)


def _post(attn2, qkvz2, gates2, x2, wa, wb, wo, lng, lnb, wcat, bcat, n2g, wrh, wrl, br):
    t = x2.shape[0]
    tm = TM_POST
    nu = 3 * ATTN_WIDTH // SGU_WIDTH
    const = lambda *shape: pl.BlockSpec(shape, lambda i: (0,) * len(shape), pipeline_mode=pl.Buffered(1))
    return pl.pallas_call(
        _post_kernel,
        grid=(t // tm,),
        in_specs=[
            pl.BlockSpec((tm, ATTN_WIDTH), lambda i: (i, 0)),
            pl.BlockSpec((tm, SGU_WIDTH), lambda i: (i, nu)),
            pl.BlockSpec((tm, SGU_WIDTH), lambda i: (i, nu + 1)),
            pl.BlockSpec((tm, D_MODEL), lambda i: (i, 0)),
            pl.BlockSpec((tm, D_MODEL), lambda i: (i, 1)),
            pl.BlockSpec((tm, D_MODEL), lambda i: (i, 0)),
            const(ATTN_WIDTH, D_MODEL),
            const(SGU_WIDTH, D_MODEL),
            const(D_MODEL, D_MODEL),
            const(1, SGU_WIDTH),
            const(1, SGU_WIDTH),
            const(SGU_WIDTH // LANES, SGU_CHUNK, 2 * SGU_CHUNK),
            const(SGU_WIDTH // LANES, SGU_CHUNK, LANES),
            const(1, D_MODEL),
            const(N_EXPERTS, D_MODEL),
            const(N_EXPERTS, D_MODEL),
            const(N_EXPERTS, 1),
        ],
        out_specs=[
            pl.BlockSpec((tm, D_MODEL), lambda i: (i, 0)),
            pl.BlockSpec((tm, D_MODEL), lambda i: (i, 0)),
            pl.BlockSpec((TOP_K, tm), lambda i: (0, i)),
            pl.BlockSpec((TOP_K, tm), lambda i: (0, i)),
            pl.BlockSpec((tm, 8), lambda i: (i, 0)),
            pl.BlockSpec((N_EXPERTS, LANES), lambda i: (0, 0)),
        ],
        out_shape=[
            jax.ShapeDtypeStruct((t, D_MODEL), F32),
            jax.ShapeDtypeStruct((t, D_MODEL), F32),
            jax.ShapeDtypeStruct((TOP_K, t), I32),
            jax.ShapeDtypeStruct((TOP_K, t), I32),
            jax.ShapeDtypeStruct((t, 8), F32),
            jax.ShapeDtypeStruct((N_EXPERTS, LANES), I32),
        ],
        scratch_shapes=[
            pltpu.VMEM((tm, SGU_WIDTH), BF16),
            pltpu.VMEM((N_EXPERTS, LANES), F32),
        ],
        compiler_params=_params(48, ("arbitrary",)),
        name="post",
    )(attn2, qkvz2, qkvz2, gates2, gates2, x2, wa, wb, wo, lng, lnb, wcat, bcat, n2g, wrh, wrl, br)


def _pos_kernel(pstart_ref, e_ref, rank_ref, pos_ref):
    e = e_ref[...]
    acc = rank_ref[...]
    for x in range(N_EXPERTS):
        acc = acc + jnp.where(e == x, pstart_ref[x], 0)
    pos_ref[...] = acc


def _positions(pstart, e4, rank4):
    t = e4.shape[1]
    tm = 4096
    return pl.pallas_call(
        _pos_kernel,
        grid_spec=pltpu.PrefetchScalarGridSpec(
            num_scalar_prefetch=1,
            grid=(t // tm,),
            in_specs=[
                pl.BlockSpec((TOP_K, tm), lambda i, ps: (0, i)),
                pl.BlockSpec((TOP_K, tm), lambda i, ps: (0, i)),
            ],
            out_specs=pl.BlockSpec((TOP_K, tm), lambda i, ps: (0, i)),
        ),
        out_shape=jax.ShapeDtypeStruct((TOP_K, t), I32),
        compiler_params=_params(32, ("arbitrary",)),
        name="positions",
    )(pstart, e4, rank4)


def _dispatch_kernel(pos_ref, hf_ref, xs_in_ref, xs_ref, sem):
    del xs_in_ref
    tm = hf_ref.shape[0]

    def body(t, carry):
        for k in range(TOP_K):
            p = pos_ref[k, t]
            pltpu.make_async_copy(hf_ref.at[pl.ds(t, 1), :], xs_ref.at[pl.ds(p, 1), :], sem).start()
        return carry

    lax.fori_loop(0, tm, body, 0, unroll=8)
    for k in range(TOP_K):
        pltpu.make_async_copy(hf_ref, xs_ref.at[pl.ds(0, tm), :], sem).wait()


def _dispatch(pos4, hf, cap):
    t = hf.shape[0]
    tm = TM_DISPATCH
    xs0 = jnp.zeros((cap, D_MODEL), F32)
    return pl.pallas_call(
        _dispatch_kernel,
        grid=(t // tm,),
        in_specs=[
            pl.BlockSpec((TOP_K, tm), lambda i: (0, i), memory_space=pltpu.SMEM),
            pl.BlockSpec((tm, D_MODEL), lambda i: (i, 0)),
            pl.BlockSpec(memory_space=pl.ANY),
        ],
        out_specs=pl.BlockSpec(memory_space=pl.ANY),
        out_shape=jax.ShapeDtypeStruct((cap, D_MODEL), F32),
        scratch_shapes=[pltpu.SemaphoreType.DMA(())],
        input_output_aliases={2: 0},
        compiler_params=_params(32, ("arbitrary",)),
        name="dispatch",
    )(pos4, hf, xs0)


def _expert_kernel(te_ref, nu_ref, xs_ref, wgu_ref, bgu_ref, wdn_ref, bdn_ref, ys_ref):
    @pl.when(pl.program_id(0) < nu_ref[0])
    def _():
        x = xs_ref[...].astype(BF16)
        gu = _dot(x, wgu_ref[0]) + bgu_ref[0]
        gate = jnp.minimum(gu[:, :D_EXPERT], SWIGLU_LIMIT)
        up = jnp.clip(gu[:, D_EXPERT:], -SWIGLU_LIMIT, SWIGLU_LIMIT)
        act = (up + 1.0) * (gate * jax.nn.sigmoid(SWIGLU_ALPHA * gate))
        ys_ref[...] = _dot(act.astype(BF16), wdn_ref[0]) + bdn_ref[0]

    @pl.when(pl.program_id(0) >= nu_ref[0])
    def _():
        ys_ref[...] = jnp.zeros_like(ys_ref)


def _experts(tile_expert, n_used, xs, wgu, bgu, wdn, bdn):
    cap = xs.shape[0]
    n_tiles = cap // EXPERT_TILE
    row = lambda i, te, nu: (jnp.minimum(i, nu[0] - 1), 0)
    by_expert = lambda i, te, nu: (te[i], 0, 0)
    return pl.pallas_call(
        _expert_kernel,
        grid_spec=pltpu.PrefetchScalarGridSpec(
            num_scalar_prefetch=2,
            grid=(n_tiles,),
            in_specs=[
                pl.BlockSpec((EXPERT_TILE, D_MODEL), row),
                pl.BlockSpec((1, D_MODEL, 2 * D_EXPERT), by_expert),
                pl.BlockSpec((1, 1, 2 * D_EXPERT), by_expert),
                pl.BlockSpec((1, D_EXPERT, D_MODEL), by_expert),
                pl.BlockSpec((1, 1, D_MODEL), by_expert),
            ],
            out_specs=pl.BlockSpec((EXPERT_TILE, D_MODEL), lambda i, te, nu: (i, 0)),
        ),
        out_shape=jax.ShapeDtypeStruct((cap, D_MODEL), F32),
        compiler_params=_params(48, ("arbitrary",)),
        name="experts",
    )(tile_expert, n_used, xs, wgu, bgu, wdn, bdn)


def _combine_kernel(pos_ref, x1_ref, gt_ref, ys_ref, o_ref, buf, sem):
    tm = x1_ref.shape[0]

    def body(t, carry):
        for k in range(TOP_K):
            p = pos_ref[k, t]
            pltpu.make_async_copy(ys_ref.at[pl.ds(p, 1), :], buf.at[k, pl.ds(t, 1), :], sem).start()
        return carry

    lax.fori_loop(0, tm, body, 0, unroll=8)
    for k in range(TOP_K):
        pltpu.make_async_copy(ys_ref.at[pl.ds(0, tm), :], buf.at[k], sem).wait()
    acc = x1_ref[...]
    g = gt_ref[...]
    for k in range(TOP_K):
        acc = acc + g[:, k:k + 1] * buf[k]
    o_ref[...] = acc


def _combine(pos4, x1, gt, ys):
    t = x1.shape[0]
    tm = TM_COMBINE
    return pl.pallas_call(
        _combine_kernel,
        grid=(t // tm,),
        in_specs=[
            pl.BlockSpec((TOP_K, tm), lambda i: (0, i), memory_space=pltpu.SMEM),
            pl.BlockSpec((tm, D_MODEL), lambda i: (i, 0)),
            pl.BlockSpec((tm, 8), lambda i: (i, 0)),
            pl.BlockSpec(memory_space=pl.ANY),
        ],
        out_specs=pl.BlockSpec((tm, D_MODEL), lambda i: (i, 0)),
        out_shape=jax.ShapeDtypeStruct((t, D_MODEL), F32),
        scratch_shapes=[pltpu.VMEM((TOP_K, tm, D_MODEL), F32), pltpu.SemaphoreType.DMA(())],
        compiler_params=_params(32, ("arbitrary",)),
        name="combine",
    )(pos4, x1, gt, ys)


def _layer(x, positions, norm1_g, w_in, q_norm_g, k_norm_g, sgu_ln_g, sgu_ln_b, w_spatial, b_spatial,
           w_branch_a, w_branch_b, w_out, norm2_g, w_router, b_router, w_gate_up, b_gate_up, w_down, b_down):
    b, seq, _ = x.shape
    t = b * seq
    x2 = x.reshape(t, D_MODEL)

    half = HEAD_DIM // 2
    inv_freq = ROPE_THETA ** (-jnp.arange(half, dtype=F32) / half)
    invf = jnp.tile(inv_freq, LANES // half).reshape(1, LANES)
    qg = jnp.tile(q_norm_g, LANES // HEAD_DIM).reshape(1, LANES)
    kg = jnp.tile(k_norm_g, LANES // HEAD_DIM).reshape(1, LANES)
    n_pairs = SGU_WIDTH // LANES
    wcat = w_spatial.reshape(n_pairs, 2, SGU_CHUNK, SGU_CHUNK).transpose(0, 2, 1, 3).reshape(n_pairs, SGU_CHUNK, 2 * SGU_CHUNK)
    bcat = jnp.repeat(b_spatial.reshape(n_pairs, 2, SGU_CHUNK).transpose(0, 2, 1), SGU_GROUP_DIM, axis=2)
    wr_t = w_router.T
    wr_hi = wr_t.astype(BF16)
    wr_lo = (wr_t - wr_hi.astype(F32)).astype(BF16)

    qkvz, gates = _inproj(x2, norm1_g.reshape(1, D_MODEL), w_in.astype(BF16))
    attn = _attention(qkvz.reshape(b, seq, QKVZ_WIDTH), positions.reshape(b, seq, 1), invf, qg, kg)
    x1, hf, e4, rank4, gt, counts = _post(
        attn.reshape(t, ATTN_WIDTH), qkvz, gates, x2,
        w_branch_a.astype(BF16), w_branch_b.astype(BF16), w_out.astype(BF16),
        sgu_ln_g.reshape(1, SGU_WIDTH), sgu_ln_b.reshape(1, SGU_WIDTH), wcat.astype(BF16), bcat,
        norm2_g.reshape(1, D_MODEL), wr_hi, wr_lo, b_router.reshape(N_EXPERTS, 1))

    n_assign = t * TOP_K
    cap = n_assign + N_EXPERTS * EXPERT_TILE
    n_tiles = cap // EXPERT_TILE
    cnt = counts[:, 0]
    padded = (cnt + EXPERT_TILE - 1) // EXPERT_TILE * EXPERT_TILE
    pend = jnp.cumsum(padded)
    pstart = (pend - padded).astype(I32)
    tile_expert = jnp.sum(jnp.arange(n_tiles, dtype=I32)[:, None] * EXPERT_TILE >= pend[None, :], axis=1)
    tile_expert = jnp.minimum(tile_expert, N_EXPERTS - 1).astype(I32)
    n_used = (pend[-1:] // EXPERT_TILE).astype(I32)

    pos4 = _positions(pstart, e4, rank4)
    xs = _dispatch(pos4, hf, cap)
    ys = _experts(tile_expert, n_used, xs,
                  w_gate_up.astype(BF16), b_gate_up.reshape(N_EXPERTS, 1, 2 * D_EXPERT),
                  w_down.astype(BF16), b_down.reshape(N_EXPERTS, 1, D_MODEL))
    out = _combine(pos4, x1, gt, ys)
    return out.reshape(b, seq, D_MODEL)


def kernel(x, positions, norm1_g, w_in, q_norm_g, k_norm_g, sgu_ln_g, sgu_ln_b, w_spatial, b_spatial, w_branch_a, w_branch_b, w_out, norm2_g, w_router, b_router, w_gate_up, b_gate_up, w_down, b_down):
    for layer in range(norm1_g.shape[0]):
        x = _layer(x, positions, norm1_g[layer], w_in[layer], q_norm_g[layer], k_norm_g[layer],
                   sgu_ln_g[layer], sgu_ln_b[layer], w_spatial[layer], b_spatial[layer],
                   w_branch_a[layer], w_branch_b[layer], w_out[layer], norm2_g[layer],
                   w_router[layer], b_router[layer], w_gate_up[layer], b_gate_up[layer],
                   w_down[layer], b_down[layer])
    return x
```

```python
import functools

import jax
import jax.numpy as jnp
from jax import lax
from jax.experimental import pallas as pl
from jax.experimental.pallas import tpu as pltpu

F32 = jnp.float32
BF16 = jnp.bfloat16
I32 = jnp.int32

D_MODEL = 1024
HEAD_DIM = 64
ATTN_GROUPS = ((128, 1), (512, 4), (2048, 16))
N_GROUPS = len(ATTN_GROUPS)
GROUP_WIDTH = 256
ATTN_WIDTH = N_GROUPS * GROUP_WIDTH
BLK = 128
ROPE_THETA = 10000.0
SGU_CHUNK = 128
SGU_GROUP_DIM = 64
SGU_WIDTH = 768
QKVZ_WIDTH = 3 * ATTN_WIDTH + 2 * SGU_WIDTH
GATE_WIDTH = 2 * D_MODEL
N_EXPERTS = 32
TOP_K = 4
D_EXPERT = D_MODEL
SWIGLU_LIMIT = 7.0
SWIGLU_ALPHA = 1.702
EPS = 1e-6
NEG_INF = -1e30
LANES = 128

TM_INPROJ = 512
TM_POST = 512
TM_ROUTE = 256
ROUTE_W = 48
SUBLANES = 8
EXPERT_TILE = 512
XS_WIDTH = D_MODEL + LANES
ROW_CHUNK = 256

_NT = (((1,), (1,)), ((), ()))
_TN = (((0,), (0,)), ((), ()))


def _dot(a, b):
    return jnp.dot(a, b, preferred_element_type=F32)


def _dot_nt(a, b):
    return lax.dot_general(a, b, _NT, preferred_element_type=F32)


def _params(vmem_mb, sem):
    return pltpu.CompilerParams(dimension_semantics=sem, vmem_limit_bytes=vmem_mb * 1024 * 1024)


def _inproj_kernel(x_ref, g_ref, w_ref, oa_ref, og_ref):
    x = x_ref[...]
    y = x * lax.rsqrt(jnp.mean(x * x, axis=-1, keepdims=True) + EPS)
    h = (y * g_ref[...]).astype(BF16)
    for lo in range(0, QKVZ_WIDTH, 768):
        oa_ref[:, lo:lo + 768] = _dot(h, w_ref[:, lo:lo + 768]).astype(BF16)
    for lo in range(0, GATE_WIDTH, 1024):
        og_ref[:, lo:lo + 1024] = _dot(h, w_ref[:, QKVZ_WIDTH + lo:QKVZ_WIDTH + lo + 1024]).astype(BF16)


def _inproj(x2, g, w):
    t = x2.shape[0]
    n = w.shape[1]
    return pl.pallas_call(
        _inproj_kernel,
        grid=(t // TM_INPROJ,),
        in_specs=[
            pl.BlockSpec((TM_INPROJ, D_MODEL), lambda i: (i, 0)),
            pl.BlockSpec((1, D_MODEL), lambda i: (0, 0)),
            pl.BlockSpec((D_MODEL, n), lambda i: (0, 0), pipeline_mode=pl.Buffered(1)),
        ],
        out_specs=[
            pl.BlockSpec((TM_INPROJ, QKVZ_WIDTH), lambda i: (i, 0)),
            pl.BlockSpec((TM_INPROJ, GATE_WIDTH), lambda i: (i, 0)),
        ],
        out_shape=[
            jax.ShapeDtypeStruct((t, QKVZ_WIDTH), BF16),
            jax.ShapeDtypeStruct((t, GATE_WIDTH), BF16),
        ],
        compiler_params=_params(48, ("arbitrary",)),
        name="inproj",
    )(x2, g, w)


def _attn_prep(src_ref, dst_ref, gain_ref, cos_s, sin_s, nat_s, seq, d, mode):
    lane = lax.broadcasted_iota(I32, (ROW_CHUNK, LANES), 1)
    first_half = (lane % HEAD_DIM) < (HEAD_DIM // 2)
    bi = lax.broadcasted_iota(I32, (LANES, LANES), 0) // HEAD_DIM
    bj = lax.broadcasted_iota(I32, (LANES, LANES), 1) // HEAD_DIM
    head_sum = jnp.where(bi == bj, 1.0, 0.0).astype(BF16)

    def body(i, carry):
        r0 = pl.multiple_of(i * ROW_CHUNK, ROW_CHUNK)
        for c in range(GROUP_WIDTH // LANES):
            x = src_ref[0, pl.ds(r0, ROW_CHUNK), c * LANES:(c + 1) * LANES].astype(F32)
            if mode == "v":
                y = x
            else:
                ss = x * x
                hi = ss.astype(BF16)
                lo = (ss - hi.astype(F32)).astype(BF16)
                ms = (_dot(hi, head_sum) + _dot(lo, head_sum)) * (1.0 / HEAD_DIM)
                y = x * lax.rsqrt(ms + EPS) * gain_ref[...]
                partner = jnp.where(first_half, pltpu.roll(y, LANES - HEAD_DIM // 2, 1), pltpu.roll(y, HEAD_DIM // 2, 1))
                y = y * cos_s[pl.ds(r0, ROW_CHUNK), :] + partner * sin_s[pl.ds(r0, ROW_CHUNK), :]
                if mode == "q":
                    y = y * (HEAD_DIM ** -0.5)
            if d == 1:
                dst_ref[pl.ds(r0, ROW_CHUNK), c * LANES:(c + 1) * LANES] = y.astype(BF16)
            else:
                nat_s[c, pl.ds(r0, ROW_CHUNK), :] = y
        return carry

    lax.fori_loop(0, seq // ROW_CHUNK, body, 0, unroll=2)
    if d > 1:
        sub = seq // d
        for r in range(d):
            for c in range(GROUP_WIDTH // LANES):
                dst_ref[r * sub:(r + 1) * sub, c * LANES:(c + 1) * LANES] = nat_s[c, pl.ds(r, sub, stride=d), :].astype(BF16)


def _attn_blocks(qr_s, kr_s, vr_s, ores_s, lres_s, seq, d):
    sub = seq // d
    nb = sub // BLK
    win = 2 * BLK if nb > 1 else BLK
    log2d = d.bit_length() - 1
    lane_head = lax.broadcasted_iota(I32, (BLK, GROUP_WIDTH), 1) // HEAD_DIM
    qi = lax.broadcasted_iota(I32, (BLK, win), 0)
    kj = lax.broadcasted_iota(I32, (BLK, win), 1)
    rel = qi - kj

    def body(idx, carry):
        r = idx & (d - 1)
        n = idx >> log2d
        row0 = pl.multiple_of(r * sub + n * BLK, BLK)
        if nb > 1:
            kn = jnp.maximum(n - 1, 0)
            k0 = pl.multiple_of(r * sub + kn * BLK, BLK)
            dist = rel + (n - kn) * BLK
        else:
            k0 = row0
            dist = rel
        valid = lax.bitcast_convert_type(dist, jnp.uint32) <= jnp.uint32(BLK)
        q = qr_s[pl.ds(row0, BLK), :]
        kw = kr_s[pl.ds(k0, win), :]
        vw = vr_s[pl.ds(k0, win), :]
        zero = jnp.zeros_like(q)
        qm = jnp.concatenate([jnp.where(lane_head == h, q, zero) for h in range(4)], axis=0)
        s = _dot_nt(qm, kw)
        ps, ms, ls = [], [], []
        for h in range(4):
            sh = jnp.where(valid, s[h * BLK:(h + 1) * BLK], NEG_INF)
            m = jnp.max(sh, axis=-1, keepdims=True)
            p = jnp.exp(sh - m)
            ls.append(jnp.sum(p, axis=-1, keepdims=True))
            ms.append(m)
            ps.append(p.astype(BF16))
        pv = _dot(jnp.concatenate(ps, axis=0), vw)
        o = jnp.zeros((BLK, GROUP_WIDTH), F32)
        lse = jnp.zeros((BLK, GROUP_WIDTH), F32)
        for h in range(4):
            sel = lane_head == h
            o = jnp.where(sel, pv[h * BLK:(h + 1) * BLK] * (1.0 / ls[h]), o)
            lse = jnp.where(sel, ms[h] + jnp.log(ls[h]), lse)
        ores_s[pl.ds(row0, BLK), :] = o
        lres_s[pl.ds(row0, BLK), :] = lse
        return carry

    lax.fori_loop(0, seq // BLK, body, 0, unroll=2)


def _attn_kernel(q_ref, k_ref, v_ref, pos_ref, invf_ref, qg_ref, kg_ref, o_ref,
                 cos_s, sin_s, nat_s, qr_s, kr_s, vr_s, ores_s, lres_s, onat_s, lnat_s, *, seq):
    g = pl.program_id(1)

    @pl.when(g == 0)
    def _():
        lane = lax.broadcasted_iota(I32, (1, LANES), 1)
        sign = jnp.where((lane % HEAD_DIM) < (HEAD_DIM // 2), -1.0, 1.0)

        def body(i, carry):
            r0 = pl.multiple_of(i * ROW_CHUNK, ROW_CHUNK)
            ang = pos_ref[0, pl.ds(r0, ROW_CHUNK), :].astype(F32) * invf_ref[...]
            cos_s[pl.ds(r0, ROW_CHUNK), :] = jnp.cos(ang)
            sin_s[pl.ds(r0, ROW_CHUNK), :] = jnp.sin(ang) * sign
            return carry

        lax.fori_loop(0, seq // ROW_CHUNK, body, 0)

    for gi, (_, d) in enumerate(ATTN_GROUPS):

        @pl.when(g == gi)
        def _(gi=gi, d=d):
            _attn_prep(q_ref, qr_s, qg_ref, cos_s, sin_s, nat_s, seq, d, "q")
            _attn_prep(k_ref, kr_s, kg_ref, cos_s, sin_s, nat_s, seq, d, "k")
            _attn_prep(v_ref, vr_s, None, cos_s, sin_s, nat_s, seq, d, "v")
            _attn_blocks(qr_s, kr_s, vr_s, ores_s, lres_s, seq, d)
            sub = seq // d
            for r in range(d):
                for c in range(GROUP_WIDTH // LANES):
                    t = gi * (GROUP_WIDTH // LANES) + c
                    onat_s[t, pl.ds(r, sub, stride=d), :] = ores_s[r * sub:(r + 1) * sub, c * LANES:(c + 1) * LANES]
                    lnat_s[t, pl.ds(r, sub, stride=d), :] = lres_s[r * sub:(r + 1) * sub, c * LANES:(c + 1) * LANES]

    @pl.when(g == N_GROUPS - 1)
    def _():
        tiles = GROUP_WIDTH // LANES

        def body(i, carry):
            r0 = pl.multiple_of(i * ROW_CHUNK, ROW_CHUNK)
            for c in range(tiles):
                ls = [lnat_s[gi * tiles + c, pl.ds(r0, ROW_CHUNK), :] for gi in range(N_GROUPS)]
                m = jnp.maximum(jnp.maximum(ls[0], ls[1]), ls[2])
                es = [jnp.exp(l - m) for l in ls]
                inv = 1.0 / (es[0] + es[1] + es[2])
                for gi in range(N_GROUPS):
                    t = gi * tiles + c
                    o_ref[0, pl.ds(r0, ROW_CHUNK), t * LANES:(t + 1) * LANES] = (
                        onat_s[t, pl.ds(r0, ROW_CHUNK), :] * (es[gi] * inv)).astype(BF16)
            return carry

        lax.fori_loop(0, seq // ROW_CHUNK, body, 0)


def _attention(qkvz3, pos3, invf, qg, kg):
    b, seq, _ = qkvz3.shape
    tiles = GROUP_WIDTH // LANES
    nq = ATTN_WIDTH // GROUP_WIDTH
    return pl.pallas_call(
        functools.partial(_attn_kernel, seq=seq),
        grid=(b, N_GROUPS),
        in_specs=[
            pl.BlockSpec((1, seq, GROUP_WIDTH), lambda i, g: (i, 0, g)),
            pl.BlockSpec((1, seq, GROUP_WIDTH), lambda i, g: (i, 0, nq + g)),
            pl.BlockSpec((1, seq, GROUP_WIDTH), lambda i, g: (i, 0, 2 * nq + g)),
            pl.BlockSpec((1, seq, 1), lambda i, g: (i, 0, 0)),
            pl.BlockSpec((1, LANES), lambda i, g: (0, 0)),
            pl.BlockSpec((1, LANES), lambda i, g: (0, 0)),
            pl.BlockSpec((1, LANES), lambda i, g: (0, 0)),
        ],
        out_specs=pl.BlockSpec((1, seq, ATTN_WIDTH), lambda i, g: (i, 0, 0)),
        out_shape=jax.ShapeDtypeStruct((b, seq, ATTN_WIDTH), BF16),
        scratch_shapes=[
            pltpu.VMEM((seq, LANES), F32),
            pltpu.VMEM((seq, LANES), F32),
            pltpu.VMEM((tiles, seq, LANES), F32),
            pltpu.VMEM((seq, GROUP_WIDTH), BF16),
            pltpu.VMEM((seq, GROUP_WIDTH), BF16),
            pltpu.VMEM((seq, GROUP_WIDTH), BF16),
            pltpu.VMEM((seq, GROUP_WIDTH), F32),
            pltpu.VMEM((seq, GROUP_WIDTH), F32),
            pltpu.VMEM((N_GROUPS * tiles, seq, LANES), F32),
            pltpu.VMEM((N_GROUPS * tiles, seq, LANES), F32),
        ],
        compiler_params=_params(48, ("arbitrary", "arbitrary")),
        name="attention",
    )(qkvz3, qkvz3, qkvz3, pos3, invf, qg, kg)


def _gelu(x):
    return 0.5 * x * (1.0 + lax.erf(x * 0.7071067811865476))


def _post_kernel(attn_ref, u_ref, vz_ref, ga_ref, gb_ref, x_ref, wa_ref, wb_ref, wo_ref,
                 lng_ref, lnb_ref, wcat_ref, bcat_ref, n2g_ref, wrh_ref, wrl_ref, br_ref,
                 x1_ref, hf_ref, e_ref, g_ref, cst_ref, cnt_ref, sgu_s, carry_s):
    tm = x_ref.shape[0]
    step = pl.program_id(0)

    @pl.when(step == 0)
    def _():
        carry_s[...] = jnp.zeros_like(carry_s)

    lane = lax.broadcasted_iota(I32, (SGU_CHUNK, LANES), 1)
    low = lane < SGU_GROUP_DIM
    trow = lax.broadcasted_iota(I32, (SGU_CHUNK, 2 * SGU_CHUNK), 0)
    tcol = lax.broadcasted_iota(I32, (SGU_CHUNK, 2 * SGU_CHUNK), 1) % SGU_CHUNK
    causal = tcol <= trow

    def chunk(c, carry):
        r0 = pl.multiple_of(c * SGU_CHUNK, SGU_CHUNK)
        u = _gelu(u_ref[pl.ds(r0, SGU_CHUNK), :].astype(F32))
        v = _gelu(vz_ref[pl.ds(r0, SGU_CHUNK), :].astype(F32))
        mu = jnp.mean(v, axis=-1, keepdims=True)
        vc = v - mu
        vn = vc * lax.rsqrt(jnp.mean(vc * vc, axis=-1, keepdims=True) + EPS)
        vn = (vn * lng_ref[...] + lnb_ref[...]).astype(BF16)
        zero = jnp.zeros((SGU_CHUNK, LANES), BF16)
        for j in range(SGU_WIDTH // LANES):
            vt = vn[:, j * LANES:(j + 1) * LANES]
            rhs = jnp.concatenate([jnp.where(low, vt, zero), jnp.where(low, zero, vt)], axis=0)
            wj = jnp.where(causal, wcat_ref[j], jnp.zeros((), BF16))
            mixed = _dot(wj, rhs) + bcat_ref[j]
            sgu_s[pl.ds(r0, SGU_CHUNK), j * LANES:(j + 1) * LANES] = (u[:, j * LANES:(j + 1) * LANES] * mixed).astype(BF16)
        return carry

    lax.fori_loop(0, tm // SGU_CHUNK, chunk, 0)

    y_a = _dot(attn_ref[...], wa_ref[...])
    y_b = _dot(sgu_s[...], wb_ref[...])
    merged = jax.nn.sigmoid(ga_ref[...].astype(F32)) * y_a + jax.nn.sigmoid(gb_ref[...].astype(F32)) * y_b
    x1 = x_ref[...] + _dot(merged.astype(BF16), wo_ref[...])
    x1_ref[...] = x1

    hf = x1 * lax.rsqrt(jnp.mean(x1 * x1, axis=-1, keepdims=True) + EPS) * n2g_ref[...]
    hi = hf.astype(BF16)
    hf_ref[...] = hi
    lo = (hf - hi.astype(F32)).astype(BF16)
    logits = _dot_nt(wrh_ref[...], hi) + _dot_nt(wrl_ref[...], hi) + _dot_nt(wrh_ref[...], lo) + br_ref[...]

    eio = lax.broadcasted_iota(I32, (N_EXPERTS, tm), 0)
    work = logits
    top_v, onehots = [], []
    for k in range(TOP_K):
        m = jnp.max(work, axis=0, keepdims=True)
        idx = jnp.min(jnp.where(work == m, eio, N_EXPERTS), axis=0, keepdims=True)
        oh = eio == idx
        work = jnp.where(oh, -jnp.inf, work)
        top_v.append(m)
        onehots.append(oh)
        e_ref[k:k + 1, :] = idx
    ex = [jnp.exp(v - top_v[0]) for v in top_v]
    inv = 1.0 / (ex[0] + ex[1] + ex[2] + ex[3])
    for k in range(TOP_K):
        g_ref[k:k + 1, :] = ex[k] * inv

    sel = jnp.zeros((N_EXPERTS, tm), F32)
    for oh in onehots:
        sel = jnp.where(oh, 1.0, sel)
    lane = lax.broadcasted_iota(I32, (N_EXPERTS, LANES), 1)
    run = carry_s[...]
    cst = jnp.zeros((N_EXPERTS, LANES), F32)
    for s in range(tm // TM_ROUTE):
        cst = jnp.where(lane == s, run, cst)
        run = run + jnp.sum(sel[:, s * TM_ROUTE:(s + 1) * TM_ROUTE], axis=1, keepdims=True)
    cst_ref[0] = cst.astype(I32)
    carry_s[...] = run
    cnt_ref[...] = run.astype(I32)


def _post(attn2, qkvz2, gates2, x2, wa, wb, wo, lng, lnb, wcat, bcat, n2g, wrh, wrl, br):
    t = x2.shape[0]
    tm = TM_POST
    nu = 3 * ATTN_WIDTH // SGU_WIDTH
    const = lambda *shape: pl.BlockSpec(shape, lambda i: (0,) * len(shape), pipeline_mode=pl.Buffered(1))
    return pl.pallas_call(
        _post_kernel,
        grid=(t // tm,),
        in_specs=[
            pl.BlockSpec((tm, ATTN_WIDTH), lambda i: (i, 0)),
            pl.BlockSpec((tm, SGU_WIDTH), lambda i: (i, nu)),
            pl.BlockSpec((tm, SGU_WIDTH), lambda i: (i, nu + 1)),
            pl.BlockSpec((tm, D_MODEL), lambda i: (i, 0)),
            pl.BlockSpec((tm, D_MODEL), lambda i: (i, 1)),
            pl.BlockSpec((tm, D_MODEL), lambda i: (i, 0)),
            const(ATTN_WIDTH, D_MODEL),
            const(SGU_WIDTH, D_MODEL),
            const(D_MODEL, D_MODEL),
            const(1, SGU_WIDTH),
            const(1, SGU_WIDTH),
            const(SGU_WIDTH // LANES, SGU_CHUNK, 2 * SGU_CHUNK),
            const(SGU_WIDTH // LANES, SGU_CHUNK, LANES),
            const(1, D_MODEL),
            const(N_EXPERTS, D_MODEL),
            const(N_EXPERTS, D_MODEL),
            const(N_EXPERTS, 1),
        ],
        out_specs=[
            pl.BlockSpec((tm, D_MODEL), lambda i: (i, 0)),
            pl.BlockSpec((tm, D_MODEL), lambda i: (i, 0)),
            pl.BlockSpec((TOP_K, tm), lambda i: (0, i)),
            pl.BlockSpec((TOP_K, tm), lambda i: (0, i)),
            pl.BlockSpec((1, N_EXPERTS, LANES), lambda i: (i, 0, 0)),
            pl.BlockSpec((N_EXPERTS, LANES), lambda i: (0, 0)),
        ],
        out_shape=[
            jax.ShapeDtypeStruct((t, D_MODEL), F32),
            jax.ShapeDtypeStruct((t, D_MODEL), BF16),
            jax.ShapeDtypeStruct((TOP_K, t), I32),
            jax.ShapeDtypeStruct((TOP_K, t), F32),
            jax.ShapeDtypeStruct((t // tm, N_EXPERTS, LANES), I32),
            jax.ShapeDtypeStruct((N_EXPERTS, LANES), I32),
        ],
        scratch_shapes=[
            pltpu.VMEM((tm, SGU_WIDTH), BF16),
            pltpu.VMEM((N_EXPERTS, LANES), F32),
        ],
        compiler_params=_params(48, ("arbitrary",)),
        name="post",
    )(attn2, qkvz2, qkvz2, gates2, gates2, x2, wa, wb, wo, lng, lnb, wcat, bcat, n2g, wrh, wrl, br)


def _route_onehot(e_ref, g_ref, tm):
    e = e_ref[...]
    eio = lax.broadcasted_iota(I32, (N_EXPERTS, tm), 0)
    sel = jnp.zeros((N_EXPERTS, tm), F32)
    gm = None
    if g_ref is not None:
        g = g_ref[...]
        gm = jnp.zeros((N_EXPERTS, tm), F32)
    for k in range(TOP_K):
        oh = eio == e[k:k + 1, :]
        sel = jnp.where(oh, 1.0, sel)
        if g_ref is not None:
            gm = jnp.where(oh, g[k:k + 1, :], gm)
    ti = lax.broadcasted_iota(I32, (tm, tm), 0)
    tj = lax.broadcasted_iota(I32, (tm, tm), 1)
    before = jnp.where(ti < tj, 1.0, 0.0).astype(BF16)
    slot = _dot(sel.astype(BF16), before)
    return jnp.where(sel > 0.0, slot, -1.0), gm


def _strip_selectors(slot, base):
    tm = slot.shape[1]
    jrow = lax.broadcasted_iota(I32, (ROUTE_W, tm), 0).astype(F32) + base
    return [jrow == slot[x:x + 1, :] for x in range(N_EXPERTS)]


def _dispatch_kernel(start_ref, cnt_ref, nchunk_ref, za_ref, zb_ref, hf_ref, e_ref, g_ref, xs_ref,
                     obuf, obuf2, zbuf, sem, sem2, semz):
    tm = hf_ref.shape[0]
    i = pl.program_id(0)
    last = pl.num_programs(0) - 1
    cur = lax.rem(i, 2)
    slot, gm = _route_onehot(e_ref, g_ref, tm)

    def build(base, dst):
        masks = _strip_selectors(slot, base)
        pt = jnp.concatenate([jnp.where(m, 1.0, 0.0).astype(BF16) for m in masks], axis=0)
        dst[:, 0:D_MODEL] = _dot(pt, hf_ref[...])
        gate = jnp.concatenate(
            [jnp.sum(jnp.where(m, gm[x:x + 1, :], 0.0), axis=1, keepdims=True) for x, m in enumerate(masks)], axis=0)
        dst[:, D_MODEL:XS_WIDTH] = jnp.broadcast_to(gate, (N_EXPERTS * ROUTE_W, LANES))

    def strip_copy(src, x, row, s):
        return pltpu.make_async_copy(src.at[pl.ds(x * ROUTE_W, ROUTE_W), :], xs_ref.at[pl.ds(row, ROUTE_W), :], s)

    build(0.0, obuf.at[cur])

    @pl.when(i > 0)
    def _():
        for x in range(N_EXPERTS):
            strip_copy(obuf.at[1 - cur], x, 0, sem.at[1 - cur]).wait()

    for x in range(N_EXPERTS):
        row = pl.multiple_of(start_ref[i * N_EXPERTS + x], SUBLANES)
        strip_copy(obuf.at[cur], x, row, sem.at[cur]).start()

    def extra(c, carry):
        build((c * ROUTE_W).astype(F32), obuf2)
        for x in range(N_EXPERTS):

            @pl.when(cnt_ref[i * N_EXPERTS + x] > c * ROUTE_W)
            def _(x=x):
                row = pl.multiple_of(start_ref[i * N_EXPERTS + x] + c * ROUTE_W, SUBLANES)
                cp = strip_copy(obuf2, x, row, sem2)
                cp.start()
                cp.wait()

        return carry

    lax.fori_loop(1, nchunk_ref[i], extra, 0)

    @pl.when(i == last)
    def _():
        for x in range(N_EXPERTS):
            strip_copy(obuf.at[cur], x, 0, sem.at[cur]).wait()
        zbuf[...] = jnp.zeros_like(zbuf)

        def zero_copy(row, nrows):
            row = pl.multiple_of(row, SUBLANES)
            return pltpu.make_async_copy(zbuf.at[pl.ds(0, nrows), :], xs_ref.at[pl.ds(row, nrows), :], semz)

        def zero_region(z, act):
            n = zb_ref[z] - za_ref[z]
            big = n // ROUTE_W
            mid = za_ref[z] + big * ROUTE_W
            small = (n - big * ROUTE_W) // SUBLANES
            lax.fori_loop(0, big, lambda m, c: (act(zero_copy(za_ref[z] + m * ROUTE_W, ROUTE_W)), c)[1], 0)
            lax.fori_loop(0, small, lambda m, c: (act(zero_copy(mid + m * SUBLANES, SUBLANES)), c)[1], 0)

        for z in range(N_EXPERTS + 1):
            zero_region(z, lambda cp: cp.start())
        for z in range(N_EXPERTS + 1):
            zero_region(z, lambda cp: cp.wait())


def _dispatch(start, cnt, nchunk, za, zb, hf, e4, g4, cap):
    t = hf.shape[0]
    tm = TM_ROUTE
    rows = N_EXPERTS * ROUTE_W
    tok = lambda i, *_: (0, i)
    return pl.pallas_call(
        _dispatch_kernel,
        grid_spec=pltpu.PrefetchScalarGridSpec(
            num_scalar_prefetch=5,
            grid=(t // tm,),
            in_specs=[
                pl.BlockSpec((tm, D_MODEL), lambda i, *_: (i, 0)),
                pl.BlockSpec((TOP_K, tm), tok),
                pl.BlockSpec((TOP_K, tm), tok),
            ],
            out_specs=pl.BlockSpec(memory_space=pl.ANY),
            scratch_shapes=[
                pltpu.VMEM((2, rows, XS_WIDTH), F32),
                pltpu.VMEM((rows, XS_WIDTH), F32),
                pltpu.VMEM((ROUTE_W, XS_WIDTH), F32),
                pltpu.SemaphoreType.DMA((2,)),
                pltpu.SemaphoreType.DMA(()),
                pltpu.SemaphoreType.DMA(()),
            ],
        ),
        out_shape=jax.ShapeDtypeStruct((cap, XS_WIDTH), F32),
        compiler_params=_params(48, ("arbitrary",)),
        name="dispatch",
    )(start, cnt, nchunk, za, zb, hf, e4, g4)


def _expert_kernel(te_ref, nu_ref, xs_ref, wgu_ref, bgu_ref, wdn_ref, bdn_ref, ys_ref, wgu_s, wdn_s):
    i = pl.program_id(0)

    @pl.when(i < nu_ref[0])
    def _():
        @pl.when((i == 0) | (te_ref[i] != te_ref[jnp.maximum(i - 1, 0)]))
        def _():
            wgu_s[...] = wgu_ref[0].astype(BF16)
            wdn_s[...] = wdn_ref[0].astype(BF16)

        x = xs_ref[:, 0:D_MODEL].astype(BF16)
        route_gate = xs_ref[:, D_MODEL:D_MODEL + 1]
        gu = _dot(x, wgu_s[...]) + bgu_ref[0]
        gate = jnp.minimum(gu[:, :D_EXPERT], SWIGLU_LIMIT)
        up = jnp.clip(gu[:, D_EXPERT:], -SWIGLU_LIMIT, SWIGLU_LIMIT)
        act = (up + 1.0) * (gate * jax.nn.sigmoid(SWIGLU_ALPHA * gate))
        ys_ref[...] = (_dot(act.astype(BF16), wdn_s[...]) + bdn_ref[0]) * route_gate

    @pl.when(i >= nu_ref[0])
    def _():
        ys_ref[...] = jnp.zeros_like(ys_ref)


def _experts(tile_expert, n_used, xs, wgu, bgu, wdn, bdn):
    cap = xs.shape[0]
    n_tiles = cap // EXPERT_TILE
    row = lambda i, te, nu: (jnp.minimum(i, nu[0] - 1), 0)
    by_expert = lambda i, te, nu: (te[i], 0, 0)
    return pl.pallas_call(
        _expert_kernel,
        grid_spec=pltpu.PrefetchScalarGridSpec(
            num_scalar_prefetch=2,
            grid=(n_tiles,),
            in_specs=[
                pl.BlockSpec((EXPERT_TILE, XS_WIDTH), row),
                pl.BlockSpec((1, D_MODEL, 2 * D_EXPERT), by_expert),
                pl.BlockSpec((1, 1, 2 * D_EXPERT), by_expert),
                pl.BlockSpec((1, D_EXPERT, D_MODEL), by_expert),
                pl.BlockSpec((1, 1, D_MODEL), by_expert),
            ],
            out_specs=pl.BlockSpec((EXPERT_TILE, D_MODEL), lambda i, te, nu: (i, 0)),
            scratch_shapes=[
                pltpu.VMEM((D_MODEL, 2 * D_EXPERT), BF16),
                pltpu.VMEM((D_EXPERT, D_MODEL), BF16),
            ],
        ),
        out_shape=jax.ShapeDtypeStruct((cap, D_MODEL), F32),
        compiler_params=_params(56, ("arbitrary",)),
        name="experts",
    )(tile_expert, n_used, xs, wgu, bgu, wdn, bdn)


def _combine_kernel(start_ref, cnt_ref, nchunk_ref, x1_ref, e_ref, ys_ref, o_ref, sbuf, sbuf2, sem, sem2):
    tm = x1_ref.shape[0]
    i = pl.program_id(0)
    last = pl.num_programs(0) - 1
    cur = lax.rem(i, 2)

    def strip_copy(dst, x, row, s):
        return pltpu.make_async_copy(ys_ref.at[pl.ds(row, ROUTE_W), :], dst.at[pl.ds(x * ROUTE_W, ROUTE_W), :], s)

    def fetch(step, b):
        for x in range(N_EXPERTS):
            row = pl.multiple_of(start_ref[step * N_EXPERTS + x], SUBLANES)
            strip_copy(sbuf.at[b], x, row, sem.at[b]).start()

    @pl.when(i == 0)
    def _():
        fetch(i, cur)

    @pl.when(i < last)
    def _():
        fetch(i + 1, 1 - cur)

    slot, _ = _route_onehot(e_ref, None, tm)

    def gathered(base, src):
        pt = jnp.concatenate([jnp.where(m, 1.0, 0.0).astype(BF16) for m in _strip_selectors(slot, base)], axis=0)
        return lax.dot_general(pt, src[...].astype(BF16), _TN, preferred_element_type=F32)

    for x in range(N_EXPERTS):
        strip_copy(sbuf.at[cur], x, 0, sem.at[cur]).wait()
    o_ref[...] = x1_ref[...] + gathered(0.0, sbuf.at[cur])

    def extra(c, carry):
        sbuf2[...] = jnp.zeros_like(sbuf2)
        for x in range(N_EXPERTS):

            @pl.when(cnt_ref[i * N_EXPERTS + x] > c * ROUTE_W)
            def _(x=x):
                row = pl.multiple_of(start_ref[i * N_EXPERTS + x] + c * ROUTE_W, SUBLANES)
                cp = strip_copy(sbuf2, x, row, sem2)
                cp.start()
                cp.wait()

        o_ref[...] = o_ref[...] + gathered((c * ROUTE_W).astype(F32), sbuf2)
        return carry

    lax.fori_loop(1, nchunk_ref[i], extra, 0)


def _combine(start, cnt, nchunk, x1, e4, ys):
    t = x1.shape[0]
    tm = TM_ROUTE
    rows = N_EXPERTS * ROUTE_W
    return pl.pallas_call(
        _combine_kernel,
        grid_spec=pltpu.PrefetchScalarGridSpec(
            num_scalar_prefetch=3,
            grid=(t // tm,),
            in_specs=[
                pl.BlockSpec((tm, D_MODEL), lambda i, *_: (i, 0)),
                pl.BlockSpec((TOP_K, tm), lambda i, *_: (0, i)),
                pl.BlockSpec(memory_space=pl.ANY),
            ],
            out_specs=pl.BlockSpec((tm, D_MODEL), lambda i, *_: (i, 0)),
            scratch_shapes=[
                pltpu.VMEM((2, rows, D_MODEL), F32),
                pltpu.VMEM((rows, D_MODEL), F32),
                pltpu.SemaphoreType.DMA((2,)),
                pltpu.SemaphoreType.DMA(()),
            ],
        ),
        out_shape=jax.ShapeDtypeStruct((t, D_MODEL), F32),
        compiler_params=_params(48, ("arbitrary",)),
        name="combine",
    )(start, cnt, nchunk, x1, e4, ys)


def _layer(x, positions, norm1_g, w_in, q_norm_g, k_norm_g, sgu_ln_g, sgu_ln_b, w_spatial, b_spatial,
           w_branch_a, w_branch_b, w_out, norm2_g, w_router, b_router, w_gate_up, b_gate_up, w_down, b_down):
    b, seq, _ = x.shape
    t = b * seq
    x2 = x.reshape(t, D_MODEL)

    half = HEAD_DIM // 2
    inv_freq = ROPE_THETA ** (-jnp.arange(half, dtype=F32) / half)
    invf = jnp.tile(inv_freq, LANES // half).reshape(1, LANES)
    qg = jnp.tile(q_norm_g, LANES // HEAD_DIM).reshape(1, LANES)
    kg = jnp.tile(k_norm_g, LANES // HEAD_DIM).reshape(1, LANES)
    n_pairs = SGU_WIDTH // LANES
    wcat = w_spatial.reshape(n_pairs, 2, SGU_CHUNK, SGU_CHUNK).transpose(0, 2, 1, 3).reshape(n_pairs, SGU_CHUNK, 2 * SGU_CHUNK)
    bcat = jnp.repeat(b_spatial.reshape(n_pairs, 2, SGU_CHUNK).transpose(0, 2, 1), SGU_GROUP_DIM, axis=2)
    wr_t = w_router.T
    wr_hi = wr_t.astype(BF16)
    wr_lo = (wr_t - wr_hi.astype(F32)).astype(BF16)

    qkvz, gates = _inproj(x2, norm1_g.reshape(1, D_MODEL), w_in.astype(BF16))
    attn = _attention(qkvz.reshape(b, seq, QKVZ_WIDTH), positions.reshape(b, seq, 1), invf, qg, kg)
    x1, hf, e4, g4, cst, counts = _post(
        attn.reshape(t, ATTN_WIDTH), qkvz, gates, x2,
        w_branch_a.astype(BF16), w_branch_b.astype(BF16), w_out.astype(BF16),
        sgu_ln_g.reshape(1, SGU_WIDTH), sgu_ln_b.reshape(1, SGU_WIDTH), wcat.astype(BF16), bcat,
        norm2_g.reshape(1, D_MODEL), wr_hi, wr_lo, b_router.reshape(N_EXPERTS, 1))

    steps_per_tile = TM_POST // TM_ROUTE
    n_steps = t // TM_ROUTE
    before = cst[:, :, :steps_per_tile].transpose(0, 2, 1).reshape(n_steps, N_EXPERTS)
    total = counts[:, 0]
    cnt = jnp.concatenate([before[1:], total[None, :]], axis=0) - before
    cnt_al = (cnt + SUBLANES - 1) // SUBLANES * SUBLANES
    used = jnp.sum(cnt_al, axis=0)
    region = (used + ROUTE_W + EXPERT_TILE - 1) // EXPERT_TILE * EXPERT_TILE
    rend = jnp.cumsum(region)
    rstart = rend - region
    start = (rstart[None, :] + jnp.cumsum(cnt_al, axis=0) - cnt_al).astype(I32).reshape(-1)
    nchunk = jnp.maximum((jnp.max(cnt, axis=1) + ROUTE_W - 1) // ROUTE_W, 1).astype(I32)
    bound = t * TOP_K + n_steps * N_EXPERTS * (SUBLANES - 1) + N_EXPERTS * (ROUTE_W + EXPERT_TILE - 1)
    cap = (bound // EXPERT_TILE + 2) * EXPERT_TILE
    n_tiles = cap // EXPERT_TILE
    tile_expert = jnp.sum(jnp.arange(n_tiles, dtype=I32)[:, None] * EXPERT_TILE >= rend[None, :], axis=1)
    tile_expert = jnp.minimum(tile_expert, N_EXPERTS - 1).astype(I32)
    n_used = (rend[-1:] // EXPERT_TILE).astype(I32)
    za = jnp.concatenate([rstart + used, rend[-1:]]).astype(I32)
    zb = jnp.concatenate([rend, jnp.full((1,), cap, rend.dtype)]).astype(I32)
    cnt = cnt.astype(I32).reshape(-1)

    xs = _dispatch(start, cnt, nchunk, za, zb, hf, e4, g4, cap)
    ys = _experts(tile_expert, n_used, xs,
                  w_gate_up, b_gate_up.reshape(N_EXPERTS, 1, 2 * D_EXPERT),
                  w_down, b_down.reshape(N_EXPERTS, 1, D_MODEL))
    out = _combine(start, cnt, nchunk, x1, e4, ys)
    return out.reshape(b, seq, D_MODEL)


def kernel(x, positions, norm1_g, w_in, q_norm_g, k_norm_g, sgu_ln_g, sgu_ln_b, w_spatial, b_spatial, w_branch_a, w_branch_b, w_out, norm2_g, w_router, b_router, w_gate_up, b_gate_up, w_down, b_down):
    for layer in range(norm1_g.shape[0]):
        x = _layer(x, positions, norm1_g[layer], w_in[layer], q_norm_g[layer], k_norm_g[layer],
                   sgu_ln_g[layer], sgu_ln_b[layer], w_spatial[layer], b_spatial[layer],
                   w_branch_a[layer], w_branch_b[layer], w_out[layer], norm2_g[layer],
                   w_router[layer], b_router[layer], w_gate_up[layer], b_gate_up[layer],
                   w_down[layer], b_down[layer])
    return x
```

```python
import functools

import jax
import jax.numpy as jnp
from jax import lax
from jax.experimental import pallas as pl
from jax.experimental.pallas import tpu as pltpu

F32 = jnp.float32
BF16 = jnp.bfloat16
I32 = jnp.int32

D_MODEL = 1024
HEAD_DIM = 64
ATTN_GROUPS = ((128, 1), (512, 4), (2048, 16))
N_GROUPS = len(ATTN_GROUPS)
GROUP_WIDTH = 256
ATTN_WIDTH = N_GROUPS * GROUP_WIDTH
BLK = 128
ROPE_THETA = 10000.0
SGU_CHUNK = 128
SGU_GROUP_DIM = 64
SGU_WIDTH = 768
QKVZ_WIDTH = 3 * ATTN_WIDTH + 2 * SGU_WIDTH
GATE_WIDTH = 2 * D_MODEL
N_EXPERTS = 32
TOP_K = 4
D_EXPERT = D_MODEL
SWIGLU_LIMIT = 7.0
SWIGLU_ALPHA = 1.702
EPS = 1e-6
NEG_INF = -1e30
LANES = 128

TM_INPROJ = 512
TM_POST = 512
TM_ROUTE = 256
SUBLANES = 8
ROUTE_ROWS = TM_ROUTE * TOP_K + N_EXPERTS * SUBLANES
EXPERT_TILE = 512
XS_WIDTH = D_MODEL + LANES
ROW_CHUNK = 256

_NT = (((1,), (1,)), ((), ()))
_TN = (((0,), (0,)), ((), ()))


def _dot(a, b):
    return jnp.dot(a, b, preferred_element_type=F32)


def _dot_nt(a, b):
    return lax.dot_general(a, b, _NT, preferred_element_type=F32)


def _params(vmem_mb, sem):
    return pltpu.CompilerParams(dimension_semantics=sem, vmem_limit_bytes=vmem_mb * 1024 * 1024)


def _inproj_kernel(x_ref, g_ref, w_ref, oa_ref, og_ref):
    x = x_ref[...]
    y = x * lax.rsqrt(jnp.mean(x * x, axis=-1, keepdims=True) + EPS)
    h = (y * g_ref[...]).astype(BF16)
    for lo in range(0, QKVZ_WIDTH, 768):
        oa_ref[:, lo:lo + 768] = _dot(h, w_ref[:, lo:lo + 768]).astype(BF16)
    for lo in range(0, GATE_WIDTH, 1024):
        og_ref[:, lo:lo + 1024] = _dot(h, w_ref[:, QKVZ_WIDTH + lo:QKVZ_WIDTH + lo + 1024]).astype(BF16)


def _inproj(x2, g, w):
    t = x2.shape[0]
    n = w.shape[1]
    return pl.pallas_call(
        _inproj_kernel,
        grid=(t // TM_INPROJ,),
        in_specs=[
            pl.BlockSpec((TM_INPROJ, D_MODEL), lambda i: (i, 0)),
            pl.BlockSpec((1, D_MODEL), lambda i: (0, 0)),
            pl.BlockSpec((D_MODEL, n), lambda i: (0, 0), pipeline_mode=pl.Buffered(1)),
        ],
        out_specs=[
            pl.BlockSpec((TM_INPROJ, QKVZ_WIDTH), lambda i: (i, 0)),
            pl.BlockSpec((TM_INPROJ, GATE_WIDTH), lambda i: (i, 0)),
        ],
        out_shape=[
            jax.ShapeDtypeStruct((t, QKVZ_WIDTH), BF16),
            jax.ShapeDtypeStruct((t, GATE_WIDTH), BF16),
        ],
        compiler_params=_params(48, ("arbitrary",)),
        name="inproj",
    )(x2, g, w)


def _attn_prep(src_ref, dst_ref, gain_ref, cos_s, sin_s, nat_s, seq, d, mode):
    lane = lax.broadcasted_iota(I32, (ROW_CHUNK, LANES), 1)
    first_half = (lane % HEAD_DIM) < (HEAD_DIM // 2)
    bi = lax.broadcasted_iota(I32, (LANES, LANES), 0) // HEAD_DIM
    bj = lax.broadcasted_iota(I32, (LANES, LANES), 1) // HEAD_DIM
    head_sum = jnp.where(bi == bj, 1.0, 0.0).astype(BF16)

    def body(i, carry):
        r0 = pl.multiple_of(i * ROW_CHUNK, ROW_CHUNK)
        for c in range(GROUP_WIDTH // LANES):
            x = src_ref[0, pl.ds(r0, ROW_CHUNK), c * LANES:(c + 1) * LANES].astype(F32)
            if mode == "v":
                y = x
            else:
                ss = x * x
                hi = ss.astype(BF16)
                lo = (ss - hi.astype(F32)).astype(BF16)
                ms = (_dot(hi, head_sum) + _dot(lo, head_sum)) * (1.0 / HEAD_DIM)
                y = x * lax.rsqrt(ms + EPS) * gain_ref[...]
                partner = jnp.where(first_half, pltpu.roll(y, LANES - HEAD_DIM // 2, 1), pltpu.roll(y, HEAD_DIM // 2, 1))
                y = y * cos_s[pl.ds(r0, ROW_CHUNK), :] + partner * sin_s[pl.ds(r0, ROW_CHUNK), :]
                if mode == "q":
                    y = y * (HEAD_DIM ** -0.5)
            if d == 1:
                dst_ref[pl.ds(r0, ROW_CHUNK), c * LANES:(c + 1) * LANES] = y.astype(BF16)
            else:
                nat_s[c, pl.ds(r0, ROW_CHUNK), :] = y
        return carry

    lax.fori_loop(0, seq // ROW_CHUNK, body, 0, unroll=2)
    if d > 1:
        sub = seq // d
        for r in range(d):
            for c in range(GROUP_WIDTH // LANES):
                dst_ref[r * sub:(r + 1) * sub, c * LANES:(c + 1) * LANES] = nat_s[c, pl.ds(r, sub, stride=d), :].astype(BF16)


def _attn_blocks(qr_s, kr_s, vr_s, ores_s, lres_s, seq, d):
    sub = seq // d
    nb = sub // BLK
    win = 2 * BLK if nb > 1 else BLK
    log2d = d.bit_length() - 1
    lane_head = lax.broadcasted_iota(I32, (BLK, GROUP_WIDTH), 1) // HEAD_DIM
    qi = lax.broadcasted_iota(I32, (BLK, win), 0)
    kj = lax.broadcasted_iota(I32, (BLK, win), 1)
    rel = qi - kj

    def body(idx, carry):
        r = idx & (d - 1)
        n = idx >> log2d
        row0 = pl.multiple_of(r * sub + n * BLK, BLK)
        if nb > 1:
            kn = jnp.maximum(n - 1, 0)
            k0 = pl.multiple_of(r * sub + kn * BLK, BLK)
            dist = rel + (n - kn) * BLK
        else:
            k0 = row0
            dist = rel
        valid = lax.bitcast_convert_type(dist, jnp.uint32) <= jnp.uint32(BLK)
        q = qr_s[pl.ds(row0, BLK), :]
        kw = kr_s[pl.ds(k0, win), :]
        vw = vr_s[pl.ds(k0, win), :]
        zero = jnp.zeros_like(q)
        qm = jnp.concatenate([jnp.where(lane_head == h, q, zero) for h in range(4)], axis=0)
        s = _dot_nt(qm, kw)
        ps, ms, ls = [], [], []
        for h in range(4):
            sh = jnp.where(valid, s[h * BLK:(h + 1) * BLK], NEG_INF)
            m = jnp.max(sh, axis=-1, keepdims=True)
            p = jnp.exp(sh - m)
            ls.append(jnp.sum(p, axis=-1, keepdims=True))
            ms.append(m)
            ps.append(p.astype(BF16))
        pv = _dot(jnp.concatenate(ps, axis=0), vw)
        o = jnp.zeros((BLK, GROUP_WIDTH), F32)
        lse = jnp.zeros((BLK, GROUP_WIDTH), F32)
        for h in range(4):
            sel = lane_head == h
            o = jnp.where(sel, pv[h * BLK:(h + 1) * BLK] * (1.0 / ls[h]), o)
            lse = jnp.where(sel, ms[h] + jnp.log(ls[h]), lse)
        ores_s[pl.ds(row0, BLK), :] = o
        lres_s[pl.ds(row0, BLK), :] = lse
        return carry

    lax.fori_loop(0, seq // BLK, body, 0, unroll=2)


def _attn_kernel(q_ref, k_ref, v_ref, pos_ref, invf_ref, qg_ref, kg_ref, o_ref,
                 cos_s, sin_s, nat_s, qr_s, kr_s, vr_s, ores_s, lres_s, onat_s, lnat_s, *, seq):
    g = pl.program_id(1)

    @pl.when(g == 0)
    def _():
        lane = lax.broadcasted_iota(I32, (1, LANES), 1)
        sign = jnp.where((lane % HEAD_DIM) < (HEAD_DIM // 2), -1.0, 1.0)

        def body(i, carry):
            r0 = pl.multiple_of(i * ROW_CHUNK, ROW_CHUNK)
            ang = pos_ref[0, pl.ds(r0, ROW_CHUNK), :].astype(F32) * invf_ref[...]
            cos_s[pl.ds(r0, ROW_CHUNK), :] = jnp.cos(ang)
            sin_s[pl.ds(r0, ROW_CHUNK), :] = jnp.sin(ang) * sign
            return carry

        lax.fori_loop(0, seq // ROW_CHUNK, body, 0)

    for gi, (_, d) in enumerate(ATTN_GROUPS):

        @pl.when(g == gi)
        def _(gi=gi, d=d):
            _attn_prep(q_ref, qr_s, qg_ref, cos_s, sin_s, nat_s, seq, d, "q")
            _attn_prep(k_ref, kr_s, kg_ref, cos_s, sin_s, nat_s, seq, d, "k")
            _attn_prep(v_ref, vr_s, None, cos_s, sin_s, nat_s, seq, d, "v")
            _attn_blocks(qr_s, kr_s, vr_s, ores_s, lres_s, seq, d)
            sub = seq // d
            for r in range(d):
                for c in range(GROUP_WIDTH // LANES):
                    t = gi * (GROUP_WIDTH // LANES) + c
                    onat_s[t, pl.ds(r, sub, stride=d), :] = ores_s[r * sub:(r + 1) * sub, c * LANES:(c + 1) * LANES]
                    lnat_s[t, pl.ds(r, sub, stride=d), :] = lres_s[r * sub:(r + 1) * sub, c * LANES:(c + 1) * LANES]

    @pl.when(g == N_GROUPS - 1)
    def _():
        tiles = GROUP_WIDTH // LANES

        def body(i, carry):
            r0 = pl.multiple_of(i * ROW_CHUNK, ROW_CHUNK)
            for c in range(tiles):
                ls = [lnat_s[gi * tiles + c, pl.ds(r0, ROW_CHUNK), :] for gi in range(N_GROUPS)]
                m = jnp.maximum(jnp.maximum(ls[0], ls[1]), ls[2])
                es = [jnp.exp(l - m) for l in ls]
                inv = 1.0 / (es[0] + es[1] + es[2])
                for gi in range(N_GROUPS):
                    t = gi * tiles + c
                    o_ref[0, pl.ds(r0, ROW_CHUNK), t * LANES:(t + 1) * LANES] = (
                        onat_s[t, pl.ds(r0, ROW_CHUNK), :] * (es[gi] * inv)).astype(BF16)
            return carry

        lax.fori_loop(0, seq // ROW_CHUNK, body, 0)


def _attention(qkvz3, pos3, invf, qg, kg):
    b, seq, _ = qkvz3.shape
    tiles = GROUP_WIDTH // LANES
    nq = ATTN_WIDTH // GROUP_WIDTH
    return pl.pallas_call(
        functools.partial(_attn_kernel, seq=seq),
        grid=(b, N_GROUPS),
        in_specs=[
            pl.BlockSpec((1, seq, GROUP_WIDTH), lambda i, g: (i, 0, g)),
            pl.BlockSpec((1, seq, GROUP_WIDTH), lambda i, g: (i, 0, nq + g)),
            pl.BlockSpec((1, seq, GROUP_WIDTH), lambda i, g: (i, 0, 2 * nq + g)),
            pl.BlockSpec((1, seq, 1), lambda i, g: (i, 0, 0)),
            pl.BlockSpec((1, LANES), lambda i, g: (0, 0)),
            pl.BlockSpec((1, LANES), lambda i, g: (0, 0)),
            pl.BlockSpec((1, LANES), lambda i, g: (0, 0)),
        ],
        out_specs=pl.BlockSpec((1, seq, ATTN_WIDTH), lambda i, g: (i, 0, 0)),
        out_shape=jax.ShapeDtypeStruct((b, seq, ATTN_WIDTH), BF16),
        scratch_shapes=[
            pltpu.VMEM((seq, LANES), F32),
            pltpu.VMEM((seq, LANES), F32),
            pltpu.VMEM((tiles, seq, LANES), F32),
            pltpu.VMEM((seq, GROUP_WIDTH), BF16),
            pltpu.VMEM((seq, GROUP_WIDTH), BF16),
            pltpu.VMEM((seq, GROUP_WIDTH), BF16),
            pltpu.VMEM((seq, GROUP_WIDTH), F32),
            pltpu.VMEM((seq, GROUP_WIDTH), F32),
            pltpu.VMEM((N_GROUPS * tiles, seq, LANES), F32),
            pltpu.VMEM((N_GROUPS * tiles, seq, LANES), F32),
        ],
        compiler_params=_params(48, ("arbitrary", "arbitrary")),
        name="attention",
    )(qkvz3, qkvz3, qkvz3, pos3, invf, qg, kg)


def _gelu(x):
    return 0.5 * x * (1.0 + lax.erf(x * 0.7071067811865476))


def _post_kernel(attn_ref, u_ref, vz_ref, ga_ref, gb_ref, x_ref, wa_ref, wb_ref, wo_ref,
                 lng_ref, lnb_ref, wcat_ref, bcat_ref, n2g_ref, wrh_ref, wrl_ref, br_ref,
                 x1_ref, hf_ref, e_ref, g_ref, cst_ref, cnt_ref, sgu_s, carry_s):
    tm = x_ref.shape[0]
    step = pl.program_id(0)

    @pl.when(step == 0)
    def _():
        carry_s[...] = jnp.zeros_like(carry_s)

    lane = lax.broadcasted_iota(I32, (SGU_CHUNK, LANES), 1)
    low = lane < SGU_GROUP_DIM
    trow = lax.broadcasted_iota(I32, (SGU_CHUNK, 2 * SGU_CHUNK), 0)
    tcol = lax.broadcasted_iota(I32, (SGU_CHUNK, 2 * SGU_CHUNK), 1) % SGU_CHUNK
    causal = tcol <= trow

    def chunk(c, carry):
        r0 = pl.multiple_of(c * SGU_CHUNK, SGU_CHUNK)
        u = _gelu(u_ref[pl.ds(r0, SGU_CHUNK), :].astype(F32))
        v = _gelu(vz_ref[pl.ds(r0, SGU_CHUNK), :].astype(F32))
        mu = jnp.mean(v, axis=-1, keepdims=True)
        vc = v - mu
        vn = vc * lax.rsqrt(jnp.mean(vc * vc, axis=-1, keepdims=True) + EPS)
        vn = (vn * lng_ref[...] + lnb_ref[...]).astype(BF16)
        zero = jnp.zeros((SGU_CHUNK, LANES), BF16)
        for j in range(SGU_WIDTH // LANES):
            vt = vn[:, j * LANES:(j + 1) * LANES]
            rhs = jnp.concatenate([jnp.where(low, vt, zero), jnp.where(low, zero, vt)], axis=0)
            wj = jnp.where(causal, wcat_ref[j], jnp.zeros((), BF16))
            mixed = _dot(wj, rhs) + bcat_ref[j]
            sgu_s[pl.ds(r0, SGU_CHUNK), j * LANES:(j + 1) * LANES] = (u[:, j * LANES:(j + 1) * LANES] * mixed).astype(BF16)
        return carry

    lax.fori_loop(0, tm // SGU_CHUNK, chunk, 0)

    y_a = _dot(attn_ref[...], wa_ref[...])
    y_b = _dot(sgu_s[...], wb_ref[...])
    merged = jax.nn.sigmoid(ga_ref[...].astype(F32)) * y_a + jax.nn.sigmoid(gb_ref[...].astype(F32)) * y_b
    x1 = x_ref[...] + _dot(merged.astype(BF16), wo_ref[...])
    x1_ref[...] = x1

    hf = x1 * lax.rsqrt(jnp.mean(x1 * x1, axis=-1, keepdims=True) + EPS) * n2g_ref[...]
    hi = hf.astype(BF16)
    hf_ref[...] = hi
    lo = (hf - hi.astype(F32)).astype(BF16)
    logits = _dot_nt(wrh_ref[...], hi) + _dot_nt(wrl_ref[...], hi) + _dot_nt(wrh_ref[...], lo) + br_ref[...]

    eio = lax.broadcasted_iota(I32, (N_EXPERTS, tm), 0)
    work = logits
    top_v, onehots = [], []
    for k in range(TOP_K):
        m = jnp.max(work, axis=0, keepdims=True)
        idx = jnp.min(jnp.where(work == m, eio, N_EXPERTS), axis=0, keepdims=True)
        oh = eio == idx
        work = jnp.where(oh, -jnp.inf, work)
        top_v.append(m)
        onehots.append(oh)
        e_ref[k:k + 1, :] = idx
    ex = [jnp.exp(v - top_v[0]) for v in top_v]
    inv = 1.0 / (ex[0] + ex[1] + ex[2] + ex[3])
    for k in range(TOP_K):
        g_ref[k:k + 1, :] = ex[k] * inv

    sel = jnp.zeros((N_EXPERTS, tm), F32)
    for oh in onehots:
        sel = jnp.where(oh, 1.0, sel)
    lane = lax.broadcasted_iota(I32, (N_EXPERTS, LANES), 1)
    run = carry_s[...]
    cst = jnp.zeros((N_EXPERTS, LANES), F32)
    for s in range(tm // TM_ROUTE):
        cst = jnp.where(lane == s, run, cst)
        run = run + jnp.sum(sel[:, s * TM_ROUTE:(s + 1) * TM_ROUTE], axis=1, keepdims=True)
    cst_ref[0] = cst.astype(I32)
    carry_s[...] = run
    cnt_ref[...] = run.astype(I32)


def _post(attn2, qkvz2, gates2, x2, wa, wb, wo, lng, lnb, wcat, bcat, n2g, wrh, wrl, br):
    t = x2.shape[0]
    tm = TM_POST
    nu = 3 * ATTN_WIDTH // SGU_WIDTH
    const = lambda *shape: pl.BlockSpec(shape, lambda i: (0,) * len(shape), pipeline_mode=pl.Buffered(1))
    return pl.pallas_call(
        _post_kernel,
        grid=(t // tm,),
        in_specs=[
            pl.BlockSpec((tm, ATTN_WIDTH), lambda i: (i, 0)),
            pl.BlockSpec((tm, SGU_WIDTH), lambda i: (i, nu)),
            pl.BlockSpec((tm, SGU_WIDTH), lambda i: (i, nu + 1)),
            pl.BlockSpec((tm, D_MODEL), lambda i: (i, 0)),
            pl.BlockSpec((tm, D_MODEL), lambda i: (i, 1)),
            pl.BlockSpec((tm, D_MODEL), lambda i: (i, 0)),
            const(ATTN_WIDTH, D_MODEL),
            const(SGU_WIDTH, D_MODEL),
            const(D_MODEL, D_MODEL),
            const(1, SGU_WIDTH),
            const(1, SGU_WIDTH),
            const(SGU_WIDTH // LANES, SGU_CHUNK, 2 * SGU_CHUNK),
            const(SGU_WIDTH // LANES, SGU_CHUNK, LANES),
            const(1, D_MODEL),
            const(N_EXPERTS, D_MODEL),
            const(N_EXPERTS, D_MODEL),
            const(N_EXPERTS, 1),
        ],
        out_specs=[
            pl.BlockSpec((tm, D_MODEL), lambda i: (i, 0)),
            pl.BlockSpec((tm, D_MODEL), lambda i: (i, 0)),
            pl.BlockSpec((TOP_K, tm), lambda i: (0, i)),
            pl.BlockSpec((TOP_K, tm), lambda i: (0, i)),
            pl.BlockSpec((1, N_EXPERTS, LANES), lambda i: (i, 0, 0)),
            pl.BlockSpec((N_EXPERTS, LANES), lambda i: (0, 0)),
        ],
        out_shape=[
            jax.ShapeDtypeStruct((t, D_MODEL), F32),
            jax.ShapeDtypeStruct((t, D_MODEL), BF16),
            jax.ShapeDtypeStruct((TOP_K, t), I32),
            jax.ShapeDtypeStruct((TOP_K, t), F32),
            jax.ShapeDtypeStruct((t // tm, N_EXPERTS, LANES), I32),
            jax.ShapeDtypeStruct((N_EXPERTS, LANES), I32),
        ],
        scratch_shapes=[
            pltpu.VMEM((tm, SGU_WIDTH), BF16),
            pltpu.VMEM((N_EXPERTS, LANES), F32),
        ],
        compiler_params=_params(48, ("arbitrary",)),
        name="post",
    )(attn2, qkvz2, qkvz2, gates2, gates2, x2, wa, wb, wo, lng, lnb, wcat, bcat, n2g, wrh, wrl, br)


def _route_rows(e_ref, tm):
    e = e_ref[...]
    eio = lax.broadcasted_iota(I32, (N_EXPERTS, tm), 0)
    onehots = [eio == e[k:k + 1, :] for k in range(TOP_K)]
    sel = jnp.zeros((N_EXPERTS, tm), F32)
    for oh in onehots:
        sel = jnp.where(oh, 1.0, sel)
    ti = lax.broadcasted_iota(I32, (tm, tm), 0)
    tj = lax.broadcasted_iota(I32, (tm, tm), 1)
    before = jnp.where(ti < tj, 1.0, 0.0).astype(BF16)
    slot = _dot(sel.astype(BF16), before)
    tiles = jnp.floor((jnp.sum(sel, axis=1, keepdims=True) + (SUBLANES - 1)) * (1.0 / SUBLANES))
    xi = lax.broadcasted_iota(I32, (N_EXPERTS, N_EXPERTS), 0)
    xj = lax.broadcasted_iota(I32, (N_EXPERTS, N_EXPERTS), 1)
    lower = jnp.where(xj < xi, 1.0, 0.0).astype(BF16)
    off = _dot(lower, jnp.broadcast_to(tiles, (N_EXPERTS, LANES)).astype(BF16))[:, 0:1] * float(SUBLANES)
    row = off + slot
    return [jnp.sum(jnp.where(oh, row, 0.0), axis=0, keepdims=True) for oh in onehots]


def _pick(rows, values, default):
    tm = rows[0].shape[1]
    rio = lax.broadcasted_iota(I32, (ROUTE_ROWS, tm), 0).astype(F32)
    out = default
    for r, v in zip(rows, values):
        out = jnp.where(rio == r, v, out)
    return out


def _wait_rows(buf, hbm_ref, sem, n):
    n = pl.multiple_of(n, SUBLANES)

    @pl.when(n > 0)
    def _():
        pltpu.make_async_copy(buf.at[pl.ds(0, n), :], hbm_ref.at[pl.ds(0, n), :], sem).wait()


def _dispatch_kernel(start_ref, off_ref, cnt_ref, tot_ref, za_ref, zb_ref, hf_ref, e_ref, g_ref, xs_ref,
                     obuf, zbuf, sem, semz):
    tm = hf_ref.shape[0]
    i = pl.program_id(0)
    last = pl.num_programs(0) - 1
    cur = lax.rem(i, 2)
    rows = _route_rows(e_ref, tm)
    g = g_ref[...]
    pt = _pick(rows, [1.0] * TOP_K, 0.0).astype(BF16)
    gate = jnp.sum(_pick(rows, [g[k:k + 1, :] for k in range(TOP_K)], 0.0), axis=1, keepdims=True)

    @pl.when(i >= 2)
    def _():
        _wait_rows(obuf.at[cur], xs_ref, sem.at[cur], tot_ref[jnp.maximum(i - 2, 0)])

    obuf[cur, :, 0:D_MODEL] = _dot(pt, hf_ref[...])
    obuf[cur, :, D_MODEL:XS_WIDTH] = jnp.broadcast_to(gate, (ROUTE_ROWS, LANES))

    for x in range(N_EXPERTS):
        n = pl.multiple_of(cnt_ref[i * N_EXPERTS + x], SUBLANES)

        @pl.when(n > 0)
        def _(x=x, n=n):
            src = pl.multiple_of(off_ref[i * N_EXPERTS + x], SUBLANES)
            dst = pl.multiple_of(start_ref[i * N_EXPERTS + x], SUBLANES)
            pltpu.make_async_copy(obuf.at[cur, pl.ds(src, n), :], xs_ref.at[pl.ds(dst, n), :], sem.at[cur]).start()

    @pl.when(i == last)
    def _():
        @pl.when(i >= 1)
        def _():
            _wait_rows(obuf.at[1 - cur], xs_ref, sem.at[1 - cur], tot_ref[jnp.maximum(i - 1, 0)])

        _wait_rows(obuf.at[cur], xs_ref, sem.at[cur], tot_ref[i])
        zbuf[...] = jnp.zeros_like(zbuf)

        def zero_region(z, act):
            n = zb_ref[z] - za_ref[z]
            big = n // EXPERT_TILE
            rest = pl.multiple_of(n - big * EXPERT_TILE, SUBLANES)

            def piece(row, nrows):
                return pltpu.make_async_copy(zbuf.at[pl.ds(0, nrows), :],
                                             xs_ref.at[pl.ds(pl.multiple_of(row, SUBLANES), nrows), :], semz)

            lax.fori_loop(0, big, lambda m, c: (act(piece(za_ref[z] + m * EXPERT_TILE, EXPERT_TILE)), c)[1], 0)

            @pl.when(rest > 0)
            def _():
                act(piece(za_ref[z] + big * EXPERT_TILE, rest))

        for z in range(N_EXPERTS + 1):
            zero_region(z, lambda cp: cp.start())
        for z in range(N_EXPERTS + 1):
            zero_region(z, lambda cp: cp.wait())


def _dispatch(start, off, cnt, tot, za, zb, hf, e4, g4, cap):
    t = hf.shape[0]
    tm = TM_ROUTE
    tok = lambda i, *_: (0, i)
    return pl.pallas_call(
        _dispatch_kernel,
        grid_spec=pltpu.PrefetchScalarGridSpec(
            num_scalar_prefetch=6,
            grid=(t // tm,),
            in_specs=[
                pl.BlockSpec((tm, D_MODEL), lambda i, *_: (i, 0)),
                pl.BlockSpec((TOP_K, tm), tok),
                pl.BlockSpec((TOP_K, tm), tok),
            ],
            out_specs=pl.BlockSpec(memory_space=pl.ANY),
            scratch_shapes=[
                pltpu.VMEM((2, ROUTE_ROWS, XS_WIDTH), F32),
                pltpu.VMEM((EXPERT_TILE, XS_WIDTH), F32),
                pltpu.SemaphoreType.DMA((2,)),
                pltpu.SemaphoreType.DMA(()),
            ],
        ),
        out_shape=jax.ShapeDtypeStruct((cap, XS_WIDTH), F32),
        compiler_params=_params(48, ("arbitrary",)),
        name="dispatch",
    )(start, off, cnt, tot, za, zb, hf, e4, g4)


def _expert_kernel(te_ref, nu_ref, xs_ref, wgu_ref, bgu_ref, wdn_ref, bdn_ref, ys_ref, wgu_s, wdn_s):
    i = pl.program_id(0)

    @pl.when(i < nu_ref[0])
    def _():
        @pl.when((i == 0) | (te_ref[i] != te_ref[jnp.maximum(i - 1, 0)]))
        def _():
            wgu_s[...] = wgu_ref[0].astype(BF16)
            wdn_s[...] = wdn_ref[0].astype(BF16)

        x = xs_ref[:, 0:D_MODEL].astype(BF16)
        route_gate = xs_ref[:, D_MODEL:D_MODEL + 1]
        gu = _dot(x, wgu_s[...]) + bgu_ref[0]
        gate = jnp.minimum(gu[:, :D_EXPERT], SWIGLU_LIMIT)
        up = jnp.clip(gu[:, D_EXPERT:], -SWIGLU_LIMIT, SWIGLU_LIMIT)
        act = (up + 1.0) * (gate * jax.nn.sigmoid(SWIGLU_ALPHA * gate))
        ys_ref[...] = (_dot(act.astype(BF16), wdn_s[...]) + bdn_ref[0]) * route_gate

    @pl.when(i >= nu_ref[0])
    def _():
        ys_ref[...] = jnp.zeros_like(ys_ref)


def _experts(tile_expert, n_used, xs, wgu, bgu, wdn, bdn):
    cap = xs.shape[0]
    n_tiles = cap // EXPERT_TILE
    row = lambda i, te, nu: (jnp.minimum(i, nu[0] - 1), 0)
    by_expert = lambda i, te, nu: (te[i], 0, 0)
    return pl.pallas_call(
        _expert_kernel,
        grid_spec=pltpu.PrefetchScalarGridSpec(
            num_scalar_prefetch=2,
            grid=(n_tiles,),
            in_specs=[
                pl.BlockSpec((EXPERT_TILE, XS_WIDTH), row),
                pl.BlockSpec((1, D_MODEL, 2 * D_EXPERT), by_expert),
                pl.BlockSpec((1, 1, 2 * D_EXPERT), by_expert),
                pl.BlockSpec((1, D_EXPERT, D_MODEL), by_expert),
                pl.BlockSpec((1, 1, D_MODEL), by_expert),
            ],
            out_specs=pl.BlockSpec((EXPERT_TILE, D_MODEL), lambda i, te, nu: (i, 0)),
            scratch_shapes=[
                pltpu.VMEM((D_MODEL, 2 * D_EXPERT), BF16),
                pltpu.VMEM((D_EXPERT, D_MODEL), BF16),
            ],
        ),
        out_shape=jax.ShapeDtypeStruct((cap, D_MODEL), F32),
        compiler_params=_params(56, ("arbitrary",)),
        name="experts",
    )(tile_expert, n_used, xs, wgu, bgu, wdn, bdn)


def _combine_kernel(start_ref, off_ref, cnt_ref, tot_ref, x1_ref, e_ref, ys_ref, o_ref, sbuf, sem):
    tm = x1_ref.shape[0]
    i = pl.program_id(0)
    last = pl.num_programs(0) - 1
    cur = lax.rem(i, 2)

    def fetch(step, b):
        for x in range(N_EXPERTS):
            n = pl.multiple_of(cnt_ref[step * N_EXPERTS + x], SUBLANES)

            @pl.when(n > 0)
            def _(x=x, n=n):
                src = pl.multiple_of(start_ref[step * N_EXPERTS + x], SUBLANES)
                dst = pl.multiple_of(off_ref[step * N_EXPERTS + x], SUBLANES)
                pltpu.make_async_copy(ys_ref.at[pl.ds(src, n), :], sbuf.at[b, pl.ds(dst, n), :], sem.at[b]).start()

    @pl.when(i == 0)
    def _():
        sbuf[...] = jnp.zeros_like(sbuf)
        fetch(i, cur)

    @pl.when(i < last)
    def _():
        fetch(i + 1, 1 - cur)

    rows = _route_rows(e_ref, tm)
    pt = _pick(rows, [1.0] * TOP_K, 0.0).astype(BF16)
    _wait_rows(sbuf.at[cur], ys_ref, sem.at[cur], tot_ref[i])
    picked = lax.dot_general(pt, sbuf[cur].astype(BF16), _TN, preferred_element_type=F32)
    o_ref[...] = x1_ref[...] + picked


def _combine(start, off, cnt, tot, x1, e4, ys):
    t = x1.shape[0]
    tm = TM_ROUTE
    return pl.pallas_call(
        _combine_kernel,
        grid_spec=pltpu.PrefetchScalarGridSpec(
            num_scalar_prefetch=4,
            grid=(t // tm,),
            in_specs=[
                pl.BlockSpec((tm, D_MODEL), lambda i, *_: (i, 0)),
                pl.BlockSpec((TOP_K, tm), lambda i, *_: (0, i)),
                pl.BlockSpec(memory_space=pl.ANY),
            ],
            out_specs=pl.BlockSpec((tm, D_MODEL), lambda i, *_: (i, 0)),
            scratch_shapes=[
                pltpu.VMEM((2, ROUTE_ROWS, D_MODEL), F32),
                pltpu.SemaphoreType.DMA((2,)),
            ],
        ),
        out_shape=jax.ShapeDtypeStruct((t, D_MODEL), F32),
        compiler_params=_params(48, ("arbitrary",)),
        name="combine",
    )(start, off, cnt, tot, x1, e4, ys)


def _layer(x, positions, norm1_g, w_in, q_norm_g, k_norm_g, sgu_ln_g, sgu_ln_b, w_spatial, b_spatial,
           w_branch_a, w_branch_b, w_out, norm2_g, w_router, b_router, w_gate_up, b_gate_up, w_down, b_down):
    b, seq, _ = x.shape
    t = b * seq
    x2 = x.reshape(t, D_MODEL)

    half = HEAD_DIM // 2
    inv_freq = ROPE_THETA ** (-jnp.arange(half, dtype=F32) / half)
    invf = jnp.tile(inv_freq, LANES // half).reshape(1, LANES)
    qg = jnp.tile(q_norm_g, LANES // HEAD_DIM).reshape(1, LANES)
    kg = jnp.tile(k_norm_g, LANES // HEAD_DIM).reshape(1, LANES)
    n_pairs = SGU_WIDTH // LANES
    wcat = w_spatial.reshape(n_pairs, 2, SGU_CHUNK, SGU_CHUNK).transpose(0, 2, 1, 3).reshape(n_pairs, SGU_CHUNK, 2 * SGU_CHUNK)
    bcat = jnp.repeat(b_spatial.reshape(n_pairs, 2, SGU_CHUNK).transpose(0, 2, 1), SGU_GROUP_DIM, axis=2)
    wr_t = w_router.T
    wr_hi = wr_t.astype(BF16)
    wr_lo = (wr_t - wr_hi.astype(F32)).astype(BF16)

    qkvz, gates = _inproj(x2, norm1_g.reshape(1, D_MODEL), w_in.astype(BF16))
    attn = _attention(qkvz.reshape(b, seq, QKVZ_WIDTH), positions.reshape(b, seq, 1), invf, qg, kg)
    x1, hf, e4, g4, cst, counts = _post(
        attn.reshape(t, ATTN_WIDTH), qkvz, gates, x2,
        w_branch_a.astype(BF16), w_branch_b.astype(BF16), w_out.astype(BF16),
        sgu_ln_g.reshape(1, SGU_WIDTH), sgu_ln_b.reshape(1, SGU_WIDTH), wcat.astype(BF16), bcat,
        norm2_g.reshape(1, D_MODEL), wr_hi, wr_lo, b_router.reshape(N_EXPERTS, 1))

    steps_per_tile = TM_POST // TM_ROUTE
    n_steps = t // TM_ROUTE
    before = cst[:, :, :steps_per_tile].transpose(0, 2, 1).reshape(n_steps, N_EXPERTS)
    total = counts[:, 0]
    cnt = jnp.concatenate([before[1:], total[None, :]], axis=0) - before
    cnt_al = (cnt + SUBLANES - 1) // SUBLANES * SUBLANES
    off = (jnp.cumsum(cnt_al, axis=1) - cnt_al).astype(I32).reshape(-1)
    tot = jnp.sum(cnt_al, axis=1).astype(I32)
    used = jnp.sum(cnt_al, axis=0)
    region = (used + EXPERT_TILE - 1) // EXPERT_TILE * EXPERT_TILE
    rend = jnp.cumsum(region)
    rstart = rend - region
    start = (rstart[None, :] + jnp.cumsum(cnt_al, axis=0) - cnt_al).astype(I32).reshape(-1)
    bound = t * TOP_K + n_steps * N_EXPERTS * (SUBLANES - 1) + N_EXPERTS * (EXPERT_TILE - 1)
    cap = (bound + EXPERT_TILE - 1) // EXPERT_TILE * EXPERT_TILE
    n_tiles = cap // EXPERT_TILE
    tile_expert = jnp.sum(jnp.arange(n_tiles, dtype=I32)[:, None] * EXPERT_TILE >= rend[None, :], axis=1)
    tile_expert = jnp.minimum(tile_expert, N_EXPERTS - 1).astype(I32)
    n_used = (rend[-1:] // EXPERT_TILE).astype(I32)
    za = jnp.concatenate([rstart + used, rend[-1:]]).astype(I32)
    zb = jnp.concatenate([rend, jnp.full((1,), cap, rend.dtype)]).astype(I32)
    cnt_al = cnt_al.astype(I32).reshape(-1)

    xs = _dispatch(start, off, cnt_al, tot, za, zb, hf, e4, g4, cap)
    ys = _experts(tile_expert, n_used, xs,
                  w_gate_up, b_gate_up.reshape(N_EXPERTS, 1, 2 * D_EXPERT),
                  w_down, b_down.reshape(N_EXPERTS, 1, D_MODEL))
    out = _combine(start, off, cnt_al, tot, x1, e4, ys)
    return out.reshape(b, seq, D_MODEL)


def kernel(x, positions, norm1_g, w_in, q_norm_g, k_norm_g, sgu_ln_g, sgu_ln_b, w_spatial, b_spatial, w_branch_a, w_branch_b, w_out, norm2_g, w_router, b_router, w_gate_up, b_gate_up, w_down, b_down):
    for layer in range(norm1_g.shape[0]):
        x = _layer(x, positions, norm1_g[layer], w_in[layer], q_norm_g[layer], k_norm_g[layer],
                   sgu_ln_g[layer], sgu_ln_b[layer], w_spatial[layer], b_spatial[layer],
                   w_branch_a[layer], w_branch_b[layer], w_out[layer], norm2_g[layer],
                   w_router[layer], b_router[layer], w_gate_up[layer], b_gate_up[layer],
                   w_down[layer], b_down[layer])
    return x
```

```python
import functools

import jax
import jax.numpy as jnp
from jax import lax
from jax.experimental import pallas as pl
from jax.experimental.pallas import tpu as pltpu

F32 = jnp.float32
BF16 = jnp.bfloat16
I32 = jnp.int32

D_MODEL = 1024
HEAD_DIM = 64
ROT_HALF = HEAD_DIM // 2
LOG2E = 1.4426950408889634
LN2 = 0.6931471805599453
Q_SCALE = HEAD_DIM ** -0.5 * LOG2E
ATTN_GROUPS = ((128, 1), (512, 4), (2048, 16))
N_GROUPS = len(ATTN_GROUPS)
GROUP_WIDTH = 256
ATTN_WIDTH = N_GROUPS * GROUP_WIDTH
BLK = 128
ROPE_THETA = 10000.0
SGU_CHUNK = 128
SGU_GROUP_DIM = 64
SGU_WIDTH = 768
GATE_WIDTH = 2 * D_MODEL
N_EXPERTS = 32
TOP_K = 4
D_EXPERT = D_MODEL
SWIGLU_LIMIT = 7.0
SWIGLU_ALPHA = 1.702
EPS = 1e-6
NEG_INF = -1e30
LANES = 128

TM_INPROJ = 512
TM_POST = 512
TM_ROUTE = 256
SUBLANES = 8
ROUTE_ROWS = TM_ROUTE * TOP_K + N_EXPERTS * SUBLANES
EXPERT_TILE = 512
XS_WIDTH = D_MODEL + LANES
ROW_CHUNK = 256

_NT = (((1,), (1,)), ((), ()))
_TN = (((0,), (0,)), ((), ()))


def _dot(a, b):
    return jnp.dot(a, b, preferred_element_type=F32)


def _dot_nt(a, b):
    return lax.dot_general(a, b, _NT, preferred_element_type=F32)


def _params(vmem_mb, sem):
    return pltpu.CompilerParams(dimension_semantics=sem, vmem_limit_bytes=vmem_mb * 1024 * 1024)


def _qk_column_order():
    cols = []
    for c in range(ATTN_WIDTH):
        g, w = divmod(c, GROUP_WIDTH)
        p, l = divmod(w, LANES)
        half, rem = divmod(l, 2 * ROT_HALF)
        hh, i = divmod(rem, ROT_HALF)
        cols.append(g * GROUP_WIDTH + (2 * p + hh) * HEAD_DIM + half * ROT_HALF + i)
    return cols


def _split3(x):
    a = x.astype(BF16)
    r = x - a.astype(F32)
    b = r.astype(BF16)
    return a, b, (r - b.astype(F32)).astype(BF16)


def _inproj_kernel(x_ref, g_ref, w_ref, pos_ref, invf_ref, qg_ref, kg_ref,
                   o0_ref, o1_ref, o2_ref, oz_ref, og_ref, stage_s, cos_s, sin_s):
    tm = x_ref.shape[0]
    x = x_ref[...]
    y = x * lax.rsqrt(jnp.mean(x * x, axis=-1, keepdims=True) + EPS)
    h = (y * g_ref[...]).astype(BF16)

    per_row = LANES // ROT_HALF
    lane = lax.broadcasted_iota(I32, (tm // per_row, LANES), 1)
    p4 = pos_ref[...].astype(F32)
    posd = p4[:, per_row - 1:per_row]
    for j in range(per_row - 2, -1, -1):
        posd = jnp.where(lane < (j + 1) * ROT_HALF, p4[:, j:j + 1], posd)
    ang = posd * invf_ref[...]
    cparts = _split3(jnp.cos(ang))
    sparts = _split3(jnp.sin(ang))
    src = lax.broadcasted_iota(I32, (LANES, LANES), 0)
    dst = lax.broadcasted_iota(I32, (LANES, LANES), 1)
    for j in range(per_row):
        pick = src == j * ROT_HALF + dst % ROT_HALF
        spread = jnp.where(pick, 1.0, 0.0).astype(BF16)
        spread_neg = jnp.where(pick, jnp.where(dst < 2 * ROT_HALF, -1.0, 1.0), 0.0).astype(BF16)
        cos_s[pl.ds(j, tm // per_row, stride=per_row), :] = sum(_dot(p, spread) for p in cparts)
        sin_s[pl.ds(j, tm // per_row, stride=per_row), :] = sum(_dot(p, spread_neg) for p in sparts)
    cosv = cos_s[...]
    sinv = sin_s[...]
    li = (lax.broadcasted_iota(I32, (LANES, LANES), 0) // ROT_HALF) % 2
    lj = (lax.broadcasted_iota(I32, (LANES, LANES), 1) // ROT_HALF) % 2
    head_sum = jnp.where(li == lj, 1.0, 0.0).astype(BF16)

    outs = (o0_ref, o1_ref, o2_ref)
    stage = 0
    for gi, (_, d) in enumerate(ATTN_GROUPS):
        for ti, (mode, gain_ref) in enumerate((("q", qg_ref), ("k", kg_ref), ("v", None))):
            col = ti * ATTN_WIDTH + gi * GROUP_WIDTH
            proj = _dot(h, w_ref[:, col:col + GROUP_WIDTH])
            for c in range(GROUP_WIDTH // LANES):
                t = proj[:, c * LANES:(c + 1) * LANES]
                if mode != "v":
                    ss = t * t
                    hi = ss.astype(BF16)
                    lo = (ss - hi.astype(F32)).astype(BF16)
                    ms = (_dot(hi, head_sum) + _dot(lo, head_sum)) * (1.0 / HEAD_DIM)
                    t = t * lax.rsqrt(ms + EPS) * gain_ref[...]
                    t = t * cosv + pltpu.roll(t, 2 * ROT_HALF, 1) * sinv
                    if mode == "q":
                        t = t * Q_SCALE
                lanes = slice(ti * GROUP_WIDTH + c * LANES, ti * GROUP_WIDTH + (c + 1) * LANES)
                if d == 1:
                    outs[gi][0, 0, :, lanes] = t.astype(BF16)
                else:
                    stage_s[stage] = t
                    for r in range(d):
                        outs[gi][0, r, :, lanes] = stage_s[stage, pl.ds(r, tm // d, stride=d), :].astype(BF16)
                    stage += 1
    zc = 3 * ATTN_WIDTH
    for lo in range(0, 2 * SGU_WIDTH, SGU_WIDTH):
        oz_ref[:, lo:lo + SGU_WIDTH] = _dot(h, w_ref[:, zc + lo:zc + lo + SGU_WIDTH]).astype(BF16)
    gc = zc + 2 * SGU_WIDTH
    for lo in range(0, GATE_WIDTH, D_MODEL):
        og_ref[:, lo:lo + D_MODEL] = _dot(h, w_ref[:, gc + lo:gc + lo + D_MODEL]).astype(BF16)


def _inproj(x2, g, w, pos2, invf, qg, kg, b, seq):
    t = x2.shape[0]
    n = w.shape[1]
    tm = TM_INPROJ
    nj = seq // tm
    n_stage = sum(1 for _, d in ATTN_GROUPS if d > 1) * 3 * (GROUP_WIDTH // LANES)
    row = lambda i, j: (i * nj + j, 0)
    const = lambda i, j: (0, 0)
    per_row = LANES // ROT_HALF
    return pl.pallas_call(
        _inproj_kernel,
        grid=(b, nj),
        in_specs=[
            pl.BlockSpec((tm, D_MODEL), row),
            pl.BlockSpec((1, D_MODEL), const),
            pl.BlockSpec((D_MODEL, n), const, pipeline_mode=pl.Buffered(1)),
            pl.BlockSpec((tm // per_row, per_row), row),
            pl.BlockSpec((1, LANES), const),
            pl.BlockSpec((1, LANES), const),
            pl.BlockSpec((1, LANES), const),
        ],
        out_specs=[pl.BlockSpec((1, d, tm // d, ATTN_WIDTH), lambda i, j: (i, 0, j, 0)) for _, d in ATTN_GROUPS] + [
            pl.BlockSpec((tm, 2 * SGU_WIDTH), row),
            pl.BlockSpec((tm, GATE_WIDTH), row),
        ],
        out_shape=[jax.ShapeDtypeStruct((b, d, seq // d, ATTN_WIDTH), BF16) for _, d in ATTN_GROUPS] + [
            jax.ShapeDtypeStruct((t, 2 * SGU_WIDTH), BF16),
            jax.ShapeDtypeStruct((t, GATE_WIDTH), BF16),
        ],
        scratch_shapes=[
            pltpu.VMEM((n_stage, tm, LANES), F32),
            pltpu.VMEM((tm, LANES), F32),
            pltpu.VMEM((tm, LANES), F32),
        ],
        compiler_params=_params(48, ("arbitrary", "arbitrary")),
        name="inproj",
    )(x2, g, w, pos2, invf, qg, kg)


def _attn_blocks(qkv_ref, ores_s, lres_s, seq, d):
    sub = seq // d
    nb = sub // BLK
    win = 2 * BLK if nb > 1 else BLK
    log2d = d.bit_length() - 1
    lane = lax.broadcasted_iota(I32, (BLK, GROUP_WIDTH), 1)
    q_head = 2 * (lane // LANES) + (lane // ROT_HALF) % 2
    lane_head = lane // HEAD_DIM
    qi = lax.broadcasted_iota(I32, (BLK, win), 0)
    kj = lax.broadcasted_iota(I32, (BLK, win), 1)
    rel = qi - kj

    def body(idx, carry):
        r = idx & (d - 1)
        n = idx >> log2d
        row0 = pl.multiple_of(r * sub + n * BLK, BLK)
        if nb > 1:
            kn = jnp.maximum(n - 1, 0)
            k0 = pl.multiple_of(r * sub + kn * BLK, BLK)
            dist = rel + (n - kn) * BLK
        else:
            k0 = row0
            dist = rel
        valid = lax.bitcast_convert_type(dist, jnp.uint32) <= jnp.uint32(BLK)
        q = qkv_ref[0, pl.ds(row0, BLK), 0:GROUP_WIDTH]
        kw = qkv_ref[0, pl.ds(k0, win), GROUP_WIDTH:2 * GROUP_WIDTH]
        vw = qkv_ref[0, pl.ds(k0, win), 2 * GROUP_WIDTH:3 * GROUP_WIDTH]
        zero = jnp.zeros_like(q)
        qm = jnp.concatenate([jnp.where(q_head == h, q, zero) for h in range(4)], axis=0)
        s = _dot_nt(qm, kw)
        ps, ms, ls = [], [], []
        for h in range(4):
            sh = jnp.where(valid, s[h * BLK:(h + 1) * BLK], NEG_INF)
            m = jnp.max(sh, axis=-1, keepdims=True)
            p = jnp.exp2(sh - m)
            ls.append(jnp.sum(p, axis=-1, keepdims=True))
            ms.append(m)
            ps.append(p.astype(BF16))
        pv = _dot(jnp.concatenate(ps, axis=0), vw)
        o = jnp.zeros((BLK, GROUP_WIDTH), F32)
        lse = jnp.zeros((BLK, GROUP_WIDTH), F32)
        for h in range(4):
            sel = lane_head == h
            o = jnp.where(sel, pv[h * BLK:(h + 1) * BLK] * (1.0 / ls[h]), o)
            lse = jnp.where(sel, (ms[h] + jnp.log2(ls[h])) * LN2, lse)
        ores_s[pl.ds(row0, BLK), :] = o
        lres_s[pl.ds(row0, BLK), :] = lse
        return carry

    lax.fori_loop(0, seq // BLK, body, 0, unroll=8)


def _attn_kernel(g0_ref, g1_ref, g2_ref, o_ref, ores_s, lres_s, onat_s, lnat_s, *, seq):
    for gi, ((_, d), qkv_ref) in enumerate(zip(ATTN_GROUPS, (g0_ref, g1_ref, g2_ref))):
        _attn_blocks(qkv_ref, ores_s, lres_s, seq, d)
        sub = seq // d
        for r in range(d):
            for c in range(GROUP_WIDTH // LANES):
                t = gi * (GROUP_WIDTH // LANES) + c
                onat_s[t, pl.ds(r, sub, stride=d), :] = ores_s[r * sub:(r + 1) * sub, c * LANES:(c + 1) * LANES]
                lnat_s[t, pl.ds(r, sub, stride=d), :] = lres_s[r * sub:(r + 1) * sub, c * LANES:(c + 1) * LANES]

    tiles = GROUP_WIDTH // LANES

    def body(i, carry):
        r0 = pl.multiple_of(i * ROW_CHUNK, ROW_CHUNK)
        for c in range(tiles):
            ls = [lnat_s[gi * tiles + c, pl.ds(r0, ROW_CHUNK), :] for gi in range(N_GROUPS)]
            m = jnp.maximum(jnp.maximum(ls[0], ls[1]), ls[2])
            es = [jnp.exp(l - m) for l in ls]
            inv = 1.0 / (es[0] + es[1] + es[2])
            for gi in range(N_GROUPS):
                t = gi * tiles + c
                o_ref[0, pl.ds(r0, ROW_CHUNK), t * LANES:(t + 1) * LANES] = (
                    onat_s[t, pl.ds(r0, ROW_CHUNK), :] * (es[gi] * inv)).astype(BF16)
        return carry

    lax.fori_loop(0, seq // ROW_CHUNK, body, 0, unroll=2)


def _attention(qkv_groups):
    b = qkv_groups[0].shape[0]
    seq = qkv_groups[0].shape[1] * qkv_groups[0].shape[2]
    tiles = GROUP_WIDTH // LANES
    blk = pl.BlockSpec((1, seq, ATTN_WIDTH), lambda i: (i, 0, 0))
    return pl.pallas_call(
        functools.partial(_attn_kernel, seq=seq),
        grid=(b,),
        in_specs=[blk] * N_GROUPS,
        out_specs=blk,
        out_shape=jax.ShapeDtypeStruct((b, seq, ATTN_WIDTH), BF16),
        scratch_shapes=[
            pltpu.VMEM((seq, GROUP_WIDTH), F32),
            pltpu.VMEM((seq, GROUP_WIDTH), F32),
            pltpu.VMEM((N_GROUPS * tiles, seq, LANES), F32),
            pltpu.VMEM((N_GROUPS * tiles, seq, LANES), F32),
        ],
        compiler_params=_params(48, ("arbitrary",)),
        name="attention",
    )(*[a.reshape(b, seq, ATTN_WIDTH) for a in qkv_groups])


def _gelu(x):
    return 0.5 * x * (1.0 + lax.erf(x * 0.7071067811865476))


def _post_kernel(attn_ref, u_ref, vz_ref, ga_ref, gb_ref, x_ref, wa_ref, wb_ref, wo_ref,
                 lng_ref, lnb_ref, wcat_ref, bcat_ref, n2g_ref, wrh_ref, wrl_ref, br_ref,
                 x1_ref, hf_ref, e_ref, g_ref, cst_ref, cnt_ref, sgu_s, carry_s):
    tm = x_ref.shape[0]
    step = pl.program_id(0)

    @pl.when(step == 0)
    def _():
        carry_s[...] = jnp.zeros_like(carry_s)

    lane = lax.broadcasted_iota(I32, (SGU_CHUNK, LANES), 1)
    low = lane < SGU_GROUP_DIM
    trow = lax.broadcasted_iota(I32, (SGU_CHUNK, 2 * SGU_CHUNK), 0)
    tcol = lax.broadcasted_iota(I32, (SGU_CHUNK, 2 * SGU_CHUNK), 1) % SGU_CHUNK
    causal = tcol <= trow

    def chunk(c):
        r0 = c * SGU_CHUNK
        u = _gelu(u_ref[pl.ds(r0, SGU_CHUNK), :].astype(F32))
        v = _gelu(vz_ref[pl.ds(r0, SGU_CHUNK), :].astype(F32))
        mu = jnp.mean(v, axis=-1, keepdims=True)
        vc = v - mu
        vn = vc * lax.rsqrt(jnp.mean(vc * vc, axis=-1, keepdims=True) + EPS)
        vn = (vn * lng_ref[...] + lnb_ref[...]).astype(BF16)
        zero = jnp.zeros((SGU_CHUNK, LANES), BF16)
        for j in range(SGU_WIDTH // LANES):
            vt = vn[:, j * LANES:(j + 1) * LANES]
            rhs = jnp.concatenate([jnp.where(low, vt, zero), jnp.where(low, zero, vt)], axis=0)
            wj = jnp.where(causal, wcat_ref[j], jnp.zeros((), BF16))
            mixed = _dot(wj, rhs) + bcat_ref[j]
            sgu_s[pl.ds(r0, SGU_CHUNK), j * LANES:(j + 1) * LANES] = (u[:, j * LANES:(j + 1) * LANES] * mixed).astype(BF16)

    for c in range(tm // SGU_CHUNK):
        chunk(c)

    y_a = _dot(attn_ref[...], wa_ref[...])
    y_b = _dot(sgu_s[...], wb_ref[...])
    merged = jax.nn.sigmoid(ga_ref[...].astype(F32)) * y_a + jax.nn.sigmoid(gb_ref[...].astype(F32)) * y_b
    x1 = x_ref[...] + _dot(merged.astype(BF16), wo_ref[...])
    x1_ref[...] = x1

    hf = x1 * lax.rsqrt(jnp.mean(x1 * x1, axis=-1, keepdims=True) + EPS) * n2g_ref[...]
    hi = hf.astype(BF16)
    hf_ref[...] = hi
    lo = (hf - hi.astype(F32)).astype(BF16)
    logits = _dot_nt(wrh_ref[...], hi) + _dot_nt(wrl_ref[...], hi) + _dot_nt(wrh_ref[...], lo) + br_ref[...]

    eio = lax.broadcasted_iota(I32, (N_EXPERTS, tm), 0)
    work = logits
    top_v, onehots = [], []
    for k in range(TOP_K):
        m = jnp.max(work, axis=0, keepdims=True)
        idx = jnp.min(jnp.where(work == m, eio, N_EXPERTS), axis=0, keepdims=True)
        oh = eio == idx
        work = jnp.where(oh, -jnp.inf, work)
        top_v.append(m)
        onehots.append(oh)
        e_ref[k:k + 1, :] = idx
    ex = [jnp.exp(v - top_v[0]) for v in top_v]
    inv = 1.0 / (ex[0] + ex[1] + ex[2] + ex[3])
    for k in range(TOP_K):
        g_ref[k:k + 1, :] = ex[k] * inv

    sel = jnp.zeros((N_EXPERTS, tm), F32)
    for oh in onehots:
        sel = jnp.where(oh, 1.0, sel)
    lane = lax.broadcasted_iota(I32, (N_EXPERTS, LANES), 1)
    run = carry_s[...]
    cst = jnp.zeros((N_EXPERTS, LANES), F32)
    for s in range(tm // TM_ROUTE):
        cst = jnp.where(lane == s, run, cst)
        run = run + jnp.sum(sel[:, s * TM_ROUTE:(s + 1) * TM_ROUTE], axis=1, keepdims=True)
    cst_ref[0] = cst.astype(I32)
    carry_s[...] = run
    cnt_ref[...] = run.astype(I32)


def _post(attn2, z2, gates2, x2, wa, wb, wo, lng, lnb, wcat, bcat, n2g, wrh, wrl, br):
    t = x2.shape[0]
    tm = TM_POST
    const = lambda *shape: pl.BlockSpec(shape, lambda i: (0,) * len(shape), pipeline_mode=pl.Buffered(1))
    return pl.pallas_call(
        _post_kernel,
        grid=(t // tm,),
        in_specs=[
            pl.BlockSpec((tm, ATTN_WIDTH), lambda i: (i, 0)),
            pl.BlockSpec((tm, SGU_WIDTH), lambda i: (i, 0)),
            pl.BlockSpec((tm, SGU_WIDTH), lambda i: (i, 1)),
            pl.BlockSpec((tm, D_MODEL), lambda i: (i, 0)),
            pl.BlockSpec((tm, D_MODEL), lambda i: (i, 1)),
            pl.BlockSpec((tm, D_MODEL), lambda i: (i, 0)),
            const(ATTN_WIDTH, D_MODEL),
            const(SGU_WIDTH, D_MODEL),
            const(D_MODEL, D_MODEL),
            const(1, SGU_WIDTH),
            const(1, SGU_WIDTH),
            const(SGU_WIDTH // LANES, SGU_CHUNK, 2 * SGU_CHUNK),
            const(SGU_WIDTH // LANES, SGU_CHUNK, LANES),
            const(1, D_MODEL),
            const(N_EXPERTS, D_MODEL),
            const(N_EXPERTS, D_MODEL),
            const(N_EXPERTS, 1),
        ],
        out_specs=[
            pl.BlockSpec((tm, D_MODEL), lambda i: (i, 0)),
            pl.BlockSpec((tm, D_MODEL), lambda i: (i, 0)),
            pl.BlockSpec((TOP_K, tm), lambda i: (0, i)),
            pl.BlockSpec((TOP_K, tm), lambda i: (0, i)),
            pl.BlockSpec((1, N_EXPERTS, LANES), lambda i: (i, 0, 0)),
            pl.BlockSpec((N_EXPERTS, LANES), lambda i: (0, 0)),
        ],
        out_shape=[
            jax.ShapeDtypeStruct((t, D_MODEL), F32),
            jax.ShapeDtypeStruct((t, D_MODEL), BF16),
            jax.ShapeDtypeStruct((TOP_K, t), I32),
            jax.ShapeDtypeStruct((TOP_K, t), F32),
            jax.ShapeDtypeStruct((t // tm, N_EXPERTS, LANES), I32),
            jax.ShapeDtypeStruct((N_EXPERTS, LANES), I32),
        ],
        scratch_shapes=[
            pltpu.VMEM((tm, SGU_WIDTH), BF16),
            pltpu.VMEM((N_EXPERTS, LANES), F32),
        ],
        compiler_params=_params(48, ("arbitrary",)),
        name="post",
    )(attn2, z2, z2, gates2, gates2, x2, wa, wb, wo, lng, lnb, wcat, bcat, n2g, wrh, wrl, br)


def _route_rows(e_ref, tm):
    e = e_ref[...]
    eio = lax.broadcasted_iota(I32, (N_EXPERTS, tm), 0)
    onehots = [eio == e[k:k + 1, :] for k in range(TOP_K)]
    sel = jnp.zeros((N_EXPERTS, tm), F32)
    for oh in onehots:
        sel = jnp.where(oh, 1.0, sel)
    ti = lax.broadcasted_iota(I32, (tm, tm), 0)
    tj = lax.broadcasted_iota(I32, (tm, tm), 1)
    before = jnp.where(ti < tj, 1.0, 0.0).astype(BF16)
    slot = _dot(sel.astype(BF16), before)
    tiles = jnp.floor((jnp.sum(sel, axis=1, keepdims=True) + (SUBLANES - 1)) * (1.0 / SUBLANES))
    xi = lax.broadcasted_iota(I32, (N_EXPERTS, N_EXPERTS), 0)
    xj = lax.broadcasted_iota(I32, (N_EXPERTS, N_EXPERTS), 1)
    lower = jnp.where(xj < xi, 1.0, 0.0).astype(BF16)
    off = _dot(lower, jnp.broadcast_to(tiles, (N_EXPERTS, LANES)).astype(BF16))[:, 0:1] * float(SUBLANES)
    row = off + slot
    return [jnp.sum(jnp.where(oh, row, 0.0), axis=0, keepdims=True) for oh in onehots]


def _pick(rows, values, default):
    tm = rows[0].shape[1]
    rio = lax.broadcasted_iota(I32, (ROUTE_ROWS, tm), 0).astype(F32)
    out = default
    for r, v in zip(rows, values):
        out = jnp.where(rio == r, v, out)
    return out


def _wait_rows(buf, hbm_ref, sem, n):
    n = pl.multiple_of(n, SUBLANES)

    @pl.when(n > 0)
    def _():
        pltpu.make_async_copy(buf.at[pl.ds(0, n), :], hbm_ref.at[pl.ds(0, n), :], sem).wait()


def _dispatch_kernel(start_ref, off_ref, cnt_ref, tot_ref, za_ref, zb_ref, hf_ref, e_ref, g_ref, xs_ref,
                     obuf, zbuf, sem, semz):
    tm = hf_ref.shape[0]
    i = pl.program_id(0)
    last = pl.num_programs(0) - 1
    cur = lax.rem(i, 2)
    rows = _route_rows(e_ref, tm)
    g = g_ref[...]
    pt = _pick(rows, [1.0] * TOP_K, 0.0).astype(BF16)
    gate = jnp.sum(_pick(rows, [g[k:k + 1, :] for k in range(TOP_K)], 0.0), axis=1, keepdims=True)

    @pl.when(i >= 2)
    def _():
        _wait_rows(obuf.at[cur], xs_ref, sem.at[cur], tot_ref[jnp.maximum(i - 2, 0)])

    obuf[cur, :, 0:D_MODEL] = _dot(pt, hf_ref[...])
    obuf[cur, :, D_MODEL:XS_WIDTH] = jnp.broadcast_to(gate, (ROUTE_ROWS, LANES))

    for x in range(N_EXPERTS):
        n = pl.multiple_of(cnt_ref[i * N_EXPERTS + x], SUBLANES)

        @pl.when(n > 0)
        def _(x=x, n=n):
            src = pl.multiple_of(off_ref[i * N_EXPERTS + x], SUBLANES)
            dst = pl.multiple_of(start_ref[i * N_EXPERTS + x], SUBLANES)
            pltpu.make_async_copy(obuf.at[cur, pl.ds(src, n), :], xs_ref.at[pl.ds(dst, n), :], sem.at[cur]).start()

    @pl.when(i == last)
    def _():
        @pl.when(i >= 1)
        def _():
            _wait_rows(obuf.at[1 - cur], xs_ref, sem.at[1 - cur], tot_ref[jnp.maximum(i - 1, 0)])

        _wait_rows(obuf.at[cur], xs_ref, sem.at[cur], tot_ref[i])
        zbuf[...] = jnp.zeros_like(zbuf)

        def zero_region(z, act):
            n = zb_ref[z] - za_ref[z]
            big = n // EXPERT_TILE
            rest = pl.multiple_of(n - big * EXPERT_TILE, SUBLANES)

            def piece(row, nrows):
                return pltpu.make_async_copy(zbuf.at[pl.ds(0, nrows), :],
                                             xs_ref.at[pl.ds(pl.multiple_of(row, SUBLANES), nrows), :], semz)

            lax.fori_loop(0, big, lambda m, c: (act(piece(za_ref[z] + m * EXPERT_TILE, EXPERT_TILE)), c)[1], 0)

            @pl.when(rest > 0)
            def _():
                act(piece(za_ref[z] + big * EXPERT_TILE, rest))

        for z in range(N_EXPERTS + 1):
            zero_region(z, lambda cp: cp.start())
        for z in range(N_EXPERTS + 1):
            zero_region(z, lambda cp: cp.wait())


def _dispatch(start, off, cnt, tot, za, zb, hf, e4, g4, cap):
    t = hf.shape[0]
    tm = TM_ROUTE
    tok = lambda i, *_: (0, i)
    return pl.pallas_call(
        _dispatch_kernel,
        grid_spec=pltpu.PrefetchScalarGridSpec(
            num_scalar_prefetch=6,
            grid=(t // tm,),
            in_specs=[
                pl.BlockSpec((tm, D_MODEL), lambda i, *_: (i, 0)),
                pl.BlockSpec((TOP_K, tm), tok),
                pl.BlockSpec((TOP_K, tm), tok),
            ],
            out_specs=pl.BlockSpec(memory_space=pl.ANY),
            scratch_shapes=[
                pltpu.VMEM((2, ROUTE_ROWS, XS_WIDTH), F32),
                pltpu.VMEM((EXPERT_TILE, XS_WIDTH), F32),
                pltpu.SemaphoreType.DMA((2,)),
                pltpu.SemaphoreType.DMA(()),
            ],
        ),
        out_shape=jax.ShapeDtypeStruct((cap, XS_WIDTH), F32),
        compiler_params=_params(48, ("arbitrary",)),
        name="dispatch",
    )(start, off, cnt, tot, za, zb, hf, e4, g4)


def _expert_kernel(te_ref, nu_ref, xs_ref, wgu_ref, bgu_ref, wdn_ref, bdn_ref, ys_ref, wgu_s, wdn_s):
    i = pl.program_id(0)

    @pl.when(i < nu_ref[0])
    def _():
        @pl.when((i == 0) | (te_ref[i] != te_ref[jnp.maximum(i - 1, 0)]))
        def _():
            wgu_s[...] = wgu_ref[0].astype(BF16)
            wdn_s[...] = wdn_ref[0].astype(BF16)

        x = xs_ref[:, 0:D_MODEL].astype(BF16)
        route_gate = xs_ref[:, D_MODEL:D_MODEL + 1]
        gu = _dot(x, wgu_s[...]) + bgu_ref[0]
        gate = jnp.minimum(gu[:, :D_EXPERT], SWIGLU_LIMIT)
        up = jnp.clip(gu[:, D_EXPERT:], -SWIGLU_LIMIT, SWIGLU_LIMIT)
        act = (up + 1.0) * (gate * jax.nn.sigmoid(SWIGLU_ALPHA * gate))
        ys_ref[...] = (_dot(act.astype(BF16), wdn_s[...]) + bdn_ref[0]) * route_gate

    @pl.when(i >= nu_ref[0])
    def _():
        ys_ref[...] = jnp.zeros_like(ys_ref)


def _experts(tile_expert, n_used, xs, wgu, bgu, wdn, bdn):
    cap = xs.shape[0]
    n_tiles = cap // EXPERT_TILE
    row = lambda i, te, nu: (jnp.minimum(i, nu[0] - 1), 0)
    by_expert = lambda i, te, nu: (te[i], 0, 0)
    return pl.pallas_call(
        _expert_kernel,
        grid_spec=pltpu.PrefetchScalarGridSpec(
            num_scalar_prefetch=2,
            grid=(n_tiles,),
            in_specs=[
                pl.BlockSpec((EXPERT_TILE, XS_WIDTH), row),
                pl.BlockSpec((1, D_MODEL, 2 * D_EXPERT), by_expert),
                pl.BlockSpec((1, 1, 2 * D_EXPERT), by_expert),
                pl.BlockSpec((1, D_EXPERT, D_MODEL), by_expert),
                pl.BlockSpec((1, 1, D_MODEL), by_expert),
            ],
            out_specs=pl.BlockSpec((EXPERT_TILE, D_MODEL), lambda i, te, nu: (i, 0)),
            scratch_shapes=[
                pltpu.VMEM((D_MODEL, 2 * D_EXPERT), BF16),
                pltpu.VMEM((D_EXPERT, D_MODEL), BF16),
            ],
        ),
        out_shape=jax.ShapeDtypeStruct((cap, D_MODEL), F32),
        compiler_params=_params(56, ("arbitrary",)),
        name="experts",
    )(tile_expert, n_used, xs, wgu, bgu, wdn, bdn)


def _combine_kernel(start_ref, off_ref, cnt_ref, tot_ref, x1_ref, e_ref, ys_ref, o_ref, sbuf, sem):
    tm = x1_ref.shape[0]
    i = pl.program_id(0)
    last = pl.num_programs(0) - 1
    cur = lax.rem(i, 2)

    def fetch(step, b):
        for x in range(N_EXPERTS):
            n = pl.multiple_of(cnt_ref[step * N_EXPERTS + x], SUBLANES)

            @pl.when(n > 0)
            def _(x=x, n=n):
                src = pl.multiple_of(start_ref[step * N_EXPERTS + x], SUBLANES)
                dst = pl.multiple_of(off_ref[step * N_EXPERTS + x], SUBLANES)
                pltpu.make_async_copy(ys_ref.at[pl.ds(src, n), :], sbuf.at[b, pl.ds(dst, n), :], sem.at[b]).start()

    @pl.when(i == 0)
    def _():
        sbuf[...] = jnp.zeros_like(sbuf)
        fetch(i, cur)

    @pl.when(i < last)
    def _():
        fetch(i + 1, 1 - cur)

    rows = _route_rows(e_ref, tm)
    pt = _pick(rows, [1.0] * TOP_K, 0.0).astype(BF16)
    _wait_rows(sbuf.at[cur], ys_ref, sem.at[cur], tot_ref[i])
    picked = lax.dot_general(pt, sbuf[cur].astype(BF16), _TN, preferred_element_type=F32)
    o_ref[...] = x1_ref[...] + picked


def _combine(start, off, cnt, tot, x1, e4, ys):
    t = x1.shape[0]
    tm = TM_ROUTE
    return pl.pallas_call(
        _combine_kernel,
        grid_spec=pltpu.PrefetchScalarGridSpec(
            num_scalar_prefetch=4,
            grid=(t // tm,),
            in_specs=[
                pl.BlockSpec((tm, D_MODEL), lambda i, *_: (i, 0)),
                pl.BlockSpec((TOP_K, tm), lambda i, *_: (0, i)),
                pl.BlockSpec(memory_space=pl.ANY),
            ],
            out_specs=pl.BlockSpec((tm, D_MODEL), lambda i, *_: (i, 0)),
            scratch_shapes=[
                pltpu.VMEM((2, ROUTE_ROWS, D_MODEL), F32),
                pltpu.SemaphoreType.DMA((2,)),
            ],
        ),
        out_shape=jax.ShapeDtypeStruct((t, D_MODEL), F32),
        compiler_params=_params(48, ("arbitrary",)),
        name="combine",
    )(start, off, cnt, tot, x1, e4, ys)


def _layer(x, positions, norm1_g, w_in, q_norm_g, k_norm_g, sgu_ln_g, sgu_ln_b, w_spatial, b_spatial,
           w_branch_a, w_branch_b, w_out, norm2_g, w_router, b_router, w_gate_up, b_gate_up, w_down, b_down):
    b, seq, _ = x.shape
    t = b * seq
    x2 = x.reshape(t, D_MODEL)

    inv_freq = ROPE_THETA ** (-jnp.arange(ROT_HALF, dtype=F32) / ROT_HALF)
    invf = jnp.tile(inv_freq, LANES // ROT_HALF).reshape(1, LANES)
    dim_of_lane = [(l // (2 * ROT_HALF)) * ROT_HALF + l % ROT_HALF for l in range(LANES)]
    qg = q_norm_g[jnp.array(dim_of_lane)].reshape(1, LANES)
    kg = k_norm_g[jnp.array(dim_of_lane)].reshape(1, LANES)
    qk_cols = _qk_column_order()
    cols = qk_cols + [ATTN_WIDTH + c for c in qk_cols] + list(range(2 * ATTN_WIDTH, w_in.shape[1]))
    w_in_p = w_in[:, jnp.array(cols)].astype(BF16)
    n_pairs = SGU_WIDTH // LANES
    wcat = w_spatial.reshape(n_pairs, 2, SGU_CHUNK, SGU_CHUNK).transpose(0, 2, 1, 3).reshape(n_pairs, SGU_CHUNK, 2 * SGU_CHUNK)
    bcat = jnp.repeat(b_spatial.reshape(n_pairs, 2, SGU_CHUNK).transpose(0, 2, 1), SGU_GROUP_DIM, axis=2)
    wr_t = w_router.T
    wr_hi = wr_t.astype(BF16)
    wr_lo = (wr_t - wr_hi.astype(F32)).astype(BF16)

    *qkv_groups, z, gates = _inproj(x2, norm1_g.reshape(1, D_MODEL), w_in_p, positions.reshape(-1, LANES // ROT_HALF), invf, qg, kg, b, seq)
    attn = _attention(qkv_groups)
    x1, hf, e4, g4, cst, counts = _post(
        attn.reshape(t, ATTN_WIDTH), z, gates, x2,
        w_branch_a.astype(BF16), w_branch_b.astype(BF16), w_out.astype(BF16),
        sgu_ln_g.reshape(1, SGU_WIDTH), sgu_ln_b.reshape(1, SGU_WIDTH), wcat.astype(BF16), bcat,
        norm2_g.reshape(1, D_MODEL), wr_hi, wr_lo, b_router.reshape(N_EXPERTS, 1))

    steps_per_tile = TM_POST // TM_ROUTE
    n_steps = t // TM_ROUTE
    before = cst[:, :, :steps_per_tile].transpose(0, 2, 1).reshape(n_steps, N_EXPERTS)
    total = counts[:, 0]
    cnt = jnp.concatenate([before[1:], total[None, :]], axis=0) - before
    cnt_al = (cnt + SUBLANES - 1) // SUBLANES * SUBLANES
    off = (jnp.cumsum(cnt_al, axis=1) - cnt_al).astype(I32).reshape(-1)
    tot = jnp.sum(cnt_al, axis=1).astype(I32)
    used = jnp.sum(cnt_al, axis=0)
    region = (used + EXPERT_TILE - 1) // EXPERT_TILE * EXPERT_TILE
    rend = jnp.cumsum(region)
    rstart = rend - region
    start = (rstart[None, :] + jnp.cumsum(cnt_al, axis=0) - cnt_al).astype(I32).reshape(-1)
    bound = t * TOP_K + n_steps * N_EXPERTS * (SUBLANES - 1) + N_EXPERTS * (EXPERT_TILE - 1)
    cap = (bound + EXPERT_TILE - 1) // EXPERT_TILE * EXPERT_TILE
    n_tiles = cap // EXPERT_TILE
    tile_expert = jnp.sum(jnp.arange(n_tiles, dtype=I32)[:, None] * EXPERT_TILE >= rend[None, :], axis=1)
    tile_expert = jnp.minimum(tile_expert, N_EXPERTS - 1).astype(I32)
    n_used = (rend[-1:] // EXPERT_TILE).astype(I32)
    za = jnp.concatenate([rstart + used, rend[-1:]]).astype(I32)
    zb = jnp.concatenate([rend, jnp.full((1,), cap, rend.dtype)]).astype(I32)
    cnt_al = cnt_al.astype(I32).reshape(-1)

    xs = _dispatch(start, off, cnt_al, tot, za, zb, hf, e4, g4, cap)
    ys = _experts(tile_expert, n_used, xs,
                  w_gate_up, b_gate_up.reshape(N_EXPERTS, 1, 2 * D_EXPERT),
                  w_down, b_down.reshape(N_EXPERTS, 1, D_MODEL))
    out = _combine(start, off, cnt_al, tot, x1, e4, ys)
    return out.reshape(b, seq, D_MODEL)


def kernel(x, positions, norm1_g, w_in, q_norm_g, k_norm_g, sgu_ln_g, sgu_ln_b, w_spatial, b_spatial, w_branch_a, w_branch_b, w_out, norm2_g, w_router, b_router, w_gate_up, b_gate_up, w_down, b_down):
    for layer in range(norm1_g.shape[0]):
        x = _layer(x, positions, norm1_g[layer], w_in[layer], q_norm_g[layer], k_norm_g[layer],
                   sgu_ln_g[layer], sgu_ln_b[layer], w_spatial[layer], b_spatial[layer],
                   w_branch_a[layer], w_branch_b[layer], w_out[layer], norm2_g[layer],
                   w_router[layer], b_router[layer], w_gate_up[layer], b_gate_up[layer],
                   w_down[layer], b_down[layer])
    return x
```

```python
import functools

import jax
import jax.numpy as jnp
from jax import lax
from jax.experimental import pallas as pl
from jax.experimental.pallas import tpu as pltpu

F32 = jnp.float32
BF16 = jnp.bfloat16
I32 = jnp.int32

D_MODEL = 1024
HEAD_DIM = 64
ROT_HALF = HEAD_DIM // 2
LOG2E = 1.4426950408889634
LN2 = 0.6931471805599453
Q_SCALE = HEAD_DIM ** -0.5 * LOG2E
ATTN_GROUPS = ((128, 1), (512, 4), (2048, 16))
N_GROUPS = len(ATTN_GROUPS)
GROUP_WIDTH = 256
ATTN_WIDTH = N_GROUPS * GROUP_WIDTH
BLK = 128
ROPE_THETA = 10000.0
SGU_CHUNK = 128
SGU_GROUP_DIM = 64
SGU_WIDTH = 768
GATE_WIDTH = 2 * D_MODEL
N_EXPERTS = 32
TOP_K = 4
D_EXPERT = D_MODEL
SWIGLU_LIMIT = 7.0
SWIGLU_ALPHA = 1.702
EPS = 1e-6
NEG_INF = -1e30
LANES = 128

TM_INPROJ = 512
TM_POST = 512
TM_ROUTE = 256
SUBLANES = 8
ROUTE_ROWS = TM_ROUTE * TOP_K + N_EXPERTS * SUBLANES
EXPERT_TILE = 512
XS_WIDTH = D_MODEL + LANES
ROW_CHUNK = 256

_NT = (((1,), (1,)), ((), ()))
_TN = (((0,), (0,)), ((), ()))


def _dot(a, b):
    return jnp.dot(a, b, preferred_element_type=F32)


def _dot_nt(a, b):
    return lax.dot_general(a, b, _NT, preferred_element_type=F32)


def _params(vmem_mb, sem):
    return pltpu.CompilerParams(dimension_semantics=sem, vmem_limit_bytes=vmem_mb * 1024 * 1024)


def _split3(x):
    a = x.astype(BF16)
    r = x - a.astype(F32)
    b = r.astype(BF16)
    return a, b, (r - b.astype(F32)).astype(BF16)


def _inproj_kernel(x_ref, g_ref, w_ref, pos_ref, invf_ref, qg_ref, kg_ref,
                   o0_ref, o1_ref, o2_ref, oz_ref, og_ref, raw_a, raw_b, stage_s, cos_s, sin_s):
    step = pl.program_id(0)

    @pl.when(step == 0)
    def _():
        raw_b[...] = jnp.zeros_like(raw_b)

    args = (x_ref, g_ref, w_ref, pos_ref, invf_ref, qg_ref, kg_ref, o0_ref, o1_ref, o2_ref, oz_ref, og_ref)

    @pl.when(lax.rem(step, 2) == 0)
    def _():
        _inproj_step(*args, raw_a, raw_b, stage_s, cos_s, sin_s)

    @pl.when(lax.rem(step, 2) == 1)
    def _():
        _inproj_step(*args, raw_b, raw_a, stage_s, cos_s, sin_s)


def _inproj_step(x_ref, g_ref, w_ref, pos_ref, invf_ref, qg_ref, kg_ref,
                 o0_ref, o1_ref, o2_ref, oz_ref, og_ref, raw_w, raw_r, stage_s, cos_s, sin_s):
    tm = x_ref.shape[0]
    x = x_ref[...]
    y = x * lax.rsqrt(jnp.mean(x * x, axis=-1, keepdims=True) + EPS)
    h = (y * g_ref[...]).astype(BF16)

    def project(lo):
        p = _dot(h, w_ref[:, lo:lo + GROUP_WIDTH])
        zc = 3 * ATTN_WIDTH
        gc = zc + 2 * SGU_WIDTH
        if lo < zc:
            raw_w[:, lo:lo + GROUP_WIDTH] = p
        elif lo < gc:
            oz_ref[:, lo - zc:lo - zc + GROUP_WIDTH] = p.astype(BF16)
        else:
            og_ref[:, lo - gc:lo - gc + GROUP_WIDTH] = p.astype(BF16)

    def rotary_tables():
        per_row = LANES // ROT_HALF
        lane = lax.broadcasted_iota(I32, (tm // per_row, LANES), 1)
        p4 = pos_ref[...].astype(F32)
        posd = p4[:, per_row - 1:per_row]
        for j in range(per_row - 2, -1, -1):
            posd = jnp.where(lane < (j + 1) * ROT_HALF, p4[:, j:j + 1], posd)
        ang = posd * invf_ref[...]
        cparts = _split3(jnp.cos(ang))
        sparts = _split3(jnp.sin(ang))
        src = lax.broadcasted_iota(I32, (LANES, LANES), 0)
        dst = lax.broadcasted_iota(I32, (LANES, LANES), 1)
        for j in range(per_row):
            pick = src == j * ROT_HALF + dst % ROT_HALF
            spread = jnp.where(pick, 1.0, 0.0).astype(BF16)
            spread_neg = jnp.where(pick, jnp.where(dst < 2 * ROT_HALF, -1.0, 1.0), 0.0).astype(BF16)
            cos_s[pl.ds(j, tm // per_row, stride=per_row), :] = sum(_dot(p, spread) for p in cparts)
            sin_s[pl.ds(j, tm // per_row, stride=per_row), :] = sum(_dot(p, spread_neg) for p in sparts)

    li = (lax.broadcasted_iota(I32, (LANES, LANES), 0) // ROT_HALF) % 2
    lj = (lax.broadcasted_iota(I32, (LANES, LANES), 1) // ROT_HALF) % 2
    head_sum = jnp.where(li == lj, 1.0, 0.0).astype(BF16)
    outs = (o0_ref, o1_ref, o2_ref)

    def finish(gi, d, ti, mode, gain_ref, c, stage):
        col = ti * ATTN_WIDTH + gi * GROUP_WIDTH
        t = raw_r[:, col + c * LANES:col + (c + 1) * LANES]
        if mode != "v":
            ss = t * t
            hi = ss.astype(BF16)
            lo = (ss - hi.astype(F32)).astype(BF16)
            ms = (_dot(hi, head_sum) + _dot(lo, head_sum)) * (1.0 / HEAD_DIM)
            t = t * lax.rsqrt(ms + EPS) * gain_ref[...]
            t = t * cos_s[...] + pltpu.roll(t, 2 * ROT_HALF, 1) * sin_s[...]
            if mode == "q":
                t = t * Q_SCALE
        lanes = slice(ti * GROUP_WIDTH + c * LANES, ti * GROUP_WIDTH + (c + 1) * LANES)
        if d == 1:
            outs[gi][0, 0, :, lanes] = t.astype(BF16)
        else:
            stage_s[stage] = t
            for r in range(d):
                outs[gi][0, r, :, lanes] = stage_s[stage, pl.ds(r, tm // d, stride=d), :].astype(BF16)

    pieces = [rotary_tables]
    stage = 0
    for gi, (_, d) in enumerate(ATTN_GROUPS):
        for ti, (mode, gain_ref) in enumerate((("q", qg_ref), ("k", kg_ref), ("v", None))):
            for c in range(GROUP_WIDTH // LANES):
                pieces.append(functools.partial(finish, gi, d, ti, mode, gain_ref, c, stage))
                stage += d > 1

    chunks = list(range(0, w_ref.shape[1], GROUP_WIDTH))
    for k in range(max(len(chunks), len(pieces))):
        if k < len(chunks):
            project(chunks[k])
        if k < len(pieces):
            pieces[k]()


def _inproj(x2, g, w, pos2, invf, qg, kg, b, seq):
    t = x2.shape[0]
    n = w.shape[1]
    tm = TM_INPROJ
    nj = seq // tm
    n_tiles = b * nj
    n_stage = sum(1 for _, d in ATTN_GROUPS if d > 1) * 3 * (GROUP_WIDTH // LANES)
    proj_row = lambda s: (jnp.minimum(s, n_tiles - 1), 0)
    done_row = lambda s: (jnp.maximum(s - 1, 0), 0)
    done_blk = lambda s: (jnp.maximum(s - 1, 0) // nj, 0, jnp.maximum(s - 1, 0) % nj, 0)
    const = lambda s: (0, 0)
    per_row = LANES // ROT_HALF
    return pl.pallas_call(
        _inproj_kernel,
        grid=(n_tiles + 1,),
        in_specs=[
            pl.BlockSpec((tm, D_MODEL), proj_row),
            pl.BlockSpec((1, D_MODEL), const),
            pl.BlockSpec((D_MODEL, n), const, pipeline_mode=pl.Buffered(1)),
            pl.BlockSpec((tm // per_row, per_row), done_row),
            pl.BlockSpec((1, LANES), const),
            pl.BlockSpec((1, LANES), const),
            pl.BlockSpec((1, LANES), const),
        ],
        out_specs=[pl.BlockSpec((1, d, tm // d, ATTN_WIDTH), done_blk) for _, d in ATTN_GROUPS] + [
            pl.BlockSpec((tm, 2 * SGU_WIDTH), proj_row),
            pl.BlockSpec((tm, GATE_WIDTH), proj_row),
        ],
        out_shape=[jax.ShapeDtypeStruct((b, d, seq // d, ATTN_WIDTH), BF16) for _, d in ATTN_GROUPS] + [
            jax.ShapeDtypeStruct((t, 2 * SGU_WIDTH), BF16),
            jax.ShapeDtypeStruct((t, GATE_WIDTH), BF16),
        ],
        scratch_shapes=[
            pltpu.VMEM((tm, 3 * ATTN_WIDTH), F32),
            pltpu.VMEM((tm, 3 * ATTN_WIDTH), F32),
            pltpu.VMEM((n_stage, tm, LANES), F32),
            pltpu.VMEM((tm, LANES), F32),
            pltpu.VMEM((tm, LANES), F32),
        ],
        compiler_params=_params(56, ("arbitrary",)),
        name="inproj",
    )(x2, g, w, pos2, invf, qg, kg)


def _attn_blocks(qkv_ref, ores_s, lres_s, seq, d):
    sub = seq // d
    nb = sub // BLK
    win = 2 * BLK if nb > 1 else BLK
    log2d = d.bit_length() - 1
    lane = lax.broadcasted_iota(I32, (BLK, GROUP_WIDTH), 1)
    q_head = 2 * (lane // LANES) + (lane // ROT_HALF) % 2
    lane_head = lane // HEAD_DIM
    qi = lax.broadcasted_iota(I32, (BLK, win), 0)
    kj = lax.broadcasted_iota(I32, (BLK, win), 1)
    rel = qi - kj

    def body(idx, carry):
        r = idx & (d - 1)
        n = idx >> log2d
        row0 = pl.multiple_of(r * sub + n * BLK, BLK)
        if nb > 1:
            kn = jnp.maximum(n - 1, 0)
            k0 = pl.multiple_of(r * sub + kn * BLK, BLK)
            dist = rel + (n - kn) * BLK
        else:
            k0 = row0
            dist = rel
        valid = lax.bitcast_convert_type(dist, jnp.uint32) <= jnp.uint32(BLK)
        q = qkv_ref[0, pl.ds(row0, BLK), 0:GROUP_WIDTH]
        kw = qkv_ref[0, pl.ds(k0, win), GROUP_WIDTH:2 * GROUP_WIDTH]
        vw = qkv_ref[0, pl.ds(k0, win), 2 * GROUP_WIDTH:3 * GROUP_WIDTH]
        zero = jnp.zeros_like(q)
        qm = jnp.concatenate([jnp.where(q_head == h, q, zero) for h in range(4)], axis=0)
        s = _dot_nt(qm, kw)
        ps, ms, ls = [], [], []
        for h in range(4):
            sh = jnp.where(valid, s[h * BLK:(h + 1) * BLK], NEG_INF)
            m = jnp.max(sh, axis=-1, keepdims=True)
            p = jnp.exp2(sh - m)
            ls.append(jnp.sum(p, axis=-1, keepdims=True))
            ms.append(m)
            ps.append(p.astype(BF16))
        pv = _dot(jnp.concatenate(ps, axis=0), vw)
        o = jnp.zeros((BLK, GROUP_WIDTH), F32)
        lse = jnp.zeros((BLK, GROUP_WIDTH), F32)
        for h in range(4):
            sel = lane_head == h
            o = jnp.where(sel, pv[h * BLK:(h + 1) * BLK] * (1.0 / ls[h]), o)
            lse = jnp.where(sel, (ms[h] + jnp.log2(ls[h])) * LN2, lse)
        ores_s[pl.ds(row0, BLK), :] = o
        lres_s[pl.ds(row0, BLK), :] = lse
        return carry

    lax.fori_loop(0, seq // BLK, body, 0, unroll=8)


def _attn_kernel(g0_ref, g1_ref, g2_ref, o_ref, ores_s, lres_s, onat_s, lnat_s, *, seq):
    for gi, ((_, d), qkv_ref) in enumerate(zip(ATTN_GROUPS, (g0_ref, g1_ref, g2_ref))):
        _attn_blocks(qkv_ref, ores_s, lres_s, seq, d)
        sub = seq // d
        for r in range(d):
            for c in range(GROUP_WIDTH // LANES):
                t = gi * (GROUP_WIDTH // LANES) + c
                onat_s[t, pl.ds(r, sub, stride=d), :] = ores_s[r * sub:(r + 1) * sub, c * LANES:(c + 1) * LANES]
                lnat_s[t, pl.ds(r, sub, stride=d), :] = lres_s[r * sub:(r + 1) * sub, c * LANES:(c + 1) * LANES]

    tiles = GROUP_WIDTH // LANES

    def body(i, carry):
        r0 = pl.multiple_of(i * ROW_CHUNK, ROW_CHUNK)
        for c in range(tiles):
            ls = [lnat_s[gi * tiles + c, pl.ds(r0, ROW_CHUNK), :] for gi in range(N_GROUPS)]
            m = jnp.maximum(jnp.maximum(ls[0], ls[1]), ls[2])
            es = [jnp.exp(l - m) for l in ls]
            inv = 1.0 / (es[0] + es[1] + es[2])
            for gi in range(N_GROUPS):
                t = gi * tiles + c
                o_ref[0, pl.ds(r0, ROW_CHUNK), t * LANES:(t + 1) * LANES] = (
                    onat_s[t, pl.ds(r0, ROW_CHUNK), :] * (es[gi] * inv)).astype(BF16)
        return carry

    lax.fori_loop(0, seq // ROW_CHUNK, body, 0, unroll=2)


def _attention(qkv_groups):
    b = qkv_groups[0].shape[0]
    seq = qkv_groups[0].shape[1] * qkv_groups[0].shape[2]
    tiles = GROUP_WIDTH // LANES
    blk = pl.BlockSpec((1, seq, ATTN_WIDTH), lambda i: (i, 0, 0))
    return pl.pallas_call(
        functools.partial(_attn_kernel, seq=seq),
        grid=(b,),
        in_specs=[blk] * N_GROUPS,
        out_specs=blk,
        out_shape=jax.ShapeDtypeStruct((b, seq, ATTN_WIDTH), BF16),
        scratch_shapes=[
            pltpu.VMEM((seq, GROUP_WIDTH), F32),
            pltpu.VMEM((seq, GROUP_WIDTH), F32),
            pltpu.VMEM((N_GROUPS * tiles, seq, LANES), F32),
            pltpu.VMEM((N_GROUPS * tiles, seq, LANES), F32),
        ],
        compiler_params=_params(48, ("arbitrary",)),
        name="attention",
    )(*[a.reshape(b, seq, ATTN_WIDTH) for a in qkv_groups])


def _gelu(x):
    return 0.5 * x * (1.0 + lax.erf(x * 0.7071067811865476))


def _post_kernel(attn_ref, u_ref, vz_ref, ga_ref, gb_ref, x_ref, wa_ref, wb_ref, wo_ref,
                 lng_ref, lnb_ref, wcat_ref, bcat_ref, n2g_ref, wrh_ref, wrl_ref, br_ref,
                 x1_ref, hf_ref, e_ref, g_ref, cst_ref, cnt_ref, sgu_s, carry_s):
    tm = x_ref.shape[0]
    step = pl.program_id(0)

    @pl.when(step == 0)
    def _():
        carry_s[...] = jnp.zeros_like(carry_s)

    lane = lax.broadcasted_iota(I32, (SGU_CHUNK, LANES), 1)
    low = lane < SGU_GROUP_DIM
    trow = lax.broadcasted_iota(I32, (SGU_CHUNK, 2 * SGU_CHUNK), 0)
    tcol = lax.broadcasted_iota(I32, (SGU_CHUNK, 2 * SGU_CHUNK), 1) % SGU_CHUNK
    causal = tcol <= trow

    def chunk(c):
        r0 = c * SGU_CHUNK
        u = _gelu(u_ref[pl.ds(r0, SGU_CHUNK), :].astype(F32))
        v = _gelu(vz_ref[pl.ds(r0, SGU_CHUNK), :].astype(F32))
        mu = jnp.mean(v, axis=-1, keepdims=True)
        vc = v - mu
        vn = vc * lax.rsqrt(jnp.mean(vc * vc, axis=-1, keepdims=True) + EPS)
        vn = (vn * lng_ref[...] + lnb_ref[...]).astype(BF16)
        zero = jnp.zeros((SGU_CHUNK, LANES), BF16)
        for j in range(SGU_WIDTH // LANES):
            vt = vn[:, j * LANES:(j + 1) * LANES]
            rhs = jnp.concatenate([jnp.where(low, vt, zero), jnp.where(low, zero, vt)], axis=0)
            wj = jnp.where(causal, wcat_ref[j], jnp.zeros((), BF16))
            mixed = _dot(wj, rhs) + bcat_ref[j]
            sgu_s[pl.ds(r0, SGU_CHUNK), j * LANES:(j + 1) * LANES] = (u[:, j * LANES:(j + 1) * LANES] * mixed).astype(BF16)

    n_chunks = tm // SGU_CHUNK
    wa_cols = D_MODEL // n_chunks
    y_a_parts = []
    for c in range(n_chunks):
        chunk(c)
        y_a_parts.append(_dot(attn_ref[...], wa_ref[:, c * wa_cols:(c + 1) * wa_cols]))
    y_a = jnp.concatenate(y_a_parts, axis=1)

    y_b = _dot(sgu_s[...], wb_ref[...])
    merged = jax.nn.sigmoid(ga_ref[...].astype(F32)) * y_a + jax.nn.sigmoid(gb_ref[...].astype(F32)) * y_b
    x1 = x_ref[...] + _dot(merged.astype(BF16), wo_ref[...])
    x1_ref[...] = x1

    hf = x1 * lax.rsqrt(jnp.mean(x1 * x1, axis=-1, keepdims=True) + EPS) * n2g_ref[...]
    hi = hf.astype(BF16)
    hf_ref[...] = hi
    lo = (hf - hi.astype(F32)).astype(BF16)
    logits = _dot_nt(wrh_ref[...], hi) + _dot_nt(wrl_ref[...], hi) + _dot_nt(wrh_ref[...], lo) + br_ref[...]

    eio = lax.broadcasted_iota(I32, (N_EXPERTS, tm), 0)
    work = logits
    top_v, onehots = [], []
    for k in range(TOP_K):
        m = jnp.max(work, axis=0, keepdims=True)
        idx = jnp.min(jnp.where(work == m, eio, N_EXPERTS), axis=0, keepdims=True)
        oh = eio == idx
        work = jnp.where(oh, -jnp.inf, work)
        top_v.append(m)
        onehots.append(oh)
        e_ref[k:k + 1, :] = idx
    ex = [jnp.exp(v - top_v[0]) for v in top_v]
    inv = 1.0 / (ex[0] + ex[1] + ex[2] + ex[3])
    for k in range(TOP_K):
        g_ref[k:k + 1, :] = ex[k] * inv

    sel = jnp.zeros((N_EXPERTS, tm), F32)
    for oh in onehots:
        sel = jnp.where(oh, 1.0, sel)
    lane = lax.broadcasted_iota(I32, (N_EXPERTS, LANES), 1)
    run = carry_s[...]
    cst = jnp.zeros((N_EXPERTS, LANES), F32)
    for s in range(tm // TM_ROUTE):
        cst = jnp.where(lane == s, run, cst)
        run = run + jnp.sum(sel[:, s * TM_ROUTE:(s + 1) * TM_ROUTE], axis=1, keepdims=True)
    cst_ref[0] = cst.astype(I32)
    carry_s[...] = run
    cnt_ref[...] = run.astype(I32)


def _post(attn2, z2, gates2, x2, wa, wb, wo, lng, lnb, wcat, bcat, n2g, wrh, wrl, br):
    t = x2.shape[0]
    tm = TM_POST
    const = lambda *shape: pl.BlockSpec(shape, lambda i: (0,) * len(shape), pipeline_mode=pl.Buffered(1))
    return pl.pallas_call(
        _post_kernel,
        grid=(t // tm,),
        in_specs=[
            pl.BlockSpec((tm, ATTN_WIDTH), lambda i: (i, 0)),
            pl.BlockSpec((tm, SGU_WIDTH), lambda i: (i, 0)),
            pl.BlockSpec((tm, SGU_WIDTH), lambda i: (i, 1)),
            pl.BlockSpec((tm, D_MODEL), lambda i: (i, 0)),
            pl.BlockSpec((tm, D_MODEL), lambda i: (i, 1)),
            pl.BlockSpec((tm, D_MODEL), lambda i: (i, 0)),
            const(ATTN_WIDTH, D_MODEL),
            const(SGU_WIDTH, D_MODEL),
            const(D_MODEL, D_MODEL),
            const(1, SGU_WIDTH),
            const(1, SGU_WIDTH),
            const(SGU_WIDTH // LANES, SGU_CHUNK, 2 * SGU_CHUNK),
            const(SGU_WIDTH // LANES, SGU_CHUNK, LANES),
            const(1, D_MODEL),
            const(N_EXPERTS, D_MODEL),
            const(N_EXPERTS, D_MODEL),
            const(N_EXPERTS, 1),
        ],
        out_specs=[
            pl.BlockSpec((tm, D_MODEL), lambda i: (i, 0)),
            pl.BlockSpec((tm, D_MODEL), lambda i: (i, 0)),
            pl.BlockSpec((TOP_K, tm), lambda i: (0, i)),
            pl.BlockSpec((TOP_K, tm), lambda i: (0, i)),
            pl.BlockSpec((1, N_EXPERTS, LANES), lambda i: (i, 0, 0)),
            pl.BlockSpec((N_EXPERTS, LANES), lambda i: (0, 0)),
        ],
        out_shape=[
            jax.ShapeDtypeStruct((t, D_MODEL), F32),
            jax.ShapeDtypeStruct((t, D_MODEL), BF16),
            jax.ShapeDtypeStruct((TOP_K, t), I32),
            jax.ShapeDtypeStruct((TOP_K, t), F32),
            jax.ShapeDtypeStruct((t // tm, N_EXPERTS, LANES), I32),
            jax.ShapeDtypeStruct((N_EXPERTS, LANES), I32),
        ],
        scratch_shapes=[
            pltpu.VMEM((tm, SGU_WIDTH), BF16),
            pltpu.VMEM((N_EXPERTS, LANES), F32),
        ],
        compiler_params=_params(48, ("arbitrary",)),
        name="post",
    )(attn2, z2, z2, gates2, gates2, x2, wa, wb, wo, lng, lnb, wcat, bcat, n2g, wrh, wrl, br)


def _route_rows(e_ref, tm):
    e = e_ref[...]
    eio = lax.broadcasted_iota(I32, (N_EXPERTS, tm), 0)
    onehots = [eio == e[k:k + 1, :] for k in range(TOP_K)]
    sel = jnp.zeros((N_EXPERTS, tm), F32)
    for oh in onehots:
        sel = jnp.where(oh, 1.0, sel)
    ti = lax.broadcasted_iota(I32, (tm, tm), 0)
    tj = lax.broadcasted_iota(I32, (tm, tm), 1)
    before = jnp.where(ti < tj, 1.0, 0.0).astype(BF16)
    slot = _dot(sel.astype(BF16), before)
    tiles = jnp.floor((jnp.sum(sel, axis=1, keepdims=True) + (SUBLANES - 1)) * (1.0 / SUBLANES))
    xi = lax.broadcasted_iota(I32, (N_EXPERTS, N_EXPERTS), 0)
    xj = lax.broadcasted_iota(I32, (N_EXPERTS, N_EXPERTS), 1)
    lower = jnp.where(xj < xi, 1.0, 0.0).astype(BF16)
    off = _dot(lower, jnp.broadcast_to(tiles, (N_EXPERTS, LANES)).astype(BF16))[:, 0:1] * float(SUBLANES)
    row = off + slot
    return [jnp.sum(jnp.where(oh, row, 0.0), axis=0, keepdims=True) for oh in onehots]


def _pick(rows, values, default):
    tm = rows[0].shape[1]
    rio = lax.broadcasted_iota(I32, (ROUTE_ROWS, tm), 0).astype(F32)
    out = default
    for r, v in zip(rows, values):
        out = jnp.where(rio == r, v, out)
    return out


def _wait_rows(buf, hbm_ref, sem, n):
    n = pl.multiple_of(n, SUBLANES)

    @pl.when(n > 0)
    def _():
        pltpu.make_async_copy(buf.at[pl.ds(0, n), :], hbm_ref.at[pl.ds(0, n), :], sem).wait()


def _dispatch_kernel(start_ref, off_ref, cnt_ref, tot_ref, za_ref, zb_ref, hf_ref, e_ref, g_ref, xs_ref,
                     obuf, zbuf, sem, semz):
    tm = hf_ref.shape[0]
    i = pl.program_id(0)
    last = pl.num_programs(0) - 1
    cur = lax.rem(i, 2)
    rows = _route_rows(e_ref, tm)
    g = g_ref[...]
    pt = _pick(rows, [1.0] * TOP_K, 0.0).astype(BF16)
    gate = jnp.sum(_pick(rows, [g[k:k + 1, :] for k in range(TOP_K)], 0.0), axis=1, keepdims=True)

    @pl.when(i >= 2)
    def _():
        _wait_rows(obuf.at[cur], xs_ref, sem.at[cur], tot_ref[jnp.maximum(i - 2, 0)])

    obuf[cur, :, 0:D_MODEL] = _dot(pt, hf_ref[...])
    obuf[cur, :, D_MODEL:XS_WIDTH] = jnp.broadcast_to(gate, (ROUTE_ROWS, LANES))

    for x in range(N_EXPERTS):
        n = pl.multiple_of(cnt_ref[i * N_EXPERTS + x], SUBLANES)

        @pl.when(n > 0)
        def _(x=x, n=n):
            src = pl.multiple_of(off_ref[i * N_EXPERTS + x], SUBLANES)
            dst = pl.multiple_of(start_ref[i * N_EXPERTS + x], SUBLANES)
            pltpu.make_async_copy(obuf.at[cur, pl.ds(src, n), :], xs_ref.at[pl.ds(dst, n), :], sem.at[cur]).start()

    @pl.when(i == last)
    def _():
        @pl.when(i >= 1)
        def _():
            _wait_rows(obuf.at[1 - cur], xs_ref, sem.at[1 - cur], tot_ref[jnp.maximum(i - 1, 0)])

        _wait_rows(obuf.at[cur], xs_ref, sem.at[cur], tot_ref[i])
        zbuf[...] = jnp.zeros_like(zbuf)

        def zero_region(z, act):
            n = zb_ref[z] - za_ref[z]
            big = n // EXPERT_TILE
            rest = pl.multiple_of(n - big * EXPERT_TILE, SUBLANES)

            def piece(row, nrows):
                return pltpu.make_async_copy(zbuf.at[pl.ds(0, nrows), :],
                                             xs_ref.at[pl.ds(pl.multiple_of(row, SUBLANES), nrows), :], semz)

            lax.fori_loop(0, big, lambda m, c: (act(piece(za_ref[z] + m * EXPERT_TILE, EXPERT_TILE)), c)[1], 0)

            @pl.when(rest > 0)
            def _():
                act(piece(za_ref[z] + big * EXPERT_TILE, rest))

        for z in range(N_EXPERTS + 1):
            zero_region(z, lambda cp: cp.start())
        for z in range(N_EXPERTS + 1):
            zero_region(z, lambda cp: cp.wait())


def _dispatch(start, off, cnt, tot, za, zb, hf, e4, g4, cap):
    t = hf.shape[0]
    tm = TM_ROUTE
    tok = lambda i, *_: (0, i)
    return pl.pallas_call(
        _dispatch_kernel,
        grid_spec=pltpu.PrefetchScalarGridSpec(
            num_scalar_prefetch=6,
            grid=(t // tm,),
            in_specs=[
                pl.BlockSpec((tm, D_MODEL), lambda i, *_: (i, 0)),
                pl.BlockSpec((TOP_K, tm), tok),
                pl.BlockSpec((TOP_K, tm), tok),
            ],
            out_specs=pl.BlockSpec(memory_space=pl.ANY),
            scratch_shapes=[
                pltpu.VMEM((2, ROUTE_ROWS, XS_WIDTH), F32),
                pltpu.VMEM((EXPERT_TILE, XS_WIDTH), F32),
                pltpu.SemaphoreType.DMA((2,)),
                pltpu.SemaphoreType.DMA(()),
            ],
        ),
        out_shape=jax.ShapeDtypeStruct((cap, XS_WIDTH), F32),
        compiler_params=_params(48, ("arbitrary",)),
        name="dispatch",
    )(start, off, cnt, tot, za, zb, hf, e4, g4)


def _expert_kernel(te_ref, nu_ref, xs_ref, wgu_ref, bgu_ref, wdn_ref, bdn_ref, ys_ref, wgu_s, wdn_s):
    i = pl.program_id(0)

    @pl.when(i < nu_ref[0])
    def _():
        @pl.when((i == 0) | (te_ref[i] != te_ref[jnp.maximum(i - 1, 0)]))
        def _():
            wgu_s[...] = wgu_ref[0].astype(BF16)
            wdn_s[...] = wdn_ref[0].astype(BF16)

        x = xs_ref[:, 0:D_MODEL].astype(BF16)
        route_gate = xs_ref[:, D_MODEL:D_MODEL + 1]
        gu = _dot(x, wgu_s[...]) + bgu_ref[0]
        gate = jnp.minimum(gu[:, :D_EXPERT], SWIGLU_LIMIT)
        up = jnp.clip(gu[:, D_EXPERT:], -SWIGLU_LIMIT, SWIGLU_LIMIT)
        act = (up + 1.0) * (gate * jax.nn.sigmoid(SWIGLU_ALPHA * gate))
        ys_ref[...] = (_dot(act.astype(BF16), wdn_s[...]) + bdn_ref[0]) * route_gate

    @pl.when(i >= nu_ref[0])
    def _():
        ys_ref[...] = jnp.zeros_like(ys_ref)


def _experts(tile_expert, n_used, xs, wgu, bgu, wdn, bdn):
    cap = xs.shape[0]
    n_tiles = cap // EXPERT_TILE
    row = lambda i, te, nu: (jnp.minimum(i, nu[0] - 1), 0)
    by_expert = lambda i, te, nu: (te[i], 0, 0)
    return pl.pallas_call(
        _expert_kernel,
        grid_spec=pltpu.PrefetchScalarGridSpec(
            num_scalar_prefetch=2,
            grid=(n_tiles,),
            in_specs=[
                pl.BlockSpec((EXPERT_TILE, XS_WIDTH), row),
                pl.BlockSpec((1, D_MODEL, 2 * D_EXPERT), by_expert),
                pl.BlockSpec((1, 1, 2 * D_EXPERT), by_expert),
                pl.BlockSpec((1, D_EXPERT, D_MODEL), by_expert),
                pl.BlockSpec((1, 1, D_MODEL), by_expert),
            ],
            out_specs=pl.BlockSpec((EXPERT_TILE, D_MODEL), lambda i, te, nu: (i, 0)),
            scratch_shapes=[
                pltpu.VMEM((D_MODEL, 2 * D_EXPERT), BF16),
                pltpu.VMEM((D_EXPERT, D_MODEL), BF16),
            ],
        ),
        out_shape=jax.ShapeDtypeStruct((cap, D_MODEL), F32),
        compiler_params=_params(56, ("arbitrary",)),
        name="experts",
    )(tile_expert, n_used, xs, wgu, bgu, wdn, bdn)


def _combine_kernel(start_ref, off_ref, cnt_ref, tot_ref, x1_ref, e_ref, ys_ref, o_ref, sbuf, sem):
    tm = x1_ref.shape[0]
    i = pl.program_id(0)
    last = pl.num_programs(0) - 1
    cur = lax.rem(i, 2)

    def fetch(step, b):
        for x in range(N_EXPERTS):
            n = pl.multiple_of(cnt_ref[step * N_EXPERTS + x], SUBLANES)

            @pl.when(n > 0)
            def _(x=x, n=n):
                src = pl.multiple_of(start_ref[step * N_EXPERTS + x], SUBLANES)
                dst = pl.multiple_of(off_ref[step * N_EXPERTS + x], SUBLANES)
                pltpu.make_async_copy(ys_ref.at[pl.ds(src, n), :], sbuf.at[b, pl.ds(dst, n), :], sem.at[b]).start()

    @pl.when(i == 0)
    def _():
        sbuf[...] = jnp.zeros_like(sbuf)
        fetch(i, cur)

    @pl.when(i < last)
    def _():
        fetch(i + 1, 1 - cur)

    rows = _route_rows(e_ref, tm)
    pt = _pick(rows, [1.0] * TOP_K, 0.0).astype(BF16)
    _wait_rows(sbuf.at[cur], ys_ref, sem.at[cur], tot_ref[i])
    picked = lax.dot_general(pt, sbuf[cur].astype(BF16), _TN, preferred_element_type=F32)
    o_ref[...] = x1_ref[...] + picked


def _combine(start, off, cnt, tot, x1, e4, ys):
    t = x1.shape[0]
    tm = TM_ROUTE
    return pl.pallas_call(
        _combine_kernel,
        grid_spec=pltpu.PrefetchScalarGridSpec(
            num_scalar_prefetch=4,
            grid=(t // tm,),
            in_specs=[
                pl.BlockSpec((tm, D_MODEL), lambda i, *_: (i, 0)),
                pl.BlockSpec((TOP_K, tm), lambda i, *_: (0, i)),
                pl.BlockSpec(memory_space=pl.ANY),
            ],
            out_specs=pl.BlockSpec((tm, D_MODEL), lambda i, *_: (i, 0)),
            scratch_shapes=[
                pltpu.VMEM((2, ROUTE_ROWS, D_MODEL), F32),
                pltpu.SemaphoreType.DMA((2,)),
            ],
        ),
        out_shape=jax.ShapeDtypeStruct((t, D_MODEL), F32),
        compiler_params=_params(48, ("arbitrary",)),
        name="combine",
    )(start, off, cnt, tot, x1, e4, ys)


def _layer(x, positions, norm1_g, w_in, q_norm_g, k_norm_g, sgu_ln_g, sgu_ln_b, w_spatial, b_spatial,
           w_branch_a, w_branch_b, w_out, norm2_g, w_router, b_router, w_gate_up, b_gate_up, w_down, b_down):
    b, seq, _ = x.shape
    t = b * seq
    x2 = x.reshape(t, D_MODEL)

    inv_freq = ROPE_THETA ** (-jnp.arange(ROT_HALF, dtype=F32) / ROT_HALF)
    invf = jnp.tile(inv_freq, LANES // ROT_HALF).reshape(1, LANES)
    dim_of_lane = [(l // (2 * ROT_HALF)) * ROT_HALF + l % ROT_HALF for l in range(LANES)]
    qg = q_norm_g[jnp.array(dim_of_lane)].reshape(1, LANES)
    kg = k_norm_g[jnp.array(dim_of_lane)].reshape(1, LANES)
    w_bf = w_in.astype(BF16)
    qk = w_bf[:, :2 * ATTN_WIDTH].reshape(D_MODEL, 2 * ATTN_WIDTH // LANES, 2, 2, ROT_HALF)
    qk = qk.transpose(0, 1, 3, 2, 4).reshape(D_MODEL, 2 * ATTN_WIDTH)
    w_in_p = jnp.concatenate([qk, w_bf[:, 2 * ATTN_WIDTH:]], axis=1)
    n_pairs = SGU_WIDTH // LANES
    wcat = w_spatial.reshape(n_pairs, 2, SGU_CHUNK, SGU_CHUNK).transpose(0, 2, 1, 3).reshape(n_pairs, SGU_CHUNK, 2 * SGU_CHUNK)
    bcat = jnp.repeat(b_spatial.reshape(n_pairs, 2, SGU_CHUNK).transpose(0, 2, 1), SGU_GROUP_DIM, axis=2)
    wr_t = w_router.T
    wr_hi = wr_t.astype(BF16)
    wr_lo = (wr_t - wr_hi.astype(F32)).astype(BF16)

    *qkv_groups, z, gates = _inproj(x2, norm1_g.reshape(1, D_MODEL), w_in_p, positions.reshape(-1, LANES // ROT_HALF), invf, qg, kg, b, seq)
    attn = _attention(qkv_groups)
    x1, hf, e4, g4, cst, counts = _post(
        attn.reshape(t, ATTN_WIDTH), z, gates, x2,
        w_branch_a.astype(BF16), w_branch_b.astype(BF16), w_out.astype(BF16),
        sgu_ln_g.reshape(1, SGU_WIDTH), sgu_ln_b.reshape(1, SGU_WIDTH), wcat.astype(BF16), bcat,
        norm2_g.reshape(1, D_MODEL), wr_hi, wr_lo, b_router.reshape(N_EXPERTS, 1))

    steps_per_tile = TM_POST // TM_ROUTE
    n_steps = t // TM_ROUTE
    before = cst[:, :, :steps_per_tile].transpose(0, 2, 1).reshape(n_steps, N_EXPERTS)
    total = counts[:, 0]
    cnt = jnp.concatenate([before[1:], total[None, :]], axis=0) - before
    cnt_al = (cnt + SUBLANES - 1) // SUBLANES * SUBLANES
    off = (jnp.cumsum(cnt_al, axis=1) - cnt_al).astype(I32).reshape(-1)
    tot = jnp.sum(cnt_al, axis=1).astype(I32)
    used = jnp.sum(cnt_al, axis=0)
    region = (used + EXPERT_TILE - 1) // EXPERT_TILE * EXPERT_TILE
    rend = jnp.cumsum(region)
    rstart = rend - region
    start = (rstart[None, :] + jnp.cumsum(cnt_al, axis=0) - cnt_al).astype(I32).reshape(-1)
    bound = t * TOP_K + n_steps * N_EXPERTS * (SUBLANES - 1) + N_EXPERTS * (EXPERT_TILE - 1)
    cap = (bound + EXPERT_TILE - 1) // EXPERT_TILE * EXPERT_TILE
    n_tiles = cap // EXPERT_TILE
    tile_expert = jnp.sum(jnp.arange(n_tiles, dtype=I32)[:, None] * EXPERT_TILE >= rend[None, :], axis=1)
    tile_expert = jnp.minimum(tile_expert, N_EXPERTS - 1).astype(I32)
    n_used = (rend[-1:] // EXPERT_TILE).astype(I32)
    za = jnp.concatenate([rstart + used, rend[-1:]]).astype(I32)
    zb = jnp.concatenate([rend, jnp.full((1,), cap, rend.dtype)]).astype(I32)
    cnt_al = cnt_al.astype(I32).reshape(-1)

    xs = _dispatch(start, off, cnt_al, tot, za, zb, hf, e4, g4, cap)
    ys = _experts(tile_expert, n_used, xs,
                  w_gate_up, b_gate_up.reshape(N_EXPERTS, 1, 2 * D_EXPERT),
                  w_down, b_down.reshape(N_EXPERTS, 1, D_MODEL))
    out = _combine(start, off, cnt_al, tot, x1, e4, ys)
    return out.reshape(b, seq, D_MODEL)


def kernel(x, positions, norm1_g, w_in, q_norm_g, k_norm_g, sgu_ln_g, sgu_ln_b, w_spatial, b_spatial, w_branch_a, w_branch_b, w_out, norm2_g, w_router, b_router, w_gate_up, b_gate_up, w_down, b_down):
    for layer in range(norm1_g.shape[0]):
        x = _layer(x, positions, norm1_g[layer], w_in[layer], q_norm_g[layer], k_norm_g[layer],
                   sgu_ln_g[layer], sgu_ln_b[layer], w_spatial[layer], b_spatial[layer],
                   w_branch_a[layer], w_branch_b[layer], w_out[layer], norm2_g[layer],
                   w_router[layer], b_router[layer], w_gate_up[layer], b_gate_up[layer],
                   w_down[layer], b_down[layer])
    return x
```

```python
import functools

import jax
import jax.numpy as jnp
from jax import lax
from jax.experimental import pallas as pl
from jax.experimental.pallas import tpu as pltpu

F32 = jnp.float32
BF16 = jnp.bfloat16
I32 = jnp.int32

D_MODEL = 1024
HEAD_DIM = 64
ROT_HALF = HEAD_DIM // 2
LOG2E = 1.4426950408889634
LN2 = 0.6931471805599453
Q_SCALE = HEAD_DIM ** -0.5 * LOG2E
ATTN_GROUPS = ((128, 1), (512, 4), (2048, 16))
N_GROUPS = len(ATTN_GROUPS)
GROUP_WIDTH = 256
ATTN_WIDTH = N_GROUPS * GROUP_WIDTH
BLK = 128
ROPE_THETA = 10000.0
SGU_CHUNK = 128
SGU_GROUP_DIM = 64
SGU_WIDTH = 768
GATE_WIDTH = 2 * D_MODEL
N_EXPERTS = 32
TOP_K = 4
D_EXPERT = D_MODEL
SWIGLU_LIMIT = 7.0
SWIGLU_ALPHA = 1.702
EPS = 1e-6
NEG_INF = -1e30
LANES = 128

TM_INPROJ = 512
TM_POST = 512
TM_ROUTE = 256
SUBLANES = 8
ROUTE_ROWS = TM_ROUTE * TOP_K + N_EXPERTS * SUBLANES
EXPERT_TILE = 512
XS_WIDTH = D_MODEL + LANES
ROW_CHUNK = 256

_NT = (((1,), (1,)), ((), ()))
_TN = (((0,), (0,)), ((), ()))


def _dot(a, b):
    return jnp.dot(a, b, preferred_element_type=F32)


def _dot_nt(a, b):
    return lax.dot_general(a, b, _NT, preferred_element_type=F32)


def _params(vmem_mb, sem):
    return pltpu.CompilerParams(dimension_semantics=sem, vmem_limit_bytes=vmem_mb * 1024 * 1024)


def _split3(x):
    a = x.astype(BF16)
    r = x - a.astype(F32)
    b = r.astype(BF16)
    return a, b, (r - b.astype(F32)).astype(BF16)


def _inproj_kernel(x_ref, g_ref, w_ref, pos_ref, invf_ref, qg_ref, kg_ref,
                   o0_ref, o1_ref, o2_ref, oz_ref, og_ref, raw_a, raw_b, stage_s, cos_s, sin_s):
    step = pl.program_id(0)

    @pl.when(step == 0)
    def _():
        raw_b[...] = jnp.zeros_like(raw_b)

    args = (x_ref, g_ref, w_ref, pos_ref, invf_ref, qg_ref, kg_ref, o0_ref, o1_ref, o2_ref, oz_ref, og_ref)

    @pl.when(lax.rem(step, 2) == 0)
    def _():
        _inproj_step(*args, raw_a, raw_b, stage_s, cos_s, sin_s)

    @pl.when(lax.rem(step, 2) == 1)
    def _():
        _inproj_step(*args, raw_b, raw_a, stage_s, cos_s, sin_s)


def _inproj_step(x_ref, g_ref, w_ref, pos_ref, invf_ref, qg_ref, kg_ref,
                 o0_ref, o1_ref, o2_ref, oz_ref, og_ref, raw_w, raw_r, stage_s, cos_s, sin_s):
    tm = x_ref.shape[0]
    x = x_ref[...]
    y = x * lax.rsqrt(jnp.mean(x * x, axis=-1, keepdims=True) + EPS)
    h = (y * g_ref[...]).astype(BF16)

    def project(lo):
        p = _dot(h, w_ref[:, lo:lo + GROUP_WIDTH])
        zc = 3 * ATTN_WIDTH
        gc = zc + 2 * SGU_WIDTH
        if lo < zc:
            raw_w[:, lo:lo + GROUP_WIDTH] = p
        elif lo < gc:
            oz_ref[:, lo - zc:lo - zc + GROUP_WIDTH] = p.astype(BF16)
        else:
            og_ref[:, lo - gc:lo - gc + GROUP_WIDTH] = p.astype(BF16)

    def rotary_tables():
        per_row = LANES // ROT_HALF
        lane = lax.broadcasted_iota(I32, (tm // per_row, LANES), 1)
        p4 = pos_ref[...].astype(F32)
        posd = p4[:, per_row - 1:per_row]
        for j in range(per_row - 2, -1, -1):
            posd = jnp.where(lane < (j + 1) * ROT_HALF, p4[:, j:j + 1], posd)
        ang = posd * invf_ref[...]
        cparts = _split3(jnp.cos(ang))
        sparts = _split3(jnp.sin(ang))
        src = lax.broadcasted_iota(I32, (LANES, LANES), 0)
        dst = lax.broadcasted_iota(I32, (LANES, LANES), 1)
        for j in range(per_row):
            pick = src == j * ROT_HALF + dst % ROT_HALF
            spread = jnp.where(pick, 1.0, 0.0).astype(BF16)
            spread_neg = jnp.where(pick, jnp.where(dst < 2 * ROT_HALF, -1.0, 1.0), 0.0).astype(BF16)
            cos_s[pl.ds(j, tm // per_row, stride=per_row), :] = sum(_dot(p, spread) for p in cparts)
            sin_s[pl.ds(j, tm // per_row, stride=per_row), :] = sum(_dot(p, spread_neg) for p in sparts)

    li = (lax.broadcasted_iota(I32, (LANES, LANES), 0) // ROT_HALF) % 2
    lj = (lax.broadcasted_iota(I32, (LANES, LANES), 1) // ROT_HALF) % 2
    head_sum = jnp.where(li == lj, 1.0, 0.0).astype(BF16)
    outs = (o0_ref, o1_ref, o2_ref)

    def finish(gi, d, ti, mode, gain_ref, c, stage):
        col = ti * ATTN_WIDTH + gi * GROUP_WIDTH
        t = raw_r[:, col + c * LANES:col + (c + 1) * LANES]
        if mode != "v":
            ss = t * t
            hi = ss.astype(BF16)
            lo = (ss - hi.astype(F32)).astype(BF16)
            ms = (_dot(hi, head_sum) + _dot(lo, head_sum)) * (1.0 / HEAD_DIM)
            t = t * lax.rsqrt(ms + EPS) * gain_ref[...]
            t = t * cos_s[...] + pltpu.roll(t, 2 * ROT_HALF, 1) * sin_s[...]
            if mode == "q":
                t = t * Q_SCALE
        lanes = slice(ti * GROUP_WIDTH + c * LANES, ti * GROUP_WIDTH + (c + 1) * LANES)
        if d == 1:
            outs[gi][0, 0, :, lanes] = t.astype(BF16)
        else:
            stage_s[stage] = t
            for r in range(d):
                outs[gi][0, r, :, lanes] = stage_s[stage, pl.ds(r, tm // d, stride=d), :].astype(BF16)

    pieces = [rotary_tables]
    stage = 0
    for gi, (_, d) in enumerate(ATTN_GROUPS):
        for ti, (mode, gain_ref) in enumerate((("q", qg_ref), ("k", kg_ref), ("v", None))):
            for c in range(GROUP_WIDTH // LANES):
                pieces.append(functools.partial(finish, gi, d, ti, mode, gain_ref, c, stage))
                stage += d > 1

    chunks = list(range(0, w_ref.shape[1], GROUP_WIDTH))
    for k in range(max(len(chunks), len(pieces))):
        if k < len(chunks):
            project(chunks[k])
        if k < len(pieces):
            pieces[k]()


def _inproj(x2, g, w, pos2, invf, qg, kg, b, seq):
    t = x2.shape[0]
    n = w.shape[1]
    tm = TM_INPROJ
    nj = seq // tm
    n_tiles = b * nj
    n_stage = sum(1 for _, d in ATTN_GROUPS if d > 1) * 3 * (GROUP_WIDTH // LANES)
    proj_row = lambda s: (jnp.minimum(s, n_tiles - 1), 0)
    done_row = lambda s: (jnp.maximum(s - 1, 0), 0)
    done_blk = lambda s: (jnp.maximum(s - 1, 0) // nj, 0, jnp.maximum(s - 1, 0) % nj, 0)
    const = lambda s: (0, 0)
    per_row = LANES // ROT_HALF
    return pl.pallas_call(
        _inproj_kernel,
        grid=(n_tiles + 1,),
        in_specs=[
            pl.BlockSpec((tm, D_MODEL), proj_row),
            pl.BlockSpec((1, D_MODEL), const),
            pl.BlockSpec((D_MODEL, n), const, pipeline_mode=pl.Buffered(1)),
            pl.BlockSpec((tm // per_row, per_row), done_row),
            pl.BlockSpec((1, LANES), const),
            pl.BlockSpec((1, LANES), const),
            pl.BlockSpec((1, LANES), const),
        ],
        out_specs=[pl.BlockSpec((1, d, tm // d, ATTN_WIDTH), done_blk) for _, d in ATTN_GROUPS] + [
            pl.BlockSpec((tm, 2 * SGU_WIDTH), proj_row),
            pl.BlockSpec((tm, GATE_WIDTH), proj_row),
        ],
        out_shape=[jax.ShapeDtypeStruct((b, d, seq // d, ATTN_WIDTH), BF16) for _, d in ATTN_GROUPS] + [
            jax.ShapeDtypeStruct((t, 2 * SGU_WIDTH), BF16),
            jax.ShapeDtypeStruct((t, GATE_WIDTH), BF16),
        ],
        scratch_shapes=[
            pltpu.VMEM((tm, 3 * ATTN_WIDTH), F32),
            pltpu.VMEM((tm, 3 * ATTN_WIDTH), F32),
            pltpu.VMEM((n_stage, tm, LANES), F32),
            pltpu.VMEM((tm, LANES), F32),
            pltpu.VMEM((tm, LANES), F32),
        ],
        compiler_params=_params(56, ("arbitrary",)),
        name="inproj",
    )(x2, g, w, pos2, invf, qg, kg)


def _attn_blocks(qkv_ref, ores_s, lres_s, seq, d):
    sub = seq // d
    nb = sub // BLK
    win = 2 * BLK if nb > 1 else BLK
    log2d = d.bit_length() - 1
    lane = lax.broadcasted_iota(I32, (BLK, GROUP_WIDTH), 1)
    q_head = 2 * (lane // LANES) + (lane // ROT_HALF) % 2
    qi = lax.broadcasted_iota(I32, (BLK, win), 0)
    kj = lax.broadcasted_iota(I32, (BLK, win), 1)
    rel = qi - kj

    def body(idx, carry):
        r = idx & (d - 1)
        n = idx >> log2d
        row0 = pl.multiple_of(r * sub + n * BLK, BLK)
        if nb > 1:
            kn = jnp.maximum(n - 1, 0)
            k0 = pl.multiple_of(r * sub + kn * BLK, BLK)
            dist = rel + (n - kn) * BLK
        else:
            k0 = row0
            dist = rel
        valid = lax.bitcast_convert_type(dist, jnp.uint32) <= jnp.uint32(BLK)
        q = qkv_ref[0, pl.ds(row0, BLK), 0:GROUP_WIDTH]
        kw = qkv_ref[0, pl.ds(k0, win), GROUP_WIDTH:2 * GROUP_WIDTH]
        vw = qkv_ref[0, pl.ds(k0, win), 2 * GROUP_WIDTH:3 * GROUP_WIDTH]
        zero = jnp.zeros_like(q)
        qm = jnp.concatenate([jnp.where(q_head == h, q, zero) for h in range(4)], axis=0)
        s = _dot_nt(qm, kw)
        ps, ms, ls = [], [], []
        for h in range(4):
            sh = jnp.where(valid, s[h * BLK:(h + 1) * BLK], NEG_INF)
            m = jnp.max(sh, axis=-1, keepdims=True)
            p = jnp.exp2(sh - m)
            ls.append(jnp.sum(p, axis=-1, keepdims=True))
            ms.append(m)
            ps.append(p.astype(BF16))
        pv = _dot(jnp.concatenate(ps, axis=0), vw)
        for h in range(4):
            lanes = slice(h * HEAD_DIM, (h + 1) * HEAD_DIM)
            ores_s[pl.ds(row0, BLK), lanes] = pv[h * BLK:(h + 1) * BLK, lanes] * (1.0 / ls[h])
            lres_s[pl.ds(row0, BLK), lanes] = jnp.broadcast_to((ms[h] + jnp.log2(ls[h])) * LN2, (BLK, HEAD_DIM))
        return carry

    lax.fori_loop(0, seq // BLK, body, 0, unroll=8)


def _attn_kernel(g0_ref, g1_ref, g2_ref, o_ref, ores_s, lres_s, onat_s, lnat_s, *, seq):
    for gi, ((_, d), qkv_ref) in enumerate(zip(ATTN_GROUPS, (g0_ref, g1_ref, g2_ref))):
        _attn_blocks(qkv_ref, ores_s, lres_s, seq, d)
        sub = seq // d
        for r in range(d):
            for c in range(GROUP_WIDTH // LANES):
                t = gi * (GROUP_WIDTH // LANES) + c
                onat_s[t, pl.ds(r, sub, stride=d), :] = ores_s[r * sub:(r + 1) * sub, c * LANES:(c + 1) * LANES]
                lnat_s[t, pl.ds(r, sub, stride=d), :] = lres_s[r * sub:(r + 1) * sub, c * LANES:(c + 1) * LANES]

    tiles = GROUP_WIDTH // LANES

    def body(i, carry):
        r0 = pl.multiple_of(i * ROW_CHUNK, ROW_CHUNK)
        for c in range(tiles):
            ls = [lnat_s[gi * tiles + c, pl.ds(r0, ROW_CHUNK), :] for gi in range(N_GROUPS)]
            m = jnp.maximum(jnp.maximum(ls[0], ls[1]), ls[2])
            es = [jnp.exp(l - m) for l in ls]
            inv = 1.0 / (es[0] + es[1] + es[2])
            for gi in range(N_GROUPS):
                t = gi * tiles + c
                o_ref[0, pl.ds(r0, ROW_CHUNK), t * LANES:(t + 1) * LANES] = (
                    onat_s[t, pl.ds(r0, ROW_CHUNK), :] * (es[gi] * inv)).astype(BF16)
        return carry

    lax.fori_loop(0, seq // ROW_CHUNK, body, 0, unroll=2)


def _attention(qkv_groups):
    b = qkv_groups[0].shape[0]
    seq = qkv_groups[0].shape[1] * qkv_groups[0].shape[2]
    tiles = GROUP_WIDTH // LANES
    blk = pl.BlockSpec((1, seq, ATTN_WIDTH), lambda i: (i, 0, 0))
    return pl.pallas_call(
        functools.partial(_attn_kernel, seq=seq),
        grid=(b,),
        in_specs=[blk] * N_GROUPS,
        out_specs=blk,
        out_shape=jax.ShapeDtypeStruct((b, seq, ATTN_WIDTH), BF16),
        scratch_shapes=[
            pltpu.VMEM((seq, GROUP_WIDTH), F32),
            pltpu.VMEM((seq, GROUP_WIDTH), F32),
            pltpu.VMEM((N_GROUPS * tiles, seq, LANES), F32),
            pltpu.VMEM((N_GROUPS * tiles, seq, LANES), F32),
        ],
        compiler_params=_params(48, ("arbitrary",)),
        name="attention",
    )(*[a.reshape(b, seq, ATTN_WIDTH) for a in qkv_groups])


def _gelu(x):
    return 0.5 * x * (1.0 + lax.erf(x * 0.7071067811865476))


def _post_kernel(attn_ref, u_ref, vz_ref, ga_ref, gb_ref, x_ref, wa_ref, wb_ref, wo_ref,
                 lng_ref, lnb_ref, wcat_ref, bcat_ref, n2g_ref, wrh_ref, wrl_ref, br_ref,
                 x1_ref, hf_ref, e_ref, g_ref, cst_ref, cnt_ref, sgu_s, carry_s, merged_a, merged_b):
    step = pl.program_id(0)

    @pl.when(step == 0)
    def _():
        carry_s[...] = jnp.zeros_like(carry_s)
        merged_b[...] = jnp.zeros_like(merged_b)

    args = (attn_ref, u_ref, vz_ref, ga_ref, gb_ref, x_ref, wa_ref, wb_ref, wo_ref, lng_ref, lnb_ref, wcat_ref,
            bcat_ref, n2g_ref, wrh_ref, wrl_ref, br_ref, x1_ref, hf_ref, e_ref, g_ref, cst_ref, cnt_ref, sgu_s, carry_s)

    @pl.when(lax.rem(step, 2) == 0)
    def _():
        _post_step(*args, merged_a, merged_b)

    @pl.when(lax.rem(step, 2) == 1)
    def _():
        _post_step(*args, merged_b, merged_a)


def _post_step(attn_ref, u_ref, vz_ref, ga_ref, gb_ref, x_ref, wa_ref, wb_ref, wo_ref,
               lng_ref, lnb_ref, wcat_ref, bcat_ref, n2g_ref, wrh_ref, wrl_ref, br_ref,
               x1_ref, hf_ref, e_ref, g_ref, cst_ref, cnt_ref, sgu_s, carry_s, merged_w, merged_r):
    tm = x_ref.shape[0]
    counted = jnp.where(pl.program_id(0) > 0, 1.0, 0.0)

    lane = lax.broadcasted_iota(I32, (SGU_CHUNK, LANES), 1)
    low = lane < SGU_GROUP_DIM
    trow = lax.broadcasted_iota(I32, (SGU_CHUNK, 2 * SGU_CHUNK), 0)
    tcol = lax.broadcasted_iota(I32, (SGU_CHUNK, 2 * SGU_CHUNK), 1) % SGU_CHUNK
    causal = tcol <= trow

    def chunk(c):
        r0 = c * SGU_CHUNK
        u = _gelu(u_ref[pl.ds(r0, SGU_CHUNK), :].astype(F32))
        v = _gelu(vz_ref[pl.ds(r0, SGU_CHUNK), :].astype(F32))
        mu = jnp.mean(v, axis=-1, keepdims=True)
        vc = v - mu
        vn = vc * lax.rsqrt(jnp.mean(vc * vc, axis=-1, keepdims=True) + EPS)
        vn = (vn * lng_ref[...] + lnb_ref[...]).astype(BF16)
        zero = jnp.zeros((SGU_CHUNK, LANES), BF16)
        for j in range(SGU_WIDTH // LANES):
            vt = vn[:, j * LANES:(j + 1) * LANES]
            rhs = jnp.concatenate([jnp.where(low, vt, zero), jnp.where(low, zero, vt)], axis=0)
            wj = jnp.where(causal, wcat_ref[j], jnp.zeros((), BF16))
            mixed = _dot(wj, rhs) + bcat_ref[j]
            sgu_s[pl.ds(r0, SGU_CHUNK), j * LANES:(j + 1) * LANES] = (u[:, j * LANES:(j + 1) * LANES] * mixed).astype(BF16)

    eio = lax.broadcasted_iota(I32, (N_EXPERTS, tm), 0)
    state = {}
    top_v, onehots = [], []

    def route_logits(hi, lo):
        state["work"] = (_dot_nt(wrh_ref[...], hi) + _dot_nt(wrl_ref[...], hi) + _dot_nt(wrh_ref[...], lo)
                         + br_ref[...])

    def route_pick(k):
        work = state["work"]
        m = jnp.max(work, axis=0, keepdims=True)
        idx = jnp.min(jnp.where(work == m, eio, N_EXPERTS), axis=0, keepdims=True)
        oh = eio == idx
        state["work"] = jnp.where(oh, -jnp.inf, work)
        top_v.append(m)
        onehots.append(oh)
        e_ref[k:k + 1, :] = idx

    def route_finish():
        ex = [jnp.exp(v - top_v[0]) for v in top_v]
        inv = 1.0 / (ex[0] + ex[1] + ex[2] + ex[3])
        for k in range(TOP_K):
            g_ref[k:k + 1, :] = ex[k] * inv
        sel = jnp.zeros((N_EXPERTS, tm), F32)
        for oh in onehots:
            sel = jnp.where(oh, counted, sel)
        lane_e = lax.broadcasted_iota(I32, (N_EXPERTS, LANES), 1)
        run = carry_s[...]
        cst = jnp.zeros((N_EXPERTS, LANES), F32)
        for s in range(tm // TM_ROUTE):
            cst = jnp.where(lane_e == s, run, cst)
            run = run + jnp.sum(sel[:, s * TM_ROUTE:(s + 1) * TM_ROUTE], axis=1, keepdims=True)
        cst_ref[0] = cst.astype(I32)
        carry_s[...] = run
        cnt_ref[...] = run.astype(I32)

    n_parts = tm // SGU_CHUNK
    width = D_MODEL // n_parts
    x1_parts, y_a_parts = [], []
    for c in range(n_parts):
        cols = slice(c * width, (c + 1) * width)
        chunk(c)
        x1_parts.append(x_ref[:, cols] + _dot(merged_r[...], wo_ref[:, cols]))
        y_a_parts.append(_dot(attn_ref[...], wa_ref[:, cols]))
    x1 = jnp.concatenate(x1_parts, axis=1)
    x1_ref[...] = x1
    hf = x1 * lax.rsqrt(jnp.mean(x1 * x1, axis=-1, keepdims=True) + EPS) * n2g_ref[...]
    hi = hf.astype(BF16)
    hf_ref[...] = hi
    route_logits(hi, (hf - hi.astype(F32)).astype(BF16))
    for c in range(n_parts):
        cols = slice(c * width, (c + 1) * width)
        y_b = _dot(sgu_s[...], wb_ref[:, cols])
        merged_w[:, cols] = (jax.nn.sigmoid(ga_ref[:, cols].astype(F32)) * y_a_parts[c]
                             + jax.nn.sigmoid(gb_ref[:, cols].astype(F32)) * y_b).astype(BF16)
        if c < TOP_K:
            route_pick(c)
    for k in range(n_parts, TOP_K):
        route_pick(k)
    route_finish()


def _post(attn2, z2, gates2, x2, wa, wb, wo, lng, lnb, wcat, bcat, n2g, wrh, wrl, br):
    t = x2.shape[0]
    tm = TM_POST
    const = lambda *shape: pl.BlockSpec(shape, lambda i: (0,) * len(shape), pipeline_mode=pl.Buffered(1))
    n_tiles = t // tm
    cur = lambda i: jnp.minimum(i, n_tiles - 1)
    prev = lambda i: jnp.maximum(i - 1, 0)
    return pl.pallas_call(
        _post_kernel,
        grid=(n_tiles + 1,),
        in_specs=[
            pl.BlockSpec((tm, ATTN_WIDTH), lambda i: (cur(i), 0)),
            pl.BlockSpec((tm, SGU_WIDTH), lambda i: (cur(i), 0)),
            pl.BlockSpec((tm, SGU_WIDTH), lambda i: (cur(i), 1)),
            pl.BlockSpec((tm, D_MODEL), lambda i: (cur(i), 0)),
            pl.BlockSpec((tm, D_MODEL), lambda i: (cur(i), 1)),
            pl.BlockSpec((tm, D_MODEL), lambda i: (prev(i), 0)),
            const(ATTN_WIDTH, D_MODEL),
            const(SGU_WIDTH, D_MODEL),
            const(D_MODEL, D_MODEL),
            const(1, SGU_WIDTH),
            const(1, SGU_WIDTH),
            const(SGU_WIDTH // LANES, SGU_CHUNK, 2 * SGU_CHUNK),
            const(SGU_WIDTH // LANES, SGU_CHUNK, LANES),
            const(1, D_MODEL),
            const(N_EXPERTS, D_MODEL),
            const(N_EXPERTS, D_MODEL),
            const(N_EXPERTS, 1),
        ],
        out_specs=[
            pl.BlockSpec((tm, D_MODEL), lambda i: (prev(i), 0)),
            pl.BlockSpec((tm, D_MODEL), lambda i: (prev(i), 0)),
            pl.BlockSpec((TOP_K, tm), lambda i: (0, prev(i))),
            pl.BlockSpec((TOP_K, tm), lambda i: (0, prev(i))),
            pl.BlockSpec((1, N_EXPERTS, LANES), lambda i: (prev(i), 0, 0)),
            pl.BlockSpec((N_EXPERTS, LANES), lambda i: (0, 0)),
        ],
        out_shape=[
            jax.ShapeDtypeStruct((t, D_MODEL), F32),
            jax.ShapeDtypeStruct((t, D_MODEL), BF16),
            jax.ShapeDtypeStruct((TOP_K, t), I32),
            jax.ShapeDtypeStruct((TOP_K, t), F32),
            jax.ShapeDtypeStruct((t // tm, N_EXPERTS, LANES), I32),
            jax.ShapeDtypeStruct((N_EXPERTS, LANES), I32),
        ],
        scratch_shapes=[
            pltpu.VMEM((tm, SGU_WIDTH), BF16),
            pltpu.VMEM((N_EXPERTS, LANES), F32),
        ] + [pltpu.VMEM((tm, D_MODEL), BF16)] * 2,
        compiler_params=_params(48, ("arbitrary",)),
        name="post",
    )(attn2, z2, z2, gates2, gates2, x2, wa, wb, wo, lng, lnb, wcat, bcat, n2g, wrh, wrl, br)


def _route_rows(e_ref, tm):
    e = e_ref[...]
    eio = lax.broadcasted_iota(I32, (N_EXPERTS, tm), 0)
    onehots = [eio == e[k:k + 1, :] for k in range(TOP_K)]
    sel = jnp.zeros((N_EXPERTS, tm), F32)
    for oh in onehots:
        sel = jnp.where(oh, 1.0, sel)
    ti = lax.broadcasted_iota(I32, (tm, tm), 0)
    tj = lax.broadcasted_iota(I32, (tm, tm), 1)
    before = jnp.where(ti < tj, 1.0, 0.0).astype(BF16)
    slot = _dot(sel.astype(BF16), before)
    tiles = jnp.floor((jnp.sum(sel, axis=1, keepdims=True) + (SUBLANES - 1)) * (1.0 / SUBLANES))
    xi = lax.broadcasted_iota(I32, (N_EXPERTS, N_EXPERTS), 0)
    xj = lax.broadcasted_iota(I32, (N_EXPERTS, N_EXPERTS), 1)
    lower = jnp.where(xj < xi, 1.0, 0.0).astype(BF16)
    off = _dot(lower, jnp.broadcast_to(tiles, (N_EXPERTS, LANES)).astype(BF16))[:, 0:1] * float(SUBLANES)
    row = off + slot
    return [jnp.sum(jnp.where(oh, row, 0.0), axis=0, keepdims=True) for oh in onehots]


def _pick(rows, values, default):
    tm = rows[0].shape[1]
    rio = lax.broadcasted_iota(I32, (ROUTE_ROWS, tm), 0).astype(F32)
    out = default
    for r, v in zip(rows, values):
        out = jnp.where(rio == r, v, out)
    return out


def _wait_rows(buf, hbm_ref, sem, n):
    n = pl.multiple_of(n, SUBLANES)

    @pl.when(n > 0)
    def _():
        pltpu.make_async_copy(buf.at[pl.ds(0, n), :], hbm_ref.at[pl.ds(0, n), :], sem).wait()


def _dispatch_kernel(start_ref, off_ref, cnt_ref, tot_ref, za_ref, zb_ref, hf_ref, e_ref, g_ref, xs_ref, rows_ref,
                     obuf, zbuf, sem, semz):
    tm = hf_ref.shape[0]
    i = pl.program_id(0)
    last = pl.num_programs(0) - 1
    cur = lax.rem(i, 2)
    rows = _route_rows(e_ref, tm)
    for k in range(TOP_K):
        rows_ref[k:k + 1, :] = rows[k]
    g = g_ref[...]
    pt = _pick(rows, [1.0] * TOP_K, 0.0).astype(BF16)
    gate = jnp.sum(_pick(rows, [g[k:k + 1, :] for k in range(TOP_K)], 0.0), axis=1, keepdims=True)

    @pl.when(i >= 2)
    def _():
        _wait_rows(obuf.at[cur], xs_ref, sem.at[cur], tot_ref[jnp.maximum(i - 2, 0)])

    obuf[cur, :, 0:D_MODEL] = _dot(pt, hf_ref[...])
    obuf[cur, :, D_MODEL:XS_WIDTH] = jnp.broadcast_to(gate, (ROUTE_ROWS, LANES))

    for x in range(N_EXPERTS):
        n = pl.multiple_of(cnt_ref[i * N_EXPERTS + x], SUBLANES)

        @pl.when(n > 0)
        def _(x=x, n=n):
            src = pl.multiple_of(off_ref[i * N_EXPERTS + x], SUBLANES)
            dst = pl.multiple_of(start_ref[i * N_EXPERTS + x], SUBLANES)
            pltpu.make_async_copy(obuf.at[cur, pl.ds(src, n), :], xs_ref.at[pl.ds(dst, n), :], sem.at[cur]).start()

    @pl.when(i == last)
    def _():
        @pl.when(i >= 1)
        def _():
            _wait_rows(obuf.at[1 - cur], xs_ref, sem.at[1 - cur], tot_ref[jnp.maximum(i - 1, 0)])

        _wait_rows(obuf.at[cur], xs_ref, sem.at[cur], tot_ref[i])
        zbuf[...] = jnp.zeros_like(zbuf)

        def zero_region(z, act):
            n = zb_ref[z] - za_ref[z]
            big = n // EXPERT_TILE
            rest = pl.multiple_of(n - big * EXPERT_TILE, SUBLANES)

            def piece(row, nrows):
                return pltpu.make_async_copy(zbuf.at[pl.ds(0, nrows), :],
                                             xs_ref.at[pl.ds(pl.multiple_of(row, SUBLANES), nrows), :], semz)

            lax.fori_loop(0, big, lambda m, c: (act(piece(za_ref[z] + m * EXPERT_TILE, EXPERT_TILE)), c)[1], 0)

            @pl.when(rest > 0)
            def _():
                act(piece(za_ref[z] + big * EXPERT_TILE, rest))

        for z in range(N_EXPERTS + 1):
            zero_region(z, lambda cp: cp.start())
        for z in range(N_EXPERTS + 1):
            zero_region(z, lambda cp: cp.wait())


def _dispatch(start, off, cnt, tot, za, zb, hf, e4, g4, cap):
    t = hf.shape[0]
    tm = TM_ROUTE
    tok = lambda i, *_: (0, i)
    return pl.pallas_call(
        _dispatch_kernel,
        grid_spec=pltpu.PrefetchScalarGridSpec(
            num_scalar_prefetch=6,
            grid=(t // tm,),
            in_specs=[
                pl.BlockSpec((tm, D_MODEL), lambda i, *_: (i, 0)),
                pl.BlockSpec((TOP_K, tm), tok),
                pl.BlockSpec((TOP_K, tm), tok),
            ],
            out_specs=[pl.BlockSpec(memory_space=pl.ANY), pl.BlockSpec((TOP_K, tm), tok)],
            scratch_shapes=[
                pltpu.VMEM((2, ROUTE_ROWS, XS_WIDTH), F32),
                pltpu.VMEM((EXPERT_TILE, XS_WIDTH), F32),
                pltpu.SemaphoreType.DMA((2,)),
                pltpu.SemaphoreType.DMA(()),
            ],
        ),
        out_shape=[jax.ShapeDtypeStruct((cap, XS_WIDTH), F32), jax.ShapeDtypeStruct((TOP_K, t), F32)],
        compiler_params=_params(48, ("arbitrary",)),
        name="dispatch",
    )(start, off, cnt, tot, za, zb, hf, e4, g4)


def _expert_kernel(te_ref, nu_ref, xs_ref, wgu_ref, bgu_ref, wdn_ref, bdn_ref, ys_ref, wgu_s, wdn_s):
    i = pl.program_id(0)

    @pl.when(i < nu_ref[0])
    def _():
        @pl.when((i == 0) | (te_ref[i] != te_ref[jnp.maximum(i - 1, 0)]))
        def _():
            wgu_s[...] = wgu_ref[0].astype(BF16)
            wdn_s[...] = wdn_ref[0].astype(BF16)

        x = xs_ref[:, 0:D_MODEL].astype(BF16)
        route_gate = xs_ref[:, D_MODEL:D_MODEL + 1]
        gu = _dot(x, wgu_s[...]) + bgu_ref[0]
        gate = jnp.minimum(gu[:, :D_EXPERT], SWIGLU_LIMIT)
        up = jnp.clip(gu[:, D_EXPERT:], -SWIGLU_LIMIT, SWIGLU_LIMIT)
        act = (up + 1.0) * (gate * jax.nn.sigmoid(SWIGLU_ALPHA * gate))
        ys_ref[...] = (_dot(act.astype(BF16), wdn_s[...]) + bdn_ref[0]) * route_gate

    @pl.when(i >= nu_ref[0])
    def _():
        ys_ref[...] = jnp.zeros_like(ys_ref)


def _experts(tile_expert, n_used, xs, wgu, bgu, wdn, bdn):
    cap = xs.shape[0]
    n_tiles = cap // EXPERT_TILE
    row = lambda i, te, nu: (jnp.minimum(i, nu[0] - 1), 0)
    by_expert = lambda i, te, nu: (te[i], 0, 0)
    return pl.pallas_call(
        _expert_kernel,
        grid_spec=pltpu.PrefetchScalarGridSpec(
            num_scalar_prefetch=2,
            grid=(n_tiles,),
            in_specs=[
                pl.BlockSpec((EXPERT_TILE, XS_WIDTH), row),
                pl.BlockSpec((1, D_MODEL, 2 * D_EXPERT), by_expert),
                pl.BlockSpec((1, 1, 2 * D_EXPERT), by_expert),
                pl.BlockSpec((1, D_EXPERT, D_MODEL), by_expert),
                pl.BlockSpec((1, 1, D_MODEL), by_expert),
            ],
            out_specs=pl.BlockSpec((EXPERT_TILE, D_MODEL), lambda i, te, nu: (i, 0)),
            scratch_shapes=[
                pltpu.VMEM((D_MODEL, 2 * D_EXPERT), BF16),
                pltpu.VMEM((D_EXPERT, D_MODEL), BF16),
            ],
        ),
        out_shape=jax.ShapeDtypeStruct((cap, D_MODEL), F32),
        compiler_params=_params(56, ("arbitrary",)),
        name="experts",
    )(tile_expert, n_used, xs, wgu, bgu, wdn, bdn)


def _combine_kernel(start_ref, off_ref, cnt_ref, tot_ref, x1_ref, rows_ref, ys_ref, o_ref, sbuf, sem):
    i = pl.program_id(0)
    last = pl.num_programs(0) - 1
    cur = lax.rem(i, 2)

    def fetch(step, b):
        for x in range(N_EXPERTS):
            n = pl.multiple_of(cnt_ref[step * N_EXPERTS + x], SUBLANES)

            @pl.when(n > 0)
            def _(x=x, n=n):
                src = pl.multiple_of(start_ref[step * N_EXPERTS + x], SUBLANES)
                dst = pl.multiple_of(off_ref[step * N_EXPERTS + x], SUBLANES)
                pltpu.make_async_copy(ys_ref.at[pl.ds(src, n), :], sbuf.at[b, pl.ds(dst, n), :], sem.at[b]).start()

    @pl.when(i == 0)
    def _():
        sbuf[...] = jnp.zeros_like(sbuf)
        fetch(i, cur)

    @pl.when(i < last)
    def _():
        fetch(i + 1, 1 - cur)

    rows = [rows_ref[k:k + 1, :] for k in range(TOP_K)]
    pt = _pick(rows, [1.0] * TOP_K, 0.0).astype(BF16)
    _wait_rows(sbuf.at[cur], ys_ref, sem.at[cur], tot_ref[i])
    picked = lax.dot_general(pt, sbuf[cur].astype(BF16), _TN, preferred_element_type=F32)
    o_ref[...] = x1_ref[...] + picked


def _combine(start, off, cnt, tot, x1, rows4, ys):
    t = x1.shape[0]
    tm = TM_ROUTE
    return pl.pallas_call(
        _combine_kernel,
        grid_spec=pltpu.PrefetchScalarGridSpec(
            num_scalar_prefetch=4,
            grid=(t // tm,),
            in_specs=[
                pl.BlockSpec((tm, D_MODEL), lambda i, *_: (i, 0)),
                pl.BlockSpec((TOP_K, tm), lambda i, *_: (0, i)),
                pl.BlockSpec(memory_space=pl.ANY),
            ],
            out_specs=pl.BlockSpec((tm, D_MODEL), lambda i, *_: (i, 0)),
            scratch_shapes=[
                pltpu.VMEM((2, ROUTE_ROWS, D_MODEL), F32),
                pltpu.SemaphoreType.DMA((2,)),
            ],
        ),
        out_shape=jax.ShapeDtypeStruct((t, D_MODEL), F32),
        compiler_params=_params(48, ("arbitrary",)),
        name="combine",
    )(start, off, cnt, tot, x1, rows4, ys)


def _layer(x, positions, norm1_g, w_in, q_norm_g, k_norm_g, sgu_ln_g, sgu_ln_b, w_spatial, b_spatial,
           w_branch_a, w_branch_b, w_out, norm2_g, w_router, b_router, w_gate_up, b_gate_up, w_down, b_down):
    b, seq, _ = x.shape
    t = b * seq
    x2 = x.reshape(t, D_MODEL)

    inv_freq = ROPE_THETA ** (-jnp.arange(ROT_HALF, dtype=F32) / ROT_HALF)
    invf = jnp.tile(inv_freq, LANES // ROT_HALF).reshape(1, LANES)
    dim_of_lane = [(l // (2 * ROT_HALF)) * ROT_HALF + l % ROT_HALF for l in range(LANES)]
    qg = q_norm_g[jnp.array(dim_of_lane)].reshape(1, LANES)
    kg = k_norm_g[jnp.array(dim_of_lane)].reshape(1, LANES)
    w_bf = w_in.astype(BF16)
    qk = w_bf[:, :2 * ATTN_WIDTH].reshape(D_MODEL, 2 * ATTN_WIDTH // LANES, 2, 2, ROT_HALF)
    qk = qk.transpose(0, 1, 3, 2, 4).reshape(D_MODEL, 2 * ATTN_WIDTH)
    w_in_p = jnp.concatenate([qk, w_bf[:, 2 * ATTN_WIDTH:]], axis=1)
    n_pairs = SGU_WIDTH // LANES
    wcat = w_spatial.reshape(n_pairs, 2, SGU_CHUNK, SGU_CHUNK).transpose(0, 2, 1, 3).reshape(n_pairs, SGU_CHUNK, 2 * SGU_CHUNK)
    bcat = jnp.repeat(b_spatial.reshape(n_pairs, 2, SGU_CHUNK).transpose(0, 2, 1), SGU_GROUP_DIM, axis=2)
    wr_t = w_router.T
    wr_hi = wr_t.astype(BF16)
    wr_lo = (wr_t - wr_hi.astype(F32)).astype(BF16)

    *qkv_groups, z, gates = _inproj(x2, norm1_g.reshape(1, D_MODEL), w_in_p, positions.reshape(-1, LANES // ROT_HALF), invf, qg, kg, b, seq)
    attn = _attention(qkv_groups)
    x1, hf, e4, g4, cst, counts = _post(
        attn.reshape(t, ATTN_WIDTH), z, gates, x2,
        w_branch_a.astype(BF16), w_branch_b.astype(BF16), w_out.astype(BF16),
        sgu_ln_g.reshape(1, SGU_WIDTH), sgu_ln_b.reshape(1, SGU_WIDTH), wcat.astype(BF16), bcat,
        norm2_g.reshape(1, D_MODEL), wr_hi, wr_lo, b_router.reshape(N_EXPERTS, 1))

    steps_per_tile = TM_POST // TM_ROUTE
    n_steps = t // TM_ROUTE
    before = cst[:, :, :steps_per_tile].transpose(0, 2, 1).reshape(n_steps, N_EXPERTS)
    total = counts[:, 0]
    cnt = jnp.concatenate([before[1:], total[None, :]], axis=0) - before
    cnt_al = (cnt + SUBLANES - 1) // SUBLANES * SUBLANES
    off = (jnp.cumsum(cnt_al, axis=1) - cnt_al).astype(I32).reshape(-1)
    tot = jnp.sum(cnt_al, axis=1).astype(I32)
    used = jnp.sum(cnt_al, axis=0)
    region = (used + EXPERT_TILE - 1) // EXPERT_TILE * EXPERT_TILE
    rend = jnp.cumsum(region)
    rstart = rend - region
    start = (rstart[None, :] + jnp.cumsum(cnt_al, axis=0) - cnt_al).astype(I32).reshape(-1)
    bound = t * TOP_K + n_steps * N_EXPERTS * (SUBLANES - 1) + N_EXPERTS * (EXPERT_TILE - 1)
    cap = (bound + EXPERT_TILE - 1) // EXPERT_TILE * EXPERT_TILE
    n_tiles = cap // EXPERT_TILE
    tile_expert = jnp.sum(jnp.arange(n_tiles, dtype=I32)[:, None] * EXPERT_TILE >= rend[None, :], axis=1)
    tile_expert = jnp.minimum(tile_expert, N_EXPERTS - 1).astype(I32)
    n_used = (rend[-1:] // EXPERT_TILE).astype(I32)
    za = jnp.concatenate([rstart + used, rend[-1:]]).astype(I32)
    zb = jnp.concatenate([rend, jnp.full((1,), cap, rend.dtype)]).astype(I32)
    cnt_al = cnt_al.astype(I32).reshape(-1)

    xs, rows4 = _dispatch(start, off, cnt_al, tot, za, zb, hf, e4, g4, cap)
    ys = _experts(tile_expert, n_used, xs,
                  w_gate_up, b_gate_up.reshape(N_EXPERTS, 1, 2 * D_EXPERT),
                  w_down, b_down.reshape(N_EXPERTS, 1, D_MODEL))
    out = _combine(start, off, cnt_al, tot, x1, rows4, ys)
    return out.reshape(b, seq, D_MODEL)


def kernel(x, positions, norm1_g, w_in, q_norm_g, k_norm_g, sgu_ln_g, sgu_ln_b, w_spatial, b_spatial, w_branch_a, w_branch_b, w_out, norm2_g, w_router, b_router, w_gate_up, b_gate_up, w_down, b_down):
    for layer in range(norm1_g.shape[0]):
        x = _layer(x, positions, norm1_g[layer], w_in[layer], q_norm_g[layer], k_norm_g[layer],
                   sgu_ln_g[layer], sgu_ln_b[layer], w_spatial[layer], b_spatial[layer],
                   w_branch_a[layer], w_branch_b[layer], w_out[layer], norm2_g[layer],
                   w_router[layer], b_router[layer], w_gate_up[layer], b_gate_up[layer],
                   w_down[layer], b_down[layer])
    return x
```

```python
import functools

import jax
import jax.numpy as jnp
from jax import lax
from jax.experimental import pallas as pl
from jax.experimental.pallas import tpu as pltpu

F32 = jnp.float32
BF16 = jnp.bfloat16
I32 = jnp.int32

D_MODEL = 1024
HEAD_DIM = 64
ROT_HALF = HEAD_DIM // 2
LOG2E = 1.4426950408889634
LN2 = 0.6931471805599453
Q_SCALE = HEAD_DIM ** -0.5 * LOG2E
ATTN_GROUPS = ((128, 1), (512, 4), (2048, 16))
N_GROUPS = len(ATTN_GROUPS)
GROUP_WIDTH = 256
ATTN_WIDTH = N_GROUPS * GROUP_WIDTH
BLK = 128
ROPE_THETA = 10000.0
SGU_CHUNK = 128
SGU_GROUP_DIM = 64
SGU_WIDTH = 768
GATE_WIDTH = 2 * D_MODEL
N_EXPERTS = 32
TOP_K = 4
D_EXPERT = D_MODEL
SWIGLU_LIMIT = 7.0
SWIGLU_ALPHA = 1.702
EPS = 1e-6
NEG_INF = -1e30
LANES = 128

TM_INPROJ = 512
TM_POST = 512
TM_ROUTE = 256
SUBLANES = 8
ROUTE_ROWS = TM_ROUTE * TOP_K + N_EXPERTS * 2 * SUBLANES
EXPERT_TILE = 512
XS_WIDTH = D_MODEL + LANES
ROW_CHUNK = 256

_NT = (((1,), (1,)), ((), ()))
_TN = (((0,), (0,)), ((), ()))


def _dot(a, b):
    return jnp.dot(a, b, preferred_element_type=F32)


def _dot_nt(a, b):
    return lax.dot_general(a, b, _NT, preferred_element_type=F32)


def _params(vmem_mb, sem):
    return pltpu.CompilerParams(dimension_semantics=sem, vmem_limit_bytes=vmem_mb * 1024 * 1024)


def _split3(x):
    a = x.astype(BF16)
    r = x - a.astype(F32)
    b = r.astype(BF16)
    return a, b, (r - b.astype(F32)).astype(BF16)


def _inproj_kernel(x_ref, g_ref, w_ref, pos_ref, invf_ref, qg_ref, kg_ref,
                   o0_ref, o1_ref, o2_ref, oz_ref, og_ref, raw_a, raw_b, stage_s, cos_s, sin_s):
    step = pl.program_id(0)

    @pl.when(step == 0)
    def _():
        raw_b[...] = jnp.zeros_like(raw_b)

    args = (x_ref, g_ref, w_ref, pos_ref, invf_ref, qg_ref, kg_ref, o0_ref, o1_ref, o2_ref, oz_ref, og_ref)

    @pl.when(lax.rem(step, 2) == 0)
    def _():
        _inproj_step(*args, raw_a, raw_b, stage_s, cos_s, sin_s)

    @pl.when(lax.rem(step, 2) == 1)
    def _():
        _inproj_step(*args, raw_b, raw_a, stage_s, cos_s, sin_s)


def _inproj_step(x_ref, g_ref, w_ref, pos_ref, invf_ref, qg_ref, kg_ref,
                 o0_ref, o1_ref, o2_ref, oz_ref, og_ref, raw_w, raw_r, stage_s, cos_s, sin_s):
    tm = x_ref.shape[0]
    x = x_ref[...]
    y = x * lax.rsqrt(jnp.mean(x * x, axis=-1, keepdims=True) + EPS)
    h = (y * g_ref[...]).astype(BF16)

    def project(lo):
        p = _dot(h, w_ref[:, lo:lo + GROUP_WIDTH])
        zc = 3 * ATTN_WIDTH
        gc = zc + 2 * SGU_WIDTH
        if lo < zc:
            raw_w[:, lo:lo + GROUP_WIDTH] = p
        elif lo < gc:
            oz_ref[:, lo - zc:lo - zc + GROUP_WIDTH] = p.astype(BF16)
        else:
            og_ref[:, lo - gc:lo - gc + GROUP_WIDTH] = p.astype(BF16)

    def rotary_tables():
        per_row = LANES // ROT_HALF
        lane = lax.broadcasted_iota(I32, (tm // per_row, LANES), 1)
        p4 = pos_ref[...].astype(F32)
        posd = p4[:, per_row - 1:per_row]
        for j in range(per_row - 2, -1, -1):
            posd = jnp.where(lane < (j + 1) * ROT_HALF, p4[:, j:j + 1], posd)
        ang = posd * invf_ref[...]
        cparts = _split3(jnp.cos(ang))
        sparts = _split3(jnp.sin(ang))
        src = lax.broadcasted_iota(I32, (LANES, LANES), 0)
        dst = lax.broadcasted_iota(I32, (LANES, LANES), 1)
        for j in range(per_row):
            pick = src == j * ROT_HALF + dst % ROT_HALF
            spread = jnp.where(pick, 1.0, 0.0).astype(BF16)
            spread_neg = jnp.where(pick, jnp.where(dst < 2 * ROT_HALF, -1.0, 1.0), 0.0).astype(BF16)
            cos_s[pl.ds(j, tm // per_row, stride=per_row), :] = sum(_dot(p, spread) for p in cparts)
            sin_s[pl.ds(j, tm // per_row, stride=per_row), :] = sum(_dot(p, spread_neg) for p in sparts)

    li = (lax.broadcasted_iota(I32, (LANES, LANES), 0) // ROT_HALF) % 2
    lj = (lax.broadcasted_iota(I32, (LANES, LANES), 1) // ROT_HALF) % 2
    head_sum = jnp.where(li == lj, 1.0, 0.0).astype(BF16)
    outs = (o0_ref, o1_ref, o2_ref)

    def finish(gi, d, ti, mode, gain_ref, c, stage):
        col = ti * ATTN_WIDTH + gi * GROUP_WIDTH
        t = raw_r[:, col + c * LANES:col + (c + 1) * LANES]
        if mode != "v":
            ss = t * t
            hi = ss.astype(BF16)
            lo = (ss - hi.astype(F32)).astype(BF16)
            ms = (_dot(hi, head_sum) + _dot(lo, head_sum)) * (1.0 / HEAD_DIM)
            t = t * lax.rsqrt(ms + EPS) * gain_ref[...]
            t = t * cos_s[...] + pltpu.roll(t, 2 * ROT_HALF, 1) * sin_s[...]
            if mode == "q":
                t = t * Q_SCALE
        lanes = slice(ti * GROUP_WIDTH + c * LANES, ti * GROUP_WIDTH + (c + 1) * LANES)
        if d == 1:
            outs[gi][0, 0, :, lanes] = t.astype(BF16)
        else:
            stage_s[stage] = t
            for r in range(d):
                outs[gi][0, r, :, lanes] = stage_s[stage, pl.ds(r, tm // d, stride=d), :].astype(BF16)

    pieces = [rotary_tables]
    stage = 0
    for gi, (_, d) in enumerate(ATTN_GROUPS):
        for ti, (mode, gain_ref) in enumerate((("q", qg_ref), ("k", kg_ref), ("v", None))):
            for c in range(GROUP_WIDTH // LANES):
                pieces.append(functools.partial(finish, gi, d, ti, mode, gain_ref, c, stage))
                stage += d > 1

    chunks = list(range(0, w_ref.shape[1], GROUP_WIDTH))
    for k in range(max(len(chunks), len(pieces))):
        if k < len(chunks):
            project(chunks[k])
        if k < len(pieces):
            pieces[k]()


def _inproj(x2, g, w, pos2, invf, qg, kg, b, seq):
    t = x2.shape[0]
    n = w.shape[1]
    tm = TM_INPROJ
    nj = seq // tm
    n_tiles = b * nj
    n_stage = sum(1 for _, d in ATTN_GROUPS if d > 1) * 3 * (GROUP_WIDTH // LANES)
    proj_row = lambda s: (jnp.minimum(s, n_tiles - 1), 0)
    done_row = lambda s: (jnp.maximum(s - 1, 0), 0)
    done_blk = lambda s: (jnp.maximum(s - 1, 0) // nj, 0, jnp.maximum(s - 1, 0) % nj, 0)
    const = lambda s: (0, 0)
    per_row = LANES // ROT_HALF
    return pl.pallas_call(
        _inproj_kernel,
        grid=(n_tiles + 1,),
        in_specs=[
            pl.BlockSpec((tm, D_MODEL), proj_row),
            pl.BlockSpec((1, D_MODEL), const),
            pl.BlockSpec((D_MODEL, n), const, pipeline_mode=pl.Buffered(1)),
            pl.BlockSpec((tm // per_row, per_row), done_row),
            pl.BlockSpec((1, LANES), const),
            pl.BlockSpec((1, LANES), const),
            pl.BlockSpec((1, LANES), const),
        ],
        out_specs=[pl.BlockSpec((1, d, tm // d, ATTN_WIDTH), done_blk) for _, d in ATTN_GROUPS] + [
            pl.BlockSpec((tm, 2 * SGU_WIDTH), proj_row),
            pl.BlockSpec((tm, GATE_WIDTH), proj_row),
        ],
        out_shape=[jax.ShapeDtypeStruct((b, d, seq // d, ATTN_WIDTH), BF16) for _, d in ATTN_GROUPS] + [
            jax.ShapeDtypeStruct((t, 2 * SGU_WIDTH), BF16),
            jax.ShapeDtypeStruct((t, GATE_WIDTH), BF16),
        ],
        scratch_shapes=[
            pltpu.VMEM((tm, 3 * ATTN_WIDTH), F32),
            pltpu.VMEM((tm, 3 * ATTN_WIDTH), F32),
            pltpu.VMEM((n_stage, tm, LANES), F32),
            pltpu.VMEM((tm, LANES), F32),
            pltpu.VMEM((tm, LANES), F32),
        ],
        compiler_params=_params(56, ("arbitrary",)),
        name="inproj",
    )(x2, g, w, pos2, invf, qg, kg)


def _attn_blocks(qkv_ref, ores_s, lres_s, seq, d):
    sub = seq // d
    nb = sub // BLK
    win = 2 * BLK if nb > 1 else BLK
    log2d = d.bit_length() - 1
    lane = lax.broadcasted_iota(I32, (BLK, GROUP_WIDTH), 1)
    q_head = 2 * (lane // LANES) + (lane // ROT_HALF) % 2
    qi = lax.broadcasted_iota(I32, (BLK, win), 0)
    kj = lax.broadcasted_iota(I32, (BLK, win), 1)
    rel = qi - kj

    def body(idx, carry):
        r = idx & (d - 1)
        n = idx >> log2d
        row0 = pl.multiple_of(r * sub + n * BLK, BLK)
        if nb > 1:
            kn = jnp.maximum(n - 1, 0)
            k0 = pl.multiple_of(r * sub + kn * BLK, BLK)
            dist = rel + (n - kn) * BLK
        else:
            k0 = row0
            dist = rel
        valid = lax.bitcast_convert_type(dist, jnp.uint32) <= jnp.uint32(BLK)
        q = qkv_ref[0, pl.ds(row0, BLK), 0:GROUP_WIDTH]
        kw = qkv_ref[0, pl.ds(k0, win), GROUP_WIDTH:2 * GROUP_WIDTH]
        vw = qkv_ref[0, pl.ds(k0, win), 2 * GROUP_WIDTH:3 * GROUP_WIDTH]
        zero = jnp.zeros_like(q)
        qm = jnp.concatenate([jnp.where(q_head == h, q, zero) for h in range(4)], axis=0)
        s = _dot_nt(qm, kw)
        ps, ms, ls = [], [], []
        for h in range(4):
            sh = jnp.where(valid, s[h * BLK:(h + 1) * BLK], NEG_INF)
            m = jnp.max(sh, axis=-1, keepdims=True)
            p = jnp.exp2(sh - m)
            ls.append(jnp.sum(p, axis=-1, keepdims=True))
            ms.append(m)
            ps.append(p.astype(BF16))
        pv = _dot(jnp.concatenate(ps, axis=0), vw)
        for h in range(4):
            lanes = slice(h * HEAD_DIM, (h + 1) * HEAD_DIM)
            ores_s[pl.ds(row0, BLK), lanes] = pv[h * BLK:(h + 1) * BLK, lanes] * (1.0 / ls[h])
            lres_s[pl.ds(row0, BLK), lanes] = jnp.broadcast_to((ms[h] + jnp.log2(ls[h])) * LN2, (BLK, HEAD_DIM))
        return carry

    lax.fori_loop(0, seq // BLK, body, 0, unroll=8)


def _attn_kernel(g0_ref, g1_ref, g2_ref, o_ref, ores_s, lres_s, onat_s, lnat_s, *, seq):
    for gi, ((_, d), qkv_ref) in enumerate(zip(ATTN_GROUPS, (g0_ref, g1_ref, g2_ref))):
        _attn_blocks(qkv_ref, ores_s, lres_s, seq, d)
        sub = seq // d
        for r in range(d):
            for c in range(GROUP_WIDTH // LANES):
                t = gi * (GROUP_WIDTH // LANES) + c
                onat_s[t, pl.ds(r, sub, stride=d), :] = ores_s[r * sub:(r + 1) * sub, c * LANES:(c + 1) * LANES]
                lnat_s[t, pl.ds(r, sub, stride=d), :] = lres_s[r * sub:(r + 1) * sub, c * LANES:(c + 1) * LANES]

    tiles = GROUP_WIDTH // LANES

    def body(i, carry):
        r0 = pl.multiple_of(i * ROW_CHUNK, ROW_CHUNK)
        for c in range(tiles):
            ls = [lnat_s[gi * tiles + c, pl.ds(r0, ROW_CHUNK), :] for gi in range(N_GROUPS)]
            m = jnp.maximum(jnp.maximum(ls[0], ls[1]), ls[2])
            es = [jnp.exp(l - m) for l in ls]
            inv = 1.0 / (es[0] + es[1] + es[2])
            for gi in range(N_GROUPS):
                t = gi * tiles + c
                o_ref[0, pl.ds(r0, ROW_CHUNK), t * LANES:(t + 1) * LANES] = (
                    onat_s[t, pl.ds(r0, ROW_CHUNK), :] * (es[gi] * inv)).astype(BF16)
        return carry

    lax.fori_loop(0, seq // ROW_CHUNK, body, 0, unroll=2)


def _attention(qkv_groups):
    b = qkv_groups[0].shape[0]
    seq = qkv_groups[0].shape[1] * qkv_groups[0].shape[2]
    tiles = GROUP_WIDTH // LANES
    blk = pl.BlockSpec((1, seq, ATTN_WIDTH), lambda i: (i, 0, 0))
    return pl.pallas_call(
        functools.partial(_attn_kernel, seq=seq),
        grid=(b,),
        in_specs=[blk] * N_GROUPS,
        out_specs=blk,
        out_shape=jax.ShapeDtypeStruct((b, seq, ATTN_WIDTH), BF16),
        scratch_shapes=[
            pltpu.VMEM((seq, GROUP_WIDTH), F32),
            pltpu.VMEM((seq, GROUP_WIDTH), F32),
            pltpu.VMEM((N_GROUPS * tiles, seq, LANES), F32),
            pltpu.VMEM((N_GROUPS * tiles, seq, LANES), F32),
        ],
        compiler_params=_params(48, ("arbitrary",)),
        name="attention",
    )(*[a.reshape(b, seq, ATTN_WIDTH) for a in qkv_groups])


def _gelu(x):
    return 0.5 * x * (1.0 + lax.erf(x * 0.7071067811865476))


def _post_kernel(attn_ref, u_ref, vz_ref, ga_ref, gb_ref, x_ref, wa_ref, wb_ref, wo_ref,
                 lng_ref, lnb_ref, wcat_ref, bcat_ref, n2g_ref, wrh_ref, wrl_ref, br_ref,
                 x1_ref, hf_ref, e_ref, g_ref, cst_ref, cnt_ref, sgu_s, carry_s, merged_a, merged_b):
    step = pl.program_id(0)

    @pl.when(step == 0)
    def _():
        carry_s[...] = jnp.zeros_like(carry_s)
        merged_b[...] = jnp.zeros_like(merged_b)

    args = (attn_ref, u_ref, vz_ref, ga_ref, gb_ref, x_ref, wa_ref, wb_ref, wo_ref, lng_ref, lnb_ref, wcat_ref,
            bcat_ref, n2g_ref, wrh_ref, wrl_ref, br_ref, x1_ref, hf_ref, e_ref, g_ref, cst_ref, cnt_ref, sgu_s, carry_s)

    @pl.when(lax.rem(step, 2) == 0)
    def _():
        _post_step(*args, merged_a, merged_b)

    @pl.when(lax.rem(step, 2) == 1)
    def _():
        _post_step(*args, merged_b, merged_a)


def _post_step(attn_ref, u_ref, vz_ref, ga_ref, gb_ref, x_ref, wa_ref, wb_ref, wo_ref,
               lng_ref, lnb_ref, wcat_ref, bcat_ref, n2g_ref, wrh_ref, wrl_ref, br_ref,
               x1_ref, hf_ref, e_ref, g_ref, cst_ref, cnt_ref, sgu_s, carry_s, merged_w, merged_r):
    tm = x_ref.shape[0]
    counted = jnp.where(pl.program_id(0) > 0, 1.0, 0.0)

    lane = lax.broadcasted_iota(I32, (SGU_CHUNK, LANES), 1)
    low = lane < SGU_GROUP_DIM
    trow = lax.broadcasted_iota(I32, (SGU_CHUNK, 2 * SGU_CHUNK), 0)
    tcol = lax.broadcasted_iota(I32, (SGU_CHUNK, 2 * SGU_CHUNK), 1) % SGU_CHUNK
    causal = tcol <= trow

    def chunk(c):
        r0 = c * SGU_CHUNK
        u = _gelu(u_ref[pl.ds(r0, SGU_CHUNK), :].astype(F32))
        v = _gelu(vz_ref[pl.ds(r0, SGU_CHUNK), :].astype(F32))
        mu = jnp.mean(v, axis=-1, keepdims=True)
        vc = v - mu
        vn = vc * lax.rsqrt(jnp.mean(vc * vc, axis=-1, keepdims=True) + EPS)
        vn = (vn * lng_ref[...] + lnb_ref[...]).astype(BF16)
        zero = jnp.zeros((SGU_CHUNK, LANES), BF16)
        for j in range(SGU_WIDTH // LANES):
            vt = vn[:, j * LANES:(j + 1) * LANES]
            rhs = jnp.concatenate([jnp.where(low, vt, zero), jnp.where(low, zero, vt)], axis=0)
            wj = jnp.where(causal, wcat_ref[j], jnp.zeros((), BF16))
            mixed = _dot(wj, rhs) + bcat_ref[j]
            sgu_s[pl.ds(r0, SGU_CHUNK), j * LANES:(j + 1) * LANES] = (u[:, j * LANES:(j + 1) * LANES] * mixed).astype(BF16)

    eio = lax.broadcasted_iota(I32, (N_EXPERTS, tm), 0)
    state = {}
    top_v, onehots = [], []

    def route_logits(hi, lo):
        state["work"] = (_dot_nt(wrh_ref[...], hi) + _dot_nt(wrl_ref[...], hi) + _dot_nt(wrh_ref[...], lo)
                         + br_ref[...])

    def route_pick(k):
        work = state["work"]
        m = jnp.max(work, axis=0, keepdims=True)
        idx = jnp.min(jnp.where(work == m, eio, N_EXPERTS), axis=0, keepdims=True)
        oh = eio == idx
        state["work"] = jnp.where(oh, -jnp.inf, work)
        top_v.append(m)
        onehots.append(oh)
        e_ref[k:k + 1, :] = idx

    def route_finish():
        ex = [jnp.exp(v - top_v[0]) for v in top_v]
        inv = 1.0 / (ex[0] + ex[1] + ex[2] + ex[3])
        for k in range(TOP_K):
            g_ref[k:k + 1, :] = ex[k] * inv
        sel = jnp.zeros((N_EXPERTS, tm), F32)
        for oh in onehots:
            sel = jnp.where(oh, counted, sel)
        lane_e = lax.broadcasted_iota(I32, (N_EXPERTS, LANES), 1)
        run = carry_s[...]
        cst = jnp.zeros((N_EXPERTS, LANES), F32)
        for s in range(tm // TM_ROUTE):
            cst = jnp.where(lane_e == s, run, cst)
            run = run + jnp.sum(sel[:, s * TM_ROUTE:(s + 1) * TM_ROUTE], axis=1, keepdims=True)
        cst_ref[0] = cst.astype(I32)
        carry_s[...] = run
        cnt_ref[...] = run.astype(I32)

    n_parts = tm // SGU_CHUNK
    width = D_MODEL // n_parts
    x1_parts, y_a_parts = [], []
    for c in range(n_parts):
        cols = slice(c * width, (c + 1) * width)
        chunk(c)
        x1_parts.append(x_ref[:, cols] + _dot(merged_r[...], wo_ref[:, cols]))
        y_a_parts.append(_dot(attn_ref[...], wa_ref[:, cols]))
    x1 = jnp.concatenate(x1_parts, axis=1)
    x1_ref[...] = x1
    hf = x1 * lax.rsqrt(jnp.mean(x1 * x1, axis=-1, keepdims=True) + EPS) * n2g_ref[...]
    hi = hf.astype(BF16)
    hf_ref[...] = hi
    route_logits(hi, (hf - hi.astype(F32)).astype(BF16))
    for c in range(n_parts):
        cols = slice(c * width, (c + 1) * width)
        y_b = _dot(sgu_s[...], wb_ref[:, cols])
        merged_w[:, cols] = (jax.nn.sigmoid(ga_ref[:, cols].astype(F32)) * y_a_parts[c]
                             + jax.nn.sigmoid(gb_ref[:, cols].astype(F32)) * y_b).astype(BF16)
        if c < TOP_K:
            route_pick(c)
    for k in range(n_parts, TOP_K):
        route_pick(k)
    route_finish()


def _post(attn2, z2, gates2, x2, wa, wb, wo, lng, lnb, wcat, bcat, n2g, wrh, wrl, br):
    t = x2.shape[0]
    tm = TM_POST
    const = lambda *shape: pl.BlockSpec(shape, lambda i: (0,) * len(shape), pipeline_mode=pl.Buffered(1))
    n_tiles = t // tm
    cur = lambda i: jnp.minimum(i, n_tiles - 1)
    prev = lambda i: jnp.maximum(i - 1, 0)
    return pl.pallas_call(
        _post_kernel,
        grid=(n_tiles + 1,),
        in_specs=[
            pl.BlockSpec((tm, ATTN_WIDTH), lambda i: (cur(i), 0)),
            pl.BlockSpec((tm, SGU_WIDTH), lambda i: (cur(i), 0)),
            pl.BlockSpec((tm, SGU_WIDTH), lambda i: (cur(i), 1)),
            pl.BlockSpec((tm, D_MODEL), lambda i: (cur(i), 0)),
            pl.BlockSpec((tm, D_MODEL), lambda i: (cur(i), 1)),
            pl.BlockSpec((tm, D_MODEL), lambda i: (prev(i), 0)),
            const(ATTN_WIDTH, D_MODEL),
            const(SGU_WIDTH, D_MODEL),
            const(D_MODEL, D_MODEL),
            const(1, SGU_WIDTH),
            const(1, SGU_WIDTH),
            const(SGU_WIDTH // LANES, SGU_CHUNK, 2 * SGU_CHUNK),
            const(SGU_WIDTH // LANES, SGU_CHUNK, LANES),
            const(1, D_MODEL),
            const(N_EXPERTS, D_MODEL),
            const(N_EXPERTS, D_MODEL),
            const(N_EXPERTS, 1),
        ],
        out_specs=[
            pl.BlockSpec((tm, D_MODEL), lambda i: (prev(i), 0)),
            pl.BlockSpec((tm, D_MODEL), lambda i: (prev(i), 0)),
            pl.BlockSpec((TOP_K, tm), lambda i: (0, prev(i))),
            pl.BlockSpec((TOP_K, tm), lambda i: (0, prev(i))),
            pl.BlockSpec((1, N_EXPERTS, LANES), lambda i: (prev(i), 0, 0)),
            pl.BlockSpec((N_EXPERTS, LANES), lambda i: (0, 0)),
        ],
        out_shape=[
            jax.ShapeDtypeStruct((t, D_MODEL), F32),
            jax.ShapeDtypeStruct((t, D_MODEL), BF16),
            jax.ShapeDtypeStruct((TOP_K, t), I32),
            jax.ShapeDtypeStruct((TOP_K, t), F32),
            jax.ShapeDtypeStruct((t // tm, N_EXPERTS, LANES), I32),
            jax.ShapeDtypeStruct((N_EXPERTS, LANES), I32),
        ],
        scratch_shapes=[
            pltpu.VMEM((tm, SGU_WIDTH), BF16),
            pltpu.VMEM((N_EXPERTS, LANES), F32),
        ] + [pltpu.VMEM((tm, D_MODEL), BF16)] * 2,
        compiler_params=_params(48, ("arbitrary",)),
        name="post",
    )(attn2, z2, z2, gates2, gates2, x2, wa, wb, wo, lng, lnb, wcat, bcat, n2g, wrh, wrl, br)


def _route_rows(e_ref, seen, tm):
    e = e_ref[...]
    eio = lax.broadcasted_iota(I32, (N_EXPERTS, tm), 0)
    onehots = [eio == e[k:k + 1, :] for k in range(TOP_K)]
    sel = jnp.zeros((N_EXPERTS, tm), F32)
    for oh in onehots:
        sel = jnp.where(oh, 1.0, sel)
    ti = lax.broadcasted_iota(I32, (tm, tm), 0)
    tj = lax.broadcasted_iota(I32, (tm, tm), 1)
    before = jnp.where(ti < tj, 1.0, 0.0).astype(BF16)
    slot = _dot(sel.astype(BF16), before)
    cnt = jnp.sum(sel, axis=1, keepdims=True)
    head = seen - SUBLANES * jnp.floor(seen * (1.0 / SUBLANES))
    tiles = jnp.where(cnt > 0.0, jnp.floor((head + cnt + (SUBLANES - 1)) * (1.0 / SUBLANES)), 0.0)
    xi = lax.broadcasted_iota(I32, (N_EXPERTS, N_EXPERTS), 0)
    xj = lax.broadcasted_iota(I32, (N_EXPERTS, N_EXPERTS), 1)
    lower = jnp.where(xj < xi, 1.0, 0.0).astype(BF16)
    off = _dot(lower, jnp.broadcast_to(tiles, (N_EXPERTS, LANES)).astype(BF16))[:, 0:1] * float(SUBLANES)
    row = off + head + slot
    return [jnp.sum(jnp.where(oh, row, 0.0), axis=0, keepdims=True) for oh in onehots], cnt


def _pick(rows, values, default):
    tm = rows[0].shape[1]
    rio = lax.broadcasted_iota(I32, (ROUTE_ROWS, tm), 0).astype(F32)
    out = default
    for r, v in zip(rows, values):
        out = jnp.where(rio == r, v, out)
    return out


def _wait_rows(buf, hbm_ref, sem, n):
    n = pl.multiple_of(n, SUBLANES)

    @pl.when(n > 0)
    def _():
        pltpu.make_async_copy(buf.at[pl.ds(0, n), :], hbm_ref.at[pl.ds(0, n), :], sem).wait()


def _dispatch_kernel(start_ref, off_ref, cnt_ref, tot_ref, keep_ref, za_ref, zb_ref, hf_ref, e_ref, g_ref,
                     xs_ref, rows_ref, obuf, zbuf, seen_s, tail_s, sem, semz):
    tm = hf_ref.shape[0]
    i = pl.program_id(0)
    last = pl.num_programs(0) - 1
    cur = lax.rem(i, 2)

    @pl.when(i == 0)
    def _():
        seen_s[...] = jnp.zeros_like(seen_s)
        tail_s[...] = jnp.zeros_like(tail_s)

    rows, cnt = _route_rows(e_ref, seen_s[:, 0:1], tm)
    seen_s[...] = seen_s[...] + cnt
    for k in range(TOP_K):
        rows_ref[k:k + 1, :] = rows[k]
    g = g_ref[...]
    pt = _pick(rows, [1.0] * TOP_K, 0.0).astype(BF16)
    gate = jnp.sum(_pick(rows, [g[k:k + 1, :] for k in range(TOP_K)], 0.0), axis=1, keepdims=True)
    obuf[cur, :, 0:D_MODEL] = _dot(pt, hf_ref[...])
    obuf[cur, :, D_MODEL:XS_WIDTH] = jnp.broadcast_to(gate, (ROUTE_ROWS, LANES))

    for x in range(N_EXPERTS):
        n = cnt_ref[i * N_EXPERTS + x]

        @pl.when(n > 0)
        def _(x=x, n=n):
            first = pl.multiple_of(off_ref[i * N_EXPERTS + x], SUBLANES)
            obuf[cur, pl.ds(first, SUBLANES), :] = obuf[cur, pl.ds(first, SUBLANES), :] + tail_s[x]
            final = pl.multiple_of(off_ref[i * N_EXPERTS + x] + n - SUBLANES, SUBLANES)
            tail_s[x] = obuf[cur, pl.ds(final, SUBLANES), :] * keep_ref[i * N_EXPERTS + x].astype(F32)

    @pl.when(i >= 1)
    def _():
        _wait_rows(obuf.at[1 - cur], xs_ref, sem.at[1 - cur], tot_ref[jnp.maximum(i - 1, 0)])

    for x in range(N_EXPERTS):
        n = pl.multiple_of(cnt_ref[i * N_EXPERTS + x], SUBLANES)

        @pl.when(n > 0)
        def _(x=x, n=n):
            src = pl.multiple_of(off_ref[i * N_EXPERTS + x], SUBLANES)
            dst = pl.multiple_of(start_ref[i * N_EXPERTS + x], SUBLANES)
            pltpu.make_async_copy(obuf.at[cur, pl.ds(src, n), :], xs_ref.at[pl.ds(dst, n), :], sem.at[cur]).start()

    @pl.when(i == last)
    def _():
        _wait_rows(obuf.at[cur], xs_ref, sem.at[cur], tot_ref[i])
        zbuf[...] = jnp.zeros_like(zbuf)

        def zero_region(z, act):
            n = zb_ref[z] - za_ref[z]
            big = n // EXPERT_TILE
            rest = pl.multiple_of(n - big * EXPERT_TILE, SUBLANES)

            def piece(row, nrows):
                return pltpu.make_async_copy(zbuf.at[pl.ds(0, nrows), :],
                                             xs_ref.at[pl.ds(pl.multiple_of(row, SUBLANES), nrows), :], semz)

            lax.fori_loop(0, big, lambda m, c: (act(piece(za_ref[z] + m * EXPERT_TILE, EXPERT_TILE)), c)[1], 0)

            @pl.when(rest > 0)
            def _():
                act(piece(za_ref[z] + big * EXPERT_TILE, rest))

        for z in range(N_EXPERTS + 1):
            zero_region(z, lambda cp: cp.start())
        for z in range(N_EXPERTS + 1):
            zero_region(z, lambda cp: cp.wait())


def _dispatch(start, off, cnt, tot, keep, za, zb, hf, e4, g4, cap):
    t = hf.shape[0]
    tm = TM_ROUTE
    tok = lambda i, *_: (0, i)
    return pl.pallas_call(
        _dispatch_kernel,
        grid_spec=pltpu.PrefetchScalarGridSpec(
            num_scalar_prefetch=7,
            grid=(t // tm,),
            in_specs=[
                pl.BlockSpec((tm, D_MODEL), lambda i, *_: (i, 0)),
                pl.BlockSpec((TOP_K, tm), tok),
                pl.BlockSpec((TOP_K, tm), tok),
            ],
            out_specs=[pl.BlockSpec(memory_space=pl.ANY), pl.BlockSpec((TOP_K, tm), tok)],
            scratch_shapes=[
                pltpu.VMEM((2, ROUTE_ROWS, XS_WIDTH), F32),
                pltpu.VMEM((EXPERT_TILE, XS_WIDTH), F32),
                pltpu.VMEM((N_EXPERTS, LANES), F32),
                pltpu.VMEM((N_EXPERTS, SUBLANES, XS_WIDTH), F32),
                pltpu.SemaphoreType.DMA((2,)),
                pltpu.SemaphoreType.DMA(()),
            ],
        ),
        out_shape=[jax.ShapeDtypeStruct((cap, XS_WIDTH), F32), jax.ShapeDtypeStruct((TOP_K, t), F32)],
        compiler_params=_params(48, ("arbitrary",)),
        name="dispatch",
    )(start, off, cnt, tot, keep, za, zb, hf, e4, g4)


def _expert_kernel(te_ref, nu_ref, xs_ref, wgu_ref, bgu_ref, wdn_ref, bdn_ref, ys_ref, wgu_s, wdn_s):
    i = pl.program_id(0)

    @pl.when(i < nu_ref[0])
    def _():
        @pl.when((i == 0) | (te_ref[i] != te_ref[jnp.maximum(i - 1, 0)]))
        def _():
            wgu_s[...] = wgu_ref[0].astype(BF16)
            wdn_s[...] = wdn_ref[0].astype(BF16)

        x = xs_ref[:, 0:D_MODEL].astype(BF16)
        route_gate = xs_ref[:, D_MODEL:D_MODEL + 1]
        gu = _dot(x, wgu_s[...]) + bgu_ref[0]
        gate = jnp.minimum(gu[:, :D_EXPERT], SWIGLU_LIMIT)
        up = jnp.clip(gu[:, D_EXPERT:], -SWIGLU_LIMIT, SWIGLU_LIMIT)
        act = (up + 1.0) * (gate * jax.nn.sigmoid(SWIGLU_ALPHA * gate))
        ys_ref[...] = (_dot(act.astype(BF16), wdn_s[...]) + bdn_ref[0]) * route_gate

    @pl.when(i >= nu_ref[0])
    def _():
        ys_ref[...] = jnp.zeros_like(ys_ref)


def _experts(tile_expert, n_used, xs, wgu, bgu, wdn, bdn):
    cap = xs.shape[0]
    n_tiles = cap // EXPERT_TILE
    row = lambda i, te, nu: (jnp.minimum(i, nu[0] - 1), 0)
    by_expert = lambda i, te, nu: (te[i], 0, 0)
    return pl.pallas_call(
        _expert_kernel,
        grid_spec=pltpu.PrefetchScalarGridSpec(
            num_scalar_prefetch=2,
            grid=(n_tiles,),
            in_specs=[
                pl.BlockSpec((EXPERT_TILE, XS_WIDTH), row),
                pl.BlockSpec((1, D_MODEL, 2 * D_EXPERT), by_expert),
                pl.BlockSpec((1, 1, 2 * D_EXPERT), by_expert),
                pl.BlockSpec((1, D_EXPERT, D_MODEL), by_expert),
                pl.BlockSpec((1, 1, D_MODEL), by_expert),
            ],
            out_specs=pl.BlockSpec((EXPERT_TILE, D_MODEL), lambda i, te, nu: (i, 0)),
            scratch_shapes=[
                pltpu.VMEM((D_MODEL, 2 * D_EXPERT), BF16),
                pltpu.VMEM((D_EXPERT, D_MODEL), BF16),
            ],
        ),
        out_shape=jax.ShapeDtypeStruct((cap, D_MODEL), F32),
        compiler_params=_params(56, ("arbitrary",)),
        name="experts",
    )(tile_expert, n_used, xs, wgu, bgu, wdn, bdn)


def _combine_kernel(start_ref, off_ref, cnt_ref, tot_ref, x1_ref, rows_ref, ys_ref, o_ref, sbuf, sem):
    i = pl.program_id(0)
    last = pl.num_programs(0) - 1
    cur = lax.rem(i, 2)

    def fetch(step, b):
        for x in range(N_EXPERTS):
            n = pl.multiple_of(cnt_ref[step * N_EXPERTS + x], SUBLANES)

            @pl.when(n > 0)
            def _(x=x, n=n):
                src = pl.multiple_of(start_ref[step * N_EXPERTS + x], SUBLANES)
                dst = pl.multiple_of(off_ref[step * N_EXPERTS + x], SUBLANES)
                pltpu.make_async_copy(ys_ref.at[pl.ds(src, n), :], sbuf.at[b, pl.ds(dst, n), :], sem.at[b]).start()

    @pl.when(i == 0)
    def _():
        sbuf[...] = jnp.zeros_like(sbuf)
        fetch(i, cur)

    @pl.when(i < last)
    def _():
        fetch(i + 1, 1 - cur)

    rows = [rows_ref[k:k + 1, :] for k in range(TOP_K)]
    pt = _pick(rows, [1.0] * TOP_K, 0.0).astype(BF16)
    _wait_rows(sbuf.at[cur], ys_ref, sem.at[cur], tot_ref[i])
    picked = lax.dot_general(pt, sbuf[cur].astype(BF16), _TN, preferred_element_type=F32)
    o_ref[...] = x1_ref[...] + picked


def _combine(start, off, cnt, tot, x1, rows4, ys):
    t = x1.shape[0]
    tm = TM_ROUTE
    return pl.pallas_call(
        _combine_kernel,
        grid_spec=pltpu.PrefetchScalarGridSpec(
            num_scalar_prefetch=4,
            grid=(t // tm,),
            in_specs=[
                pl.BlockSpec((tm, D_MODEL), lambda i, *_: (i, 0)),
                pl.BlockSpec((TOP_K, tm), lambda i, *_: (0, i)),
                pl.BlockSpec(memory_space=pl.ANY),
            ],
            out_specs=pl.BlockSpec((tm, D_MODEL), lambda i, *_: (i, 0)),
            scratch_shapes=[
                pltpu.VMEM((2, ROUTE_ROWS, D_MODEL), F32),
                pltpu.SemaphoreType.DMA((2,)),
            ],
        ),
        out_shape=jax.ShapeDtypeStruct((t, D_MODEL), F32),
        compiler_params=_params(48, ("arbitrary",)),
        name="combine",
    )(start, off, cnt, tot, x1, rows4, ys)


def _layer(x, positions, norm1_g, w_in, q_norm_g, k_norm_g, sgu_ln_g, sgu_ln_b, w_spatial, b_spatial,
           w_branch_a, w_branch_b, w_out, norm2_g, w_router, b_router, w_gate_up, b_gate_up, w_down, b_down):
    b, seq, _ = x.shape
    t = b * seq
    x2 = x.reshape(t, D_MODEL)

    inv_freq = ROPE_THETA ** (-jnp.arange(ROT_HALF, dtype=F32) / ROT_HALF)
    invf = jnp.tile(inv_freq, LANES // ROT_HALF).reshape(1, LANES)
    dim_of_lane = [(l // (2 * ROT_HALF)) * ROT_HALF + l % ROT_HALF for l in range(LANES)]
    qg = q_norm_g[jnp.array(dim_of_lane)].reshape(1, LANES)
    kg = k_norm_g[jnp.array(dim_of_lane)].reshape(1, LANES)
    w_bf = w_in.astype(BF16)
    qk = w_bf[:, :2 * ATTN_WIDTH].reshape(D_MODEL, 2 * ATTN_WIDTH // LANES, 2, 2, ROT_HALF)
    qk = qk.transpose(0, 1, 3, 2, 4).reshape(D_MODEL, 2 * ATTN_WIDTH)
    w_in_p = jnp.concatenate([qk, w_bf[:, 2 * ATTN_WIDTH:]], axis=1)
    n_pairs = SGU_WIDTH // LANES
    wcat = w_spatial.reshape(n_pairs, 2, SGU_CHUNK, SGU_CHUNK).transpose(0, 2, 1, 3).reshape(n_pairs, SGU_CHUNK, 2 * SGU_CHUNK)
    bcat = jnp.repeat(b_spatial.reshape(n_pairs, 2, SGU_CHUNK).transpose(0, 2, 1), SGU_GROUP_DIM, axis=2)
    wr_t = w_router.T
    wr_hi = wr_t.astype(BF16)
    wr_lo = (wr_t - wr_hi.astype(F32)).astype(BF16)

    *qkv_groups, z, gates = _inproj(x2, norm1_g.reshape(1, D_MODEL), w_in_p, positions.reshape(-1, LANES // ROT_HALF), invf, qg, kg, b, seq)
    attn = _attention(qkv_groups)
    x1, hf, e4, g4, cst, counts = _post(
        attn.reshape(t, ATTN_WIDTH), z, gates, x2,
        w_branch_a.astype(BF16), w_branch_b.astype(BF16), w_out.astype(BF16),
        sgu_ln_g.reshape(1, SGU_WIDTH), sgu_ln_b.reshape(1, SGU_WIDTH), wcat.astype(BF16), bcat,
        norm2_g.reshape(1, D_MODEL), wr_hi, wr_lo, b_router.reshape(N_EXPERTS, 1))

    steps_per_tile = TM_POST // TM_ROUTE
    n_steps = t // TM_ROUTE
    before = cst[:, :, :steps_per_tile].transpose(0, 2, 1).reshape(n_steps, N_EXPERTS)
    total = counts[:, 0]
    cnt = jnp.concatenate([before[1:], total[None, :]], axis=0) - before
    head = before % SUBLANES
    moved = jnp.where(cnt > 0, (head + cnt + SUBLANES - 1) // SUBLANES * SUBLANES, 0)
    keep = ((head + cnt) % SUBLANES != 0).astype(I32).reshape(-1)
    off = (jnp.cumsum(moved, axis=1) - moved).astype(I32).reshape(-1)
    tot = jnp.sum(moved, axis=1).astype(I32)
    region = (total + EXPERT_TILE - 1) // EXPERT_TILE * EXPERT_TILE
    rend = jnp.cumsum(region)
    rstart = rend - region
    start = (rstart[None, :] + before - head).astype(I32).reshape(-1)
    bound = t * TOP_K + N_EXPERTS * (EXPERT_TILE - 1)
    cap = (bound + EXPERT_TILE - 1) // EXPERT_TILE * EXPERT_TILE
    n_tiles = cap // EXPERT_TILE
    tile_expert = jnp.sum(jnp.arange(n_tiles, dtype=I32)[:, None] * EXPERT_TILE >= rend[None, :], axis=1)
    tile_expert = jnp.minimum(tile_expert, N_EXPERTS - 1).astype(I32)
    n_used = (rend[-1:] // EXPERT_TILE).astype(I32)
    filled = (total + SUBLANES - 1) // SUBLANES * SUBLANES
    za = jnp.concatenate([rstart + filled, rend[-1:]]).astype(I32)
    zb = jnp.concatenate([rend, jnp.full((1,), cap, rend.dtype)]).astype(I32)
    moved = moved.astype(I32).reshape(-1)

    xs, rows4 = _dispatch(start, off, moved, tot, keep, za, zb, hf, e4, g4, cap)
    ys = _experts(tile_expert, n_used, xs,
                  w_gate_up, b_gate_up.reshape(N_EXPERTS, 1, 2 * D_EXPERT),
                  w_down, b_down.reshape(N_EXPERTS, 1, D_MODEL))
    out = _combine(start, off, moved, tot, x1, rows4, ys)
    return out.reshape(b, seq, D_MODEL)


def kernel(x, positions, norm1_g, w_in, q_norm_g, k_norm_g, sgu_ln_g, sgu_ln_b, w_spatial, b_spatial, w_branch_a, w_branch_b, w_out, norm2_g, w_router, b_router, w_gate_up, b_gate_up, w_down, b_down):
    for layer in range(norm1_g.shape[0]):
        x = _layer(x, positions, norm1_g[layer], w_in[layer], q_norm_g[layer], k_norm_g[layer],
                   sgu_ln_g[layer], sgu_ln_b[layer], w_spatial[layer], b_spatial[layer],
                   w_branch_a[layer], w_branch_b[layer], w_out[layer], norm2_g[layer],
                   w_router[layer], b_router[layer], w_gate_up[layer], b_gate_up[layer],
                   w_down[layer], b_down[layer])
    return x
```

```python
import functools

import jax
import jax.numpy as jnp
from jax import lax
from jax.experimental import pallas as pl
from jax.experimental.pallas import tpu as pltpu

F32 = jnp.float32
BF16 = jnp.bfloat16
I32 = jnp.int32

D_MODEL = 1024
HEAD_DIM = 64
ROT_HALF = HEAD_DIM // 2
LOG2E = 1.4426950408889634
LN2 = 0.6931471805599453
Q_SCALE = HEAD_DIM ** -0.5 * LOG2E
ATTN_GROUPS = ((128, 1), (512, 4), (2048, 16))
N_GROUPS = len(ATTN_GROUPS)
GROUP_WIDTH = 256
ATTN_WIDTH = N_GROUPS * GROUP_WIDTH
BLK = 128
ROPE_THETA = 10000.0
SGU_CHUNK = 128
SGU_GROUP_DIM = 64
SGU_WIDTH = 768
GATE_WIDTH = 2 * D_MODEL
N_EXPERTS = 32
TOP_K = 4
D_EXPERT = D_MODEL
SWIGLU_LIMIT = 7.0
SWIGLU_ALPHA = 1.702
EPS = 1e-6
NEG_INF = -1e30
LANES = 128

TM_INPROJ = 512
TM_POST = 512
TM_ROUTE = 256
SUBLANES = 8
ROUTE_ROWS = TM_ROUTE * TOP_K + N_EXPERTS * 2 * SUBLANES
EXPERT_TILE = 512
XS_WIDTH = D_MODEL + LANES
ROW_CHUNK = 256

_NT = (((1,), (1,)), ((), ()))
_TN = (((0,), (0,)), ((), ()))


def _dot(a, b):
    return jnp.dot(a, b, preferred_element_type=F32)


def _dot_nt(a, b):
    return lax.dot_general(a, b, _NT, preferred_element_type=F32)


def _params(vmem_mb, sem):
    return pltpu.CompilerParams(dimension_semantics=sem, vmem_limit_bytes=vmem_mb * 1024 * 1024)


def _split3(x):
    a = x.astype(BF16)
    r = x - a.astype(F32)
    b = r.astype(BF16)
    return a, b, (r - b.astype(F32)).astype(BF16)


def _inproj_kernel(x_ref, g_ref, w_ref, pos_ref, invf_ref, qg_ref, kg_ref,
                   o0_ref, o1_ref, o2_ref, oz_ref, og_ref, raw_a, raw_b, stage_s, cos_s, sin_s):
    step = pl.program_id(0)

    @pl.when(step == 0)
    def _():
        raw_b[...] = jnp.zeros_like(raw_b)

    args = (x_ref, g_ref, w_ref, pos_ref, invf_ref, qg_ref, kg_ref, o0_ref, o1_ref, o2_ref, oz_ref, og_ref)

    @pl.when(lax.rem(step, 2) == 0)
    def _():
        _inproj_step(*args, raw_a, raw_b, stage_s, cos_s, sin_s)

    @pl.when(lax.rem(step, 2) == 1)
    def _():
        _inproj_step(*args, raw_b, raw_a, stage_s, cos_s, sin_s)


def _inproj_step(x_ref, g_ref, w_ref, pos_ref, invf_ref, qg_ref, kg_ref,
                 o0_ref, o1_ref, o2_ref, oz_ref, og_ref, raw_w, raw_r, stage_s, cos_s, sin_s):
    tm = x_ref.shape[0]
    x = x_ref[...]
    y = x * lax.rsqrt(jnp.mean(x * x, axis=-1, keepdims=True) + EPS)
    h = (y * g_ref[...]).astype(BF16)

    def project(lo):
        p = _dot(h, w_ref[:, lo:lo + GROUP_WIDTH])
        zc = 3 * ATTN_WIDTH
        gc = zc + 2 * SGU_WIDTH
        if lo < zc:
            raw_w[:, lo:lo + GROUP_WIDTH] = p
        elif lo < gc:
            oz_ref[:, lo - zc:lo - zc + GROUP_WIDTH] = p.astype(BF16)
        else:
            og_ref[:, lo - gc:lo - gc + GROUP_WIDTH] = p.astype(BF16)

    def rotary_tables():
        per_row = LANES // ROT_HALF
        lane = lax.broadcasted_iota(I32, (tm // per_row, LANES), 1)
        p4 = pos_ref[...].astype(F32)
        posd = p4[:, per_row - 1:per_row]
        for j in range(per_row - 2, -1, -1):
            posd = jnp.where(lane < (j + 1) * ROT_HALF, p4[:, j:j + 1], posd)
        ang = posd * invf_ref[...]
        cparts = _split3(jnp.cos(ang))
        sparts = _split3(jnp.sin(ang))
        src = lax.broadcasted_iota(I32, (LANES, LANES), 0)
        dst = lax.broadcasted_iota(I32, (LANES, LANES), 1)
        for j in range(per_row):
            pick = src == j * ROT_HALF + dst % ROT_HALF
            spread = jnp.where(pick, 1.0, 0.0).astype(BF16)
            spread_neg = jnp.where(pick, jnp.where(dst < 2 * ROT_HALF, -1.0, 1.0), 0.0).astype(BF16)
            cos_s[pl.ds(j, tm // per_row, stride=per_row), :] = sum(_dot(p, spread) for p in cparts)
            sin_s[pl.ds(j, tm // per_row, stride=per_row), :] = sum(_dot(p, spread_neg) for p in sparts)

    li = (lax.broadcasted_iota(I32, (LANES, LANES), 0) // ROT_HALF) % 2
    lj = (lax.broadcasted_iota(I32, (LANES, LANES), 1) // ROT_HALF) % 2
    head_sum = jnp.where(li == lj, 1.0, 0.0).astype(BF16)
    outs = (o0_ref, o1_ref, o2_ref)

    def finish(gi, d, ti, mode, gain_ref, c, stage):
        col = ti * ATTN_WIDTH + gi * GROUP_WIDTH
        t = raw_r[:, col + c * LANES:col + (c + 1) * LANES]
        if mode != "v":
            ss = t * t
            hi = ss.astype(BF16)
            lo = (ss - hi.astype(F32)).astype(BF16)
            ms = (_dot(hi, head_sum) + _dot(lo, head_sum)) * (1.0 / HEAD_DIM)
            t = t * lax.rsqrt(ms + EPS) * gain_ref[...]
            t = t * cos_s[...] + pltpu.roll(t, 2 * ROT_HALF, 1) * sin_s[...]
            if mode == "q":
                t = t * Q_SCALE
        lanes = slice(ti * GROUP_WIDTH + c * LANES, ti * GROUP_WIDTH + (c + 1) * LANES)
        if d == 1:
            outs[gi][0, 0, :, lanes] = t.astype(BF16)
        else:
            stage_s[stage] = t
            for r in range(d):
                outs[gi][0, r, :, lanes] = stage_s[stage, pl.ds(r, tm // d, stride=d), :].astype(BF16)

    pieces = [rotary_tables]
    stage = 0
    for gi, (_, d) in enumerate(ATTN_GROUPS):
        for ti, (mode, gain_ref) in enumerate((("q", qg_ref), ("k", kg_ref), ("v", None))):
            for c in range(GROUP_WIDTH // LANES):
                pieces.append(functools.partial(finish, gi, d, ti, mode, gain_ref, c, stage))
                stage += d > 1

    chunks = list(range(0, w_ref.shape[1], GROUP_WIDTH))
    for k in range(max(len(chunks), len(pieces))):
        if k < len(chunks):
            project(chunks[k])
        if k < len(pieces):
            pieces[k]()


def _inproj(x2, g, w, pos2, invf, qg, kg, b, seq):
    t = x2.shape[0]
    n = w.shape[1]
    tm = TM_INPROJ
    nj = seq // tm
    n_tiles = b * nj
    n_stage = sum(1 for _, d in ATTN_GROUPS if d > 1) * 3 * (GROUP_WIDTH // LANES)
    proj_row = lambda s: (jnp.minimum(s, n_tiles - 1), 0)
    done_row = lambda s: (jnp.maximum(s - 1, 0), 0)
    done_blk = lambda s: (jnp.maximum(s - 1, 0) // nj, 0, jnp.maximum(s - 1, 0) % nj, 0)
    const = lambda s: (0, 0)
    per_row = LANES // ROT_HALF
    return pl.pallas_call(
        _inproj_kernel,
        grid=(n_tiles + 1,),
        in_specs=[
            pl.BlockSpec((tm, D_MODEL), proj_row),
            pl.BlockSpec((1, D_MODEL), const),
            pl.BlockSpec((D_MODEL, n), const, pipeline_mode=pl.Buffered(1)),
            pl.BlockSpec((tm // per_row, per_row), done_row),
            pl.BlockSpec((1, LANES), const),
            pl.BlockSpec((1, LANES), const),
            pl.BlockSpec((1, LANES), const),
        ],
        out_specs=[pl.BlockSpec((1, d, tm // d, ATTN_WIDTH), done_blk) for _, d in ATTN_GROUPS] + [
            pl.BlockSpec((tm, 2 * SGU_WIDTH), proj_row),
            pl.BlockSpec((tm, GATE_WIDTH), proj_row),
        ],
        out_shape=[jax.ShapeDtypeStruct((b, d, seq // d, ATTN_WIDTH), BF16) for _, d in ATTN_GROUPS] + [
            jax.ShapeDtypeStruct((t, 2 * SGU_WIDTH), BF16),
            jax.ShapeDtypeStruct((t, GATE_WIDTH), BF16),
        ],
        scratch_shapes=[
            pltpu.VMEM((tm, 3 * ATTN_WIDTH), F32),
            pltpu.VMEM((tm, 3 * ATTN_WIDTH), F32),
            pltpu.VMEM((n_stage, tm, LANES), F32),
            pltpu.VMEM((tm, LANES), F32),
            pltpu.VMEM((tm, LANES), F32),
        ],
        compiler_params=_params(56, ("arbitrary",)),
        name="inproj",
    )(x2, g, w, pos2, invf, qg, kg)


def _attn_blocks(qkv_ref, ores_s, lres_s, seq, d):
    sub = seq // d
    nb = sub // BLK
    win = 2 * BLK if nb > 1 else BLK
    log2d = d.bit_length() - 1
    lane = lax.broadcasted_iota(I32, (BLK, GROUP_WIDTH), 1)
    q_head = 2 * (lane // LANES) + (lane // ROT_HALF) % 2
    qi = lax.broadcasted_iota(I32, (BLK, win), 0)
    kj = lax.broadcasted_iota(I32, (BLK, win), 1)
    rel = qi - kj

    def body(idx, carry):
        r = idx & (d - 1)
        n = idx >> log2d
        row0 = pl.multiple_of(r * sub + n * BLK, BLK)
        if nb > 1:
            kn = jnp.maximum(n - 1, 0)
            k0 = pl.multiple_of(r * sub + kn * BLK, BLK)
            dist = rel + (n - kn) * BLK
        else:
            k0 = row0
            dist = rel
        valid = lax.bitcast_convert_type(dist, jnp.uint32) <= jnp.uint32(BLK)
        q = qkv_ref[0, pl.ds(row0, BLK), 0:GROUP_WIDTH]
        kw = qkv_ref[0, pl.ds(k0, win), GROUP_WIDTH:2 * GROUP_WIDTH]
        vw = qkv_ref[0, pl.ds(k0, win), 2 * GROUP_WIDTH:3 * GROUP_WIDTH]
        zero = jnp.zeros_like(q)
        qm = jnp.concatenate([jnp.where(q_head == h, q, zero) for h in range(4)], axis=0)
        s = _dot_nt(qm, kw)
        ps, ms, ls = [], [], []
        for h in range(4):
            sh = jnp.where(valid, s[h * BLK:(h + 1) * BLK], NEG_INF)
            m = jnp.max(sh, axis=-1, keepdims=True)
            p = jnp.exp2(sh - m)
            ls.append(jnp.sum(p, axis=-1, keepdims=True))
            ms.append(m)
            ps.append(p.astype(BF16))
        pv = _dot(jnp.concatenate(ps, axis=0), vw)
        for h in range(4):
            lanes = slice(h * HEAD_DIM, (h + 1) * HEAD_DIM)
            ores_s[pl.ds(row0, BLK), lanes] = pv[h * BLK:(h + 1) * BLK, lanes] * (1.0 / ls[h])
            lres_s[pl.ds(row0, BLK), lanes] = jnp.broadcast_to((ms[h] + jnp.log2(ls[h])) * LN2, (BLK, HEAD_DIM))
        return carry

    lax.fori_loop(0, seq // BLK, body, 0, unroll=8)


def _attn_kernel(g0_ref, g1_ref, g2_ref, o_ref, ores_s, lres_s, onat_s, lnat_s, *, seq):
    for gi, ((_, d), qkv_ref) in enumerate(zip(ATTN_GROUPS, (g0_ref, g1_ref, g2_ref))):
        _attn_blocks(qkv_ref, ores_s, lres_s, seq, d)
        sub = seq // d
        for r in range(d):
            for c in range(GROUP_WIDTH // LANES):
                t = gi * (GROUP_WIDTH // LANES) + c
                onat_s[t, pl.ds(r, sub, stride=d), :] = ores_s[r * sub:(r + 1) * sub, c * LANES:(c + 1) * LANES]
                lnat_s[t, pl.ds(r, sub, stride=d), :] = lres_s[r * sub:(r + 1) * sub, c * LANES:(c + 1) * LANES]

    tiles = GROUP_WIDTH // LANES

    def body(i, carry):
        r0 = pl.multiple_of(i * ROW_CHUNK, ROW_CHUNK)
        for c in range(tiles):
            ls = [lnat_s[gi * tiles + c, pl.ds(r0, ROW_CHUNK), :] for gi in range(N_GROUPS)]
            m = jnp.maximum(jnp.maximum(ls[0], ls[1]), ls[2])
            es = [jnp.exp(l - m) for l in ls]
            inv = 1.0 / (es[0] + es[1] + es[2])
            for gi in range(N_GROUPS):
                t = gi * tiles + c
                o_ref[0, pl.ds(r0, ROW_CHUNK), t * LANES:(t + 1) * LANES] = (
                    onat_s[t, pl.ds(r0, ROW_CHUNK), :] * (es[gi] * inv)).astype(BF16)
        return carry

    lax.fori_loop(0, seq // ROW_CHUNK, body, 0, unroll=2)


def _attention(qkv_groups):
    b = qkv_groups[0].shape[0]
    seq = qkv_groups[0].shape[1] * qkv_groups[0].shape[2]
    tiles = GROUP_WIDTH // LANES
    blk = pl.BlockSpec((1, seq, ATTN_WIDTH), lambda i: (i, 0, 0))
    return pl.pallas_call(
        functools.partial(_attn_kernel, seq=seq),
        grid=(b,),
        in_specs=[blk] * N_GROUPS,
        out_specs=blk,
        out_shape=jax.ShapeDtypeStruct((b, seq, ATTN_WIDTH), BF16),
        scratch_shapes=[
            pltpu.VMEM((seq, GROUP_WIDTH), F32),
            pltpu.VMEM((seq, GROUP_WIDTH), F32),
            pltpu.VMEM((N_GROUPS * tiles, seq, LANES), F32),
            pltpu.VMEM((N_GROUPS * tiles, seq, LANES), F32),
        ],
        compiler_params=_params(48, ("arbitrary",)),
        name="attention",
    )(*[a.reshape(b, seq, ATTN_WIDTH) for a in qkv_groups])


def _gelu(x):
    return 0.5 * x * (1.0 + lax.erf(x * 0.7071067811865476))


def _post_kernel(attn_ref, u_ref, vz_ref, ga_ref, gb_ref, x_ref, wa_ref, wb_ref, wo_ref,
                 lng_ref, lnb_ref, wcat_ref, bcat_ref, n2g_ref, wrh_ref, wrl_ref, br_ref,
                 x1_ref, hf_ref, e_ref, g_ref, cst_ref, cnt_ref, sgu_s, carry_s, merged_a, merged_b):
    step = pl.program_id(0)

    @pl.when(step == 0)
    def _():
        carry_s[...] = jnp.zeros_like(carry_s)
        merged_b[...] = jnp.zeros_like(merged_b)

    args = (attn_ref, u_ref, vz_ref, ga_ref, gb_ref, x_ref, wa_ref, wb_ref, wo_ref, lng_ref, lnb_ref, wcat_ref,
            bcat_ref, n2g_ref, wrh_ref, wrl_ref, br_ref, x1_ref, hf_ref, e_ref, g_ref, cst_ref, cnt_ref, sgu_s, carry_s)

    @pl.when(lax.rem(step, 2) == 0)
    def _():
        _post_step(*args, merged_a, merged_b)

    @pl.when(lax.rem(step, 2) == 1)
    def _():
        _post_step(*args, merged_b, merged_a)


def _post_step(attn_ref, u_ref, vz_ref, ga_ref, gb_ref, x_ref, wa_ref, wb_ref, wo_ref,
               lng_ref, lnb_ref, wcat_ref, bcat_ref, n2g_ref, wrh_ref, wrl_ref, br_ref,
               x1_ref, hf_ref, e_ref, g_ref, cst_ref, cnt_ref, sgu_s, carry_s, merged_w, merged_r):
    tm = x_ref.shape[0]
    counted = jnp.where(pl.program_id(0) > 0, 1.0, 0.0)

    lane = lax.broadcasted_iota(I32, (SGU_CHUNK, LANES), 1)
    low = lane < SGU_GROUP_DIM
    trow = lax.broadcasted_iota(I32, (SGU_CHUNK, 2 * SGU_CHUNK), 0)
    tcol = lax.broadcasted_iota(I32, (SGU_CHUNK, 2 * SGU_CHUNK), 1) % SGU_CHUNK
    causal = tcol <= trow

    def chunk(c):
        r0 = c * SGU_CHUNK
        u = _gelu(u_ref[pl.ds(r0, SGU_CHUNK), :].astype(F32))
        v = _gelu(vz_ref[pl.ds(r0, SGU_CHUNK), :].astype(F32))
        mu = jnp.mean(v, axis=-1, keepdims=True)
        vc = v - mu
        vn = vc * lax.rsqrt(jnp.mean(vc * vc, axis=-1, keepdims=True) + EPS)
        vn = (vn * lng_ref[...] + lnb_ref[...]).astype(BF16)
        zero = jnp.zeros((SGU_CHUNK, LANES), BF16)
        for j in range(SGU_WIDTH // LANES):
            vt = vn[:, j * LANES:(j + 1) * LANES]
            rhs = jnp.concatenate([jnp.where(low, vt, zero), jnp.where(low, zero, vt)], axis=0)
            wj = jnp.where(causal, wcat_ref[j], jnp.zeros((), BF16))
            mixed = _dot(wj, rhs) + bcat_ref[j]
            sgu_s[pl.ds(r0, SGU_CHUNK), j * LANES:(j + 1) * LANES] = (u[:, j * LANES:(j + 1) * LANES] * mixed).astype(BF16)

    eio = lax.broadcasted_iota(I32, (N_EXPERTS, tm), 0)
    state = {}
    top_v, onehots = [], []

    def route_logits(hi, lo):
        state["work"] = (_dot_nt(wrh_ref[...], hi) + _dot_nt(wrl_ref[...], hi) + _dot_nt(wrh_ref[...], lo)
                         + br_ref[...])

    def route_pick(k):
        work = state["work"]
        m = jnp.max(work, axis=0, keepdims=True)
        idx = jnp.min(jnp.where(work == m, eio, N_EXPERTS), axis=0, keepdims=True)
        oh = eio == idx
        state["work"] = jnp.where(oh, -jnp.inf, work)
        top_v.append(m)
        onehots.append(oh)
        e_ref[k:k + 1, :] = idx

    def route_finish():
        ex = [jnp.exp(v - top_v[0]) for v in top_v]
        inv = 1.0 / (ex[0] + ex[1] + ex[2] + ex[3])
        for k in range(TOP_K):
            g_ref[k:k + 1, :] = ex[k] * inv
        sel = jnp.zeros((N_EXPERTS, tm), F32)
        for oh in onehots:
            sel = jnp.where(oh, counted, sel)
        lane_e = lax.broadcasted_iota(I32, (N_EXPERTS, LANES), 1)
        run = carry_s[...]
        cst = jnp.zeros((N_EXPERTS, LANES), F32)
        for s in range(tm // TM_ROUTE):
            cst = jnp.where(lane_e == s, run, cst)
            run = run + jnp.sum(sel[:, s * TM_ROUTE:(s + 1) * TM_ROUTE], axis=1, keepdims=True)
        cst_ref[0] = cst.astype(I32)
        carry_s[...] = run
        cnt_ref[...] = run.astype(I32)

    n_parts = tm // SGU_CHUNK
    width = D_MODEL // n_parts
    x1_parts, y_a_parts = [], []
    for c in range(n_parts):
        cols = slice(c * width, (c + 1) * width)
        chunk(c)
        x1_parts.append(x_ref[:, cols] + _dot(merged_r[...], wo_ref[:, cols]))
        y_a_parts.append(_dot(attn_ref[...], wa_ref[:, cols]))
    x1 = jnp.concatenate(x1_parts, axis=1)
    x1_ref[...] = x1
    hf = x1 * lax.rsqrt(jnp.mean(x1 * x1, axis=-1, keepdims=True) + EPS) * n2g_ref[...]
    hi = hf.astype(BF16)
    hf_ref[...] = hi
    route_logits(hi, (hf - hi.astype(F32)).astype(BF16))
    for c in range(n_parts):
        cols = slice(c * width, (c + 1) * width)
        y_b = _dot(sgu_s[...], wb_ref[:, cols])
        merged_w[:, cols] = (jax.nn.sigmoid(ga_ref[:, cols].astype(F32)) * y_a_parts[c]
                             + jax.nn.sigmoid(gb_ref[:, cols].astype(F32)) * y_b).astype(BF16)
        if c < TOP_K:
            route_pick(c)
    for k in range(n_parts, TOP_K):
        route_pick(k)
    route_finish()


def _post(attn2, z2, gates2, x2, wa, wb, wo, lng, lnb, wcat, bcat, n2g, wrh, wrl, br):
    t = x2.shape[0]
    tm = TM_POST
    const = lambda *shape: pl.BlockSpec(shape, lambda i: (0,) * len(shape), pipeline_mode=pl.Buffered(1))
    n_tiles = t // tm
    cur = lambda i: jnp.minimum(i, n_tiles - 1)
    prev = lambda i: jnp.maximum(i - 1, 0)
    return pl.pallas_call(
        _post_kernel,
        grid=(n_tiles + 1,),
        in_specs=[
            pl.BlockSpec((tm, ATTN_WIDTH), lambda i: (cur(i), 0)),
            pl.BlockSpec((tm, SGU_WIDTH), lambda i: (cur(i), 0)),
            pl.BlockSpec((tm, SGU_WIDTH), lambda i: (cur(i), 1)),
            pl.BlockSpec((tm, D_MODEL), lambda i: (cur(i), 0)),
            pl.BlockSpec((tm, D_MODEL), lambda i: (cur(i), 1)),
            pl.BlockSpec((tm, D_MODEL), lambda i: (prev(i), 0)),
            const(ATTN_WIDTH, D_MODEL),
            const(SGU_WIDTH, D_MODEL),
            const(D_MODEL, D_MODEL),
            const(1, SGU_WIDTH),
            const(1, SGU_WIDTH),
            const(SGU_WIDTH // LANES, SGU_CHUNK, 2 * SGU_CHUNK),
            const(SGU_WIDTH // LANES, SGU_CHUNK, LANES),
            const(1, D_MODEL),
            const(N_EXPERTS, D_MODEL),
            const(N_EXPERTS, D_MODEL),
            const(N_EXPERTS, 1),
        ],
        out_specs=[
            pl.BlockSpec((tm, D_MODEL), lambda i: (prev(i), 0)),
            pl.BlockSpec((tm, D_MODEL), lambda i: (prev(i), 0)),
            pl.BlockSpec((TOP_K, tm), lambda i: (0, prev(i))),
            pl.BlockSpec((TOP_K, tm), lambda i: (0, prev(i))),
            pl.BlockSpec((1, N_EXPERTS, LANES), lambda i: (prev(i), 0, 0)),
            pl.BlockSpec((N_EXPERTS, LANES), lambda i: (0, 0)),
        ],
        out_shape=[
            jax.ShapeDtypeStruct((t, D_MODEL), F32),
            jax.ShapeDtypeStruct((t, D_MODEL), BF16),
            jax.ShapeDtypeStruct((TOP_K, t), I32),
            jax.ShapeDtypeStruct((TOP_K, t), F32),
            jax.ShapeDtypeStruct((t // tm, N_EXPERTS, LANES), I32),
            jax.ShapeDtypeStruct((N_EXPERTS, LANES), I32),
        ],
        scratch_shapes=[
            pltpu.VMEM((tm, SGU_WIDTH), BF16),
            pltpu.VMEM((N_EXPERTS, LANES), F32),
        ] + [pltpu.VMEM((tm, D_MODEL), BF16)] * 2,
        compiler_params=_params(48, ("arbitrary",)),
        name="post",
    )(attn2, z2, z2, gates2, gates2, x2, wa, wb, wo, lng, lnb, wcat, bcat, n2g, wrh, wrl, br)


def _route_rows(e_ref, seen, tm):
    e = e_ref[...]
    eio = lax.broadcasted_iota(I32, (N_EXPERTS, tm), 0)
    onehots = [eio == e[k:k + 1, :] for k in range(TOP_K)]
    sel = jnp.zeros((N_EXPERTS, tm), F32)
    for oh in onehots:
        sel = jnp.where(oh, 1.0, sel)
    ti = lax.broadcasted_iota(I32, (tm, tm), 0)
    tj = lax.broadcasted_iota(I32, (tm, tm), 1)
    before = jnp.where(ti < tj, 1.0, 0.0).astype(BF16)
    slot = _dot(sel.astype(BF16), before)
    cnt = jnp.sum(sel, axis=1, keepdims=True)
    head = seen - SUBLANES * jnp.floor(seen * (1.0 / SUBLANES))
    tiles = jnp.where(cnt > 0.0, jnp.floor((head + cnt + (SUBLANES - 1)) * (1.0 / SUBLANES)), 0.0)
    xi = lax.broadcasted_iota(I32, (N_EXPERTS, N_EXPERTS), 0)
    xj = lax.broadcasted_iota(I32, (N_EXPERTS, N_EXPERTS), 1)
    lower = jnp.where(xj < xi, 1.0, 0.0).astype(BF16)
    off = _dot(lower, jnp.broadcast_to(tiles, (N_EXPERTS, LANES)).astype(BF16))[:, 0:1] * float(SUBLANES)
    row = off + head + slot
    return [jnp.sum(jnp.where(oh, row, 0.0), axis=0, keepdims=True) for oh in onehots], cnt


def _pick(rows, values, default):
    tm = rows[0].shape[1]
    rio = lax.broadcasted_iota(I32, (ROUTE_ROWS, tm), 0).astype(F32)
    out = default
    for r, v in zip(rows, values):
        out = jnp.where(rio == r, v, out)
    return out


def _wait_rows(buf, hbm_ref, sem, n):
    n = pl.multiple_of(n, SUBLANES)

    @pl.when(n > 0)
    def _():
        pltpu.make_async_copy(buf.at[pl.ds(0, n), :], hbm_ref.at[pl.ds(0, n), :], sem).wait()


def _dispatch_kernel(start_ref, off_ref, staged_ref, full_ref, tot_ref, flush_ref, za_ref, zb_ref, hf_ref, e_ref, g_ref,
                     xs_ref, rows_ref, obuf, zbuf, seen_s, tail_s, first_s, sem, semz):
    tm = hf_ref.shape[0]
    i = pl.program_id(0)
    last = pl.num_programs(0) - 1
    cur = lax.rem(i, 2)
    spare = ROUTE_ROWS - SUBLANES

    @pl.when(i == 0)
    def _():
        seen_s[...] = jnp.zeros_like(seen_s)
        tail_s[...] = jnp.zeros_like(tail_s)

    @pl.when(i >= 2)
    def _():
        _wait_rows(obuf.at[cur], xs_ref, sem.at[cur], tot_ref[jnp.maximum(i - 2, 0)])

    rows, cnt = _route_rows(e_ref, seen_s[:, 0:1], tm)
    seen_s[...] = seen_s[...] + cnt
    for k in range(TOP_K):
        rows_ref[k:k + 1, :] = rows[k]
    g = g_ref[...]
    pt = _pick(rows, [1.0] * TOP_K, 0.0).astype(BF16)
    gate = jnp.sum(_pick(rows, [g[k:k + 1, :] for k in range(TOP_K)], 0.0), axis=1, keepdims=True)
    obuf[cur, :, 0:D_MODEL] = _dot(pt, hf_ref[...])
    obuf[cur, :, D_MODEL:XS_WIDTH] = jnp.broadcast_to(gate, (ROUTE_ROWS, LANES))

    for x in range(N_EXPERTS):
        k = i * N_EXPERTS + x
        staged, full = staged_ref[k], full_ref[k]
        has = jnp.where(staged > 0, 1.0, 0.0)
        partial = jnp.where(staged > full, 1.0, 0.0)
        alone = jnp.where(full == 0, 1.0, 0.0)
        first = pl.multiple_of(jnp.where(staged > 0, off_ref[k], spare), SUBLANES)
        rest = pl.multiple_of(jnp.where(staged > 0, off_ref[k] + full, spare), SUBLANES)
        kept = tail_s[x]
        merged = obuf[cur, pl.ds(first, SUBLANES), :] + kept * has
        first_s[cur, x] = merged
        ending = merged * alone + obuf[cur, pl.ds(rest, SUBLANES), :] * (1.0 - alone)
        tail_s[x] = ending * partial + kept * (1.0 - has)

    for x in range(N_EXPERTS):
        n = pl.multiple_of(full_ref[i * N_EXPERTS + x], SUBLANES)
        dst = pl.multiple_of(start_ref[i * N_EXPERTS + x], SUBLANES)

        @pl.when(n > 0)
        def _(x=x, dst=dst):
            pltpu.make_async_copy(first_s.at[cur, x], xs_ref.at[pl.ds(dst, SUBLANES), :], sem.at[cur]).start()

        @pl.when(n > SUBLANES)
        def _(x=x, n=n, dst=dst):
            src = pl.multiple_of(off_ref[i * N_EXPERTS + x] + SUBLANES, SUBLANES)
            more = pl.multiple_of(n - SUBLANES, SUBLANES)
            pltpu.make_async_copy(obuf.at[cur, pl.ds(src, more), :],
                                  xs_ref.at[pl.ds(pl.multiple_of(dst + SUBLANES, SUBLANES), more), :], sem.at[cur]).start()

    @pl.when(i == last)
    def _():
        @pl.when(i >= 1)
        def _():
            _wait_rows(obuf.at[1 - cur], xs_ref, sem.at[1 - cur], tot_ref[jnp.maximum(i - 1, 0)])

        _wait_rows(obuf.at[cur], xs_ref, sem.at[cur], tot_ref[i])
        for x in range(N_EXPERTS):
            row = flush_ref[x]

            @pl.when(row >= 0)
            def _(x=x, row=row):
                cp = pltpu.make_async_copy(tail_s.at[x], xs_ref.at[pl.ds(pl.multiple_of(row, SUBLANES), SUBLANES), :], semz)
                cp.start()
                cp.wait()
        zbuf[...] = jnp.zeros_like(zbuf)

        def zero_region(z, act):
            n = zb_ref[z] - za_ref[z]
            big = n // EXPERT_TILE
            rest = pl.multiple_of(n - big * EXPERT_TILE, SUBLANES)

            def piece(row, nrows):
                return pltpu.make_async_copy(zbuf.at[pl.ds(0, nrows), :],
                                             xs_ref.at[pl.ds(pl.multiple_of(row, SUBLANES), nrows), :], semz)

            lax.fori_loop(0, big, lambda m, c: (act(piece(za_ref[z] + m * EXPERT_TILE, EXPERT_TILE)), c)[1], 0)

            @pl.when(rest > 0)
            def _():
                act(piece(za_ref[z] + big * EXPERT_TILE, rest))

        for z in range(N_EXPERTS + 1):
            zero_region(z, lambda cp: cp.start())
        for z in range(N_EXPERTS + 1):
            zero_region(z, lambda cp: cp.wait())


def _dispatch(start, off, staged, full, tot, flush, za, zb, hf, e4, g4, cap):
    t = hf.shape[0]
    tm = TM_ROUTE
    tok = lambda i, *_: (0, i)
    return pl.pallas_call(
        _dispatch_kernel,
        grid_spec=pltpu.PrefetchScalarGridSpec(
            num_scalar_prefetch=8,
            grid=(t // tm,),
            in_specs=[
                pl.BlockSpec((tm, D_MODEL), lambda i, *_: (i, 0)),
                pl.BlockSpec((TOP_K, tm), tok),
                pl.BlockSpec((TOP_K, tm), tok),
            ],
            out_specs=[pl.BlockSpec(memory_space=pl.ANY), pl.BlockSpec((TOP_K, tm), tok)],
            scratch_shapes=[
                pltpu.VMEM((2, ROUTE_ROWS, XS_WIDTH), F32),
                pltpu.VMEM((EXPERT_TILE, XS_WIDTH), F32),
                pltpu.VMEM((N_EXPERTS, LANES), F32),
                pltpu.VMEM((N_EXPERTS, SUBLANES, XS_WIDTH), F32),
                pltpu.VMEM((2, N_EXPERTS, SUBLANES, XS_WIDTH), F32),
                pltpu.SemaphoreType.DMA((2,)),
                pltpu.SemaphoreType.DMA(()),
            ],
        ),
        out_shape=[jax.ShapeDtypeStruct((cap, XS_WIDTH), F32), jax.ShapeDtypeStruct((TOP_K, t), F32)],
        compiler_params=_params(48, ("arbitrary",)),
        name="dispatch",
    )(start, off, staged, full, tot, flush, za, zb, hf, e4, g4)


def _expert_kernel(te_ref, nu_ref, xs_ref, wgu_ref, bgu_ref, wdn_ref, bdn_ref, ys_ref, wgu_s, wdn_s):
    i = pl.program_id(0)

    @pl.when(i < nu_ref[0])
    def _():
        @pl.when((i == 0) | (te_ref[i] != te_ref[jnp.maximum(i - 1, 0)]))
        def _():
            wgu_s[...] = wgu_ref[0].astype(BF16)
            wdn_s[...] = wdn_ref[0].astype(BF16)

        x = xs_ref[:, 0:D_MODEL].astype(BF16)
        route_gate = xs_ref[:, D_MODEL:D_MODEL + 1]
        gu = _dot(x, wgu_s[...]) + bgu_ref[0]
        gate = jnp.minimum(gu[:, :D_EXPERT], SWIGLU_LIMIT)
        up = jnp.clip(gu[:, D_EXPERT:], -SWIGLU_LIMIT, SWIGLU_LIMIT)
        act = (up + 1.0) * (gate * jax.nn.sigmoid(SWIGLU_ALPHA * gate))
        ys_ref[...] = (_dot(act.astype(BF16), wdn_s[...]) + bdn_ref[0]) * route_gate

    @pl.when(i >= nu_ref[0])
    def _():
        ys_ref[...] = jnp.zeros_like(ys_ref)


def _experts(tile_expert, n_used, xs, wgu, bgu, wdn, bdn):
    cap = xs.shape[0]
    n_tiles = cap // EXPERT_TILE
    row = lambda i, te, nu: (jnp.minimum(i, nu[0] - 1), 0)
    by_expert = lambda i, te, nu: (te[i], 0, 0)
    return pl.pallas_call(
        _expert_kernel,
        grid_spec=pltpu.PrefetchScalarGridSpec(
            num_scalar_prefetch=2,
            grid=(n_tiles,),
            in_specs=[
                pl.BlockSpec((EXPERT_TILE, XS_WIDTH), row),
                pl.BlockSpec((1, D_MODEL, 2 * D_EXPERT), by_expert),
                pl.BlockSpec((1, 1, 2 * D_EXPERT), by_expert),
                pl.BlockSpec((1, D_EXPERT, D_MODEL), by_expert),
                pl.BlockSpec((1, 1, D_MODEL), by_expert),
            ],
            out_specs=pl.BlockSpec((EXPERT_TILE, D_MODEL), lambda i, te, nu: (i, 0)),
            scratch_shapes=[
                pltpu.VMEM((D_MODEL, 2 * D_EXPERT), BF16),
                pltpu.VMEM((D_EXPERT, D_MODEL), BF16),
            ],
        ),
        out_shape=jax.ShapeDtypeStruct((cap, D_MODEL), F32),
        compiler_params=_params(56, ("arbitrary",)),
        name="experts",
    )(tile_expert, n_used, xs, wgu, bgu, wdn, bdn)


def _combine_kernel(start_ref, off_ref, cnt_ref, tot_ref, x1_ref, rows_ref, ys_ref, o_ref, sbuf, sem):
    i = pl.program_id(0)
    last = pl.num_programs(0) - 1
    cur = lax.rem(i, 2)

    def fetch(step, b):
        for x in range(N_EXPERTS):
            n = pl.multiple_of(cnt_ref[step * N_EXPERTS + x], SUBLANES)

            @pl.when(n > 0)
            def _(x=x, n=n):
                src = pl.multiple_of(start_ref[step * N_EXPERTS + x], SUBLANES)
                dst = pl.multiple_of(off_ref[step * N_EXPERTS + x], SUBLANES)
                pltpu.make_async_copy(ys_ref.at[pl.ds(src, n), :], sbuf.at[b, pl.ds(dst, n), :], sem.at[b]).start()

    @pl.when(i == 0)
    def _():
        sbuf[...] = jnp.zeros_like(sbuf)
        fetch(i, cur)

    @pl.when(i < last)
    def _():
        fetch(i + 1, 1 - cur)

    rows = [rows_ref[k:k + 1, :] for k in range(TOP_K)]
    pt = _pick(rows, [1.0] * TOP_K, 0.0).astype(BF16)
    _wait_rows(sbuf.at[cur], ys_ref, sem.at[cur], tot_ref[i])
    picked = lax.dot_general(pt, sbuf[cur].astype(BF16), _TN, preferred_element_type=F32)
    o_ref[...] = x1_ref[...] + picked


def _combine(start, off, cnt, tot, x1, rows4, ys):
    t = x1.shape[0]
    tm = TM_ROUTE
    return pl.pallas_call(
        _combine_kernel,
        grid_spec=pltpu.PrefetchScalarGridSpec(
            num_scalar_prefetch=4,
            grid=(t // tm,),
            in_specs=[
                pl.BlockSpec((tm, D_MODEL), lambda i, *_: (i, 0)),
                pl.BlockSpec((TOP_K, tm), lambda i, *_: (0, i)),
                pl.BlockSpec(memory_space=pl.ANY),
            ],
            out_specs=pl.BlockSpec((tm, D_MODEL), lambda i, *_: (i, 0)),
            scratch_shapes=[
                pltpu.VMEM((2, ROUTE_ROWS, D_MODEL), F32),
                pltpu.SemaphoreType.DMA((2,)),
            ],
        ),
        out_shape=jax.ShapeDtypeStruct((t, D_MODEL), F32),
        compiler_params=_params(48, ("arbitrary",)),
        name="combine",
    )(start, off, cnt, tot, x1, rows4, ys)


def _layer(x, positions, norm1_g, w_in, q_norm_g, k_norm_g, sgu_ln_g, sgu_ln_b, w_spatial, b_spatial,
           w_branch_a, w_branch_b, w_out, norm2_g, w_router, b_router, w_gate_up, b_gate_up, w_down, b_down):
    b, seq, _ = x.shape
    t = b * seq
    x2 = x.reshape(t, D_MODEL)

    inv_freq = ROPE_THETA ** (-jnp.arange(ROT_HALF, dtype=F32) / ROT_HALF)
    invf = jnp.tile(inv_freq, LANES // ROT_HALF).reshape(1, LANES)
    dim_of_lane = [(l // (2 * ROT_HALF)) * ROT_HALF + l % ROT_HALF for l in range(LANES)]
    qg = q_norm_g[jnp.array(dim_of_lane)].reshape(1, LANES)
    kg = k_norm_g[jnp.array(dim_of_lane)].reshape(1, LANES)
    w_bf = w_in.astype(BF16)
    qk = w_bf[:, :2 * ATTN_WIDTH].reshape(D_MODEL, 2 * ATTN_WIDTH // LANES, 2, 2, ROT_HALF)
    qk = qk.transpose(0, 1, 3, 2, 4).reshape(D_MODEL, 2 * ATTN_WIDTH)
    w_in_p = jnp.concatenate([qk, w_bf[:, 2 * ATTN_WIDTH:]], axis=1)
    n_pairs = SGU_WIDTH // LANES
    wcat = w_spatial.reshape(n_pairs, 2, SGU_CHUNK, SGU_CHUNK).transpose(0, 2, 1, 3).reshape(n_pairs, SGU_CHUNK, 2 * SGU_CHUNK)
    bcat = jnp.repeat(b_spatial.reshape(n_pairs, 2, SGU_CHUNK).transpose(0, 2, 1), SGU_GROUP_DIM, axis=2)
    wr_t = w_router.T
    wr_hi = wr_t.astype(BF16)
    wr_lo = (wr_t - wr_hi.astype(F32)).astype(BF16)

    *qkv_groups, z, gates = _inproj(x2, norm1_g.reshape(1, D_MODEL), w_in_p, positions.reshape(-1, LANES // ROT_HALF), invf, qg, kg, b, seq)
    attn = _attention(qkv_groups)
    x1, hf, e4, g4, cst, counts = _post(
        attn.reshape(t, ATTN_WIDTH), z, gates, x2,
        w_branch_a.astype(BF16), w_branch_b.astype(BF16), w_out.astype(BF16),
        sgu_ln_g.reshape(1, SGU_WIDTH), sgu_ln_b.reshape(1, SGU_WIDTH), wcat.astype(BF16), bcat,
        norm2_g.reshape(1, D_MODEL), wr_hi, wr_lo, b_router.reshape(N_EXPERTS, 1))

    steps_per_tile = TM_POST // TM_ROUTE
    n_steps = t // TM_ROUTE
    before = cst[:, :, :steps_per_tile].transpose(0, 2, 1).reshape(n_steps, N_EXPERTS)
    total = counts[:, 0]
    cnt = jnp.concatenate([before[1:], total[None, :]], axis=0) - before
    head = before % SUBLANES
    staged = jnp.where(cnt > 0, (head + cnt + SUBLANES - 1) // SUBLANES * SUBLANES, 0)
    full = jnp.where(cnt > 0, (head + cnt) // SUBLANES * SUBLANES, 0)
    off = (jnp.cumsum(staged, axis=1) - staged).astype(I32).reshape(-1)
    tot_staged = jnp.sum(staged, axis=1).astype(I32)
    tot_full = jnp.sum(full, axis=1).astype(I32)
    region = (total + EXPERT_TILE - 1) // EXPERT_TILE * EXPERT_TILE
    rend = jnp.cumsum(region)
    rstart = rend - region
    start = (rstart[None, :] + before - head).astype(I32).reshape(-1)
    bound = t * TOP_K + N_EXPERTS * (EXPERT_TILE - 1)
    cap = (bound + EXPERT_TILE - 1) // EXPERT_TILE * EXPERT_TILE
    n_tiles = cap // EXPERT_TILE
    tile_expert = jnp.sum(jnp.arange(n_tiles, dtype=I32)[:, None] * EXPERT_TILE >= rend[None, :], axis=1)
    tile_expert = jnp.minimum(tile_expert, N_EXPERTS - 1).astype(I32)
    n_used = (rend[-1:] // EXPERT_TILE).astype(I32)
    filled = (total + SUBLANES - 1) // SUBLANES * SUBLANES
    flush = jnp.where(total % SUBLANES != 0, rstart + total // SUBLANES * SUBLANES, -1).astype(I32)
    za = jnp.concatenate([rstart + filled, rend[-1:]]).astype(I32)
    zb = jnp.concatenate([rend, jnp.full((1,), cap, rend.dtype)]).astype(I32)
    staged = staged.astype(I32).reshape(-1)
    full = full.astype(I32).reshape(-1)

    xs, rows4 = _dispatch(start, off, staged, full, tot_full, flush, za, zb, hf, e4, g4, cap)
    ys = _experts(tile_expert, n_used, xs,
                  w_gate_up, b_gate_up.reshape(N_EXPERTS, 1, 2 * D_EXPERT),
                  w_down, b_down.reshape(N_EXPERTS, 1, D_MODEL))
    out = _combine(start, off, staged, tot_staged, x1, rows4, ys)
    return out.reshape(b, seq, D_MODEL)


def kernel(x, positions, norm1_g, w_in, q_norm_g, k_norm_g, sgu_ln_g, sgu_ln_b, w_spatial, b_spatial, w_branch_a, w_branch_b, w_out, norm2_g, w_router, b_router, w_gate_up, b_gate_up, w_down, b_down):
    for layer in range(norm1_g.shape[0]):
        x = _layer(x, positions, norm1_g[layer], w_in[layer], q_norm_g[layer], k_norm_g[layer],
                   sgu_ln_g[layer], sgu_ln_b[layer], w_spatial[layer], b_spatial[layer],
                   w_branch_a[layer], w_branch_b[layer], w_out[layer], norm2_g[layer],
                   w_router[layer], b_router[layer], w_gate_up[layer], b_gate_up[layer],
                   w_down[layer], b_down[layer])
    return x
```

```python
import functools

import jax
import jax.numpy as jnp
from jax import lax
from jax.experimental import pallas as pl
from jax.experimental.pallas import tpu as pltpu

F32 = jnp.float32
BF16 = jnp.bfloat16
I32 = jnp.int32

D_MODEL = 1024
HEAD_DIM = 64
ROT_HALF = HEAD_DIM // 2
LOG2E = 1.4426950408889634
LN2 = 0.6931471805599453
Q_SCALE = HEAD_DIM ** -0.5 * LOG2E
ATTN_GROUPS = ((128, 1), (512, 4), (2048, 16))
N_GROUPS = len(ATTN_GROUPS)
GROUP_WIDTH = 256
ATTN_WIDTH = N_GROUPS * GROUP_WIDTH
BLK = 128
ROPE_THETA = 10000.0
SGU_CHUNK = 128
SGU_GROUP_DIM = 64
SGU_WIDTH = 768
GATE_WIDTH = 2 * D_MODEL
N_EXPERTS = 32
TOP_K = 4
D_EXPERT = D_MODEL
SWIGLU_LIMIT = 7.0
SWIGLU_ALPHA = 1.702
EPS = 1e-6
NEG_INF = -1e30
LANES = 128

TM_INPROJ = 512
TM_POST = 512
TM_ROUTE = 256
SUBLANES = 8
ROUTE_ROWS = TM_ROUTE * TOP_K + N_EXPERTS * 2 * SUBLANES
EXPERT_TILE = 512
XS_WIDTH = D_MODEL + LANES
ROW_CHUNK = 256

_NT = (((1,), (1,)), ((), ()))
_TN = (((0,), (0,)), ((), ()))


def _dot(a, b):
    return jnp.dot(a, b, preferred_element_type=F32)


def _dot_nt(a, b):
    return lax.dot_general(a, b, _NT, preferred_element_type=F32)


def _gelu(x):
    return 0.5 * x * (1.0 + lax.erf(x * 0.7071067811865476))


def _params(vmem_mb, sem):
    return pltpu.CompilerParams(dimension_semantics=sem, vmem_limit_bytes=vmem_mb * 1024 * 1024)


def _split3(x):
    a = x.astype(BF16)
    r = x - a.astype(F32)
    b = r.astype(BF16)
    return a, b, (r - b.astype(F32)).astype(BF16)


def _inproj_kernel(x_ref, g_ref, w_ref, pos_ref, invf_ref, qg_ref, kg_ref,
                   o0_ref, o1_ref, o2_ref, oz_ref, og_ref, raw_a, raw_b, stage_s, cos_s, sin_s):
    step = pl.program_id(0)

    @pl.when(step == 0)
    def _():
        raw_b[...] = jnp.zeros_like(raw_b)

    args = (x_ref, g_ref, w_ref, pos_ref, invf_ref, qg_ref, kg_ref, o0_ref, o1_ref, o2_ref, oz_ref, og_ref)

    @pl.when(lax.rem(step, 2) == 0)
    def _():
        _inproj_step(*args, raw_a, raw_b, stage_s, cos_s, sin_s)

    @pl.when(lax.rem(step, 2) == 1)
    def _():
        _inproj_step(*args, raw_b, raw_a, stage_s, cos_s, sin_s)


def _inproj_step(x_ref, g_ref, w_ref, pos_ref, invf_ref, qg_ref, kg_ref,
                 o0_ref, o1_ref, o2_ref, oz_ref, og_ref, raw_w, raw_r, stage_s, cos_s, sin_s):
    tm = x_ref.shape[0]
    x = x_ref[...]
    y = x * lax.rsqrt(jnp.mean(x * x, axis=-1, keepdims=True) + EPS)
    h = (y * g_ref[...]).astype(BF16)

    def project(lo):
        p = _dot(h, w_ref[:, lo:lo + GROUP_WIDTH])
        zc = 3 * ATTN_WIDTH
        gc = zc + 2 * SGU_WIDTH
        if lo < zc:
            raw_w[:, lo:lo + GROUP_WIDTH] = p
        elif lo < gc:
            oz_ref[:, lo - zc:lo - zc + GROUP_WIDTH] = p.astype(BF16)
        else:
            og_ref[:, lo - gc:lo - gc + GROUP_WIDTH] = p.astype(BF16)

    def rotary_tables():
        per_row = LANES // ROT_HALF
        lane = lax.broadcasted_iota(I32, (tm // per_row, LANES), 1)
        p4 = pos_ref[...].astype(F32)
        posd = p4[:, per_row - 1:per_row]
        for j in range(per_row - 2, -1, -1):
            posd = jnp.where(lane < (j + 1) * ROT_HALF, p4[:, j:j + 1], posd)
        ang = posd * invf_ref[...]
        cparts = _split3(jnp.cos(ang))
        sparts = _split3(jnp.sin(ang))
        src = lax.broadcasted_iota(I32, (LANES, LANES), 0)
        dst = lax.broadcasted_iota(I32, (LANES, LANES), 1)
        for j in range(per_row):
            pick = src == j * ROT_HALF + dst % ROT_HALF
            spread = jnp.where(pick, 1.0, 0.0).astype(BF16)
            spread_neg = jnp.where(pick, jnp.where(dst < 2 * ROT_HALF, -1.0, 1.0), 0.0).astype(BF16)
            cos_s[pl.ds(j, tm // per_row, stride=per_row), :] = sum(_dot(p, spread) for p in cparts)
            sin_s[pl.ds(j, tm // per_row, stride=per_row), :] = sum(_dot(p, spread_neg) for p in sparts)

    li = (lax.broadcasted_iota(I32, (LANES, LANES), 0) // ROT_HALF) % 2
    lj = (lax.broadcasted_iota(I32, (LANES, LANES), 1) // ROT_HALF) % 2
    head_sum = jnp.where(li == lj, 1.0, 0.0).astype(BF16)
    outs = (o0_ref, o1_ref, o2_ref)

    def finish(gi, d, ti, mode, gain_ref, c, stage):
        col = ti * ATTN_WIDTH + gi * GROUP_WIDTH
        t = raw_r[:, col + c * LANES:col + (c + 1) * LANES]
        if mode != "v":
            ss = t * t
            hi = ss.astype(BF16)
            lo = (ss - hi.astype(F32)).astype(BF16)
            ms = (_dot(hi, head_sum) + _dot(lo, head_sum)) * (1.0 / HEAD_DIM)
            t = t * lax.rsqrt(ms + EPS) * gain_ref[...]
            t = t * cos_s[...] + pltpu.roll(t, 2 * ROT_HALF, 1) * sin_s[...]
            if mode == "q":
                t = t * Q_SCALE
        lanes = slice(ti * GROUP_WIDTH + c * LANES, ti * GROUP_WIDTH + (c + 1) * LANES)
        if d == 1:
            outs[gi][0, 0, :, lanes] = t.astype(BF16)
        else:
            stage_s[stage] = t
            for r in range(d):
                outs[gi][0, r, :, lanes] = stage_s[stage, pl.ds(r, tm // d, stride=d), :].astype(BF16)

    pieces = [rotary_tables]
    stage = 0
    for gi, (_, d) in enumerate(ATTN_GROUPS):
        for ti, (mode, gain_ref) in enumerate((("q", qg_ref), ("k", kg_ref), ("v", None))):
            for c in range(GROUP_WIDTH // LANES):
                pieces.append(functools.partial(finish, gi, d, ti, mode, gain_ref, c, stage))
                stage += d > 1

    chunks = list(range(0, w_ref.shape[1], GROUP_WIDTH))
    for k in range(max(len(chunks), len(pieces))):
        if k < len(chunks):
            project(chunks[k])
        if k < len(pieces):
            pieces[k]()


def _inproj(x2, g, w, pos2, invf, qg, kg, b, seq):
    t = x2.shape[0]
    n = w.shape[1]
    tm = TM_INPROJ
    nj = seq // tm
    n_tiles = b * nj
    n_stage = sum(1 for _, d in ATTN_GROUPS if d > 1) * 3 * (GROUP_WIDTH // LANES)
    proj_row = lambda s: (jnp.minimum(s, n_tiles - 1), 0)
    done_row = lambda s: (jnp.maximum(s - 1, 0), 0)
    done_blk = lambda s: (jnp.maximum(s - 1, 0) // nj, 0, jnp.maximum(s - 1, 0) % nj, 0)
    const = lambda s: (0, 0)
    per_row = LANES // ROT_HALF
    return pl.pallas_call(
        _inproj_kernel,
        grid=(n_tiles + 1,),
        in_specs=[
            pl.BlockSpec((tm, D_MODEL), proj_row),
            pl.BlockSpec((1, D_MODEL), const),
            pl.BlockSpec((D_MODEL, n), const, pipeline_mode=pl.Buffered(1)),
            pl.BlockSpec((tm // per_row, per_row), done_row),
            pl.BlockSpec((1, LANES), const),
            pl.BlockSpec((1, LANES), const),
            pl.BlockSpec((1, LANES), const),
        ],
        out_specs=[pl.BlockSpec((1, d, tm // d, ATTN_WIDTH), done_blk) for _, d in ATTN_GROUPS] + [
            pl.BlockSpec((tm, 2 * SGU_WIDTH), proj_row),
            pl.BlockSpec((tm, GATE_WIDTH), proj_row),
        ],
        out_shape=[jax.ShapeDtypeStruct((b, d, seq // d, ATTN_WIDTH), BF16) for _, d in ATTN_GROUPS] + [
            jax.ShapeDtypeStruct((t, 2 * SGU_WIDTH), BF16),
            jax.ShapeDtypeStruct((t, GATE_WIDTH), BF16),
        ],
        scratch_shapes=[
            pltpu.VMEM((tm, 3 * ATTN_WIDTH), F32),
            pltpu.VMEM((tm, 3 * ATTN_WIDTH), F32),
            pltpu.VMEM((n_stage, tm, LANES), F32),
            pltpu.VMEM((tm, LANES), F32),
            pltpu.VMEM((tm, LANES), F32),
        ],
        compiler_params=_params(56, ("arbitrary",)),
        name="inproj",
    )(x2, g, w, pos2, invf, qg, kg)


def _attn_blocks(qkv_ref, ores_s, lres_s, seq, d):
    sub = seq // d
    nb = sub // BLK
    win = 2 * BLK if nb > 1 else BLK
    log2d = d.bit_length() - 1
    lane = lax.broadcasted_iota(I32, (BLK, GROUP_WIDTH), 1)
    q_head = 2 * (lane // LANES) + (lane // ROT_HALF) % 2
    qi = lax.broadcasted_iota(I32, (BLK, win), 0)
    kj = lax.broadcasted_iota(I32, (BLK, win), 1)
    rel = qi - kj

    def body(idx, carry):
        r = idx & (d - 1)
        n = idx >> log2d
        row0 = pl.multiple_of(r * sub + n * BLK, BLK)
        if nb > 1:
            kn = jnp.maximum(n - 1, 0)
            k0 = pl.multiple_of(r * sub + kn * BLK, BLK)
            dist = rel + (n - kn) * BLK
        else:
            k0 = row0
            dist = rel
        valid = lax.bitcast_convert_type(dist, jnp.uint32) <= jnp.uint32(BLK)
        q = qkv_ref[0, pl.ds(row0, BLK), 0:GROUP_WIDTH]
        kw = qkv_ref[0, pl.ds(k0, win), GROUP_WIDTH:2 * GROUP_WIDTH]
        vw = qkv_ref[0, pl.ds(k0, win), 2 * GROUP_WIDTH:3 * GROUP_WIDTH]
        zero = jnp.zeros_like(q)
        qm = jnp.concatenate([jnp.where(q_head == h, q, zero) for h in range(4)], axis=0)
        s = _dot_nt(qm, kw)
        ps, ms, ls = [], [], []
        for h in range(4):
            sh = jnp.where(valid, s[h * BLK:(h + 1) * BLK], NEG_INF)
            m = jnp.max(sh, axis=-1, keepdims=True)
            p = jnp.exp2(sh - m)
            ls.append(jnp.sum(p, axis=-1, keepdims=True))
            ms.append(m)
            ps.append(p.astype(BF16))
        pv = _dot(jnp.concatenate(ps, axis=0), vw)
        for h in range(4):
            lanes = slice(h * HEAD_DIM, (h + 1) * HEAD_DIM)
            ores_s[pl.ds(row0, BLK), lanes] = pv[h * BLK:(h + 1) * BLK, lanes] * (1.0 / ls[h])
            lres_s[pl.ds(row0, BLK), lanes] = jnp.broadcast_to((ms[h] + jnp.log2(ls[h])) * LN2, (BLK, HEAD_DIM))
        return carry

    lax.fori_loop(0, seq // BLK, body, 0, unroll=True)


def _attn_kernel(g0_ref, g1_ref, g2_ref, o_ref, ores_s, lres_s, onat_s, lnat_s, *, seq):
    for gi, ((_, d), qkv_ref) in enumerate(zip(ATTN_GROUPS, (g0_ref, g1_ref, g2_ref))):
        _attn_blocks(qkv_ref, ores_s, lres_s, seq, d)
        sub = seq // d
        for r in range(d):
            for c in range(GROUP_WIDTH // LANES):
                t = gi * (GROUP_WIDTH // LANES) + c
                onat_s[t, pl.ds(r, sub, stride=d), :] = ores_s[r * sub:(r + 1) * sub, c * LANES:(c + 1) * LANES]
                lnat_s[t, pl.ds(r, sub, stride=d), :] = lres_s[r * sub:(r + 1) * sub, c * LANES:(c + 1) * LANES]

    tiles = GROUP_WIDTH // LANES

    def body(i, carry):
        r0 = pl.multiple_of(i * ROW_CHUNK, ROW_CHUNK)
        for c in range(tiles):
            ls = [lnat_s[gi * tiles + c, pl.ds(r0, ROW_CHUNK), :] for gi in range(N_GROUPS)]
            m = jnp.maximum(jnp.maximum(ls[0], ls[1]), ls[2])
            es = [jnp.exp(l - m) for l in ls]
            inv = 1.0 / (es[0] + es[1] + es[2])
            for gi in range(N_GROUPS):
                t = gi * tiles + c
                o_ref[0, pl.ds(r0, ROW_CHUNK), t * LANES:(t + 1) * LANES] = (
                    onat_s[t, pl.ds(r0, ROW_CHUNK), :] * (es[gi] * inv)).astype(BF16)
        return carry

    lax.fori_loop(0, seq // ROW_CHUNK, body, 0, unroll=2)


def _attention(qkv_groups):
    b = qkv_groups[0].shape[0]
    seq = qkv_groups[0].shape[1] * qkv_groups[0].shape[2]
    tiles = GROUP_WIDTH // LANES
    blk = pl.BlockSpec((1, seq, ATTN_WIDTH), lambda i: (i, 0, 0))
    return pl.pallas_call(
        functools.partial(_attn_kernel, seq=seq),
        grid=(b,),
        in_specs=[blk] * N_GROUPS,
        out_specs=blk,
        out_shape=jax.ShapeDtypeStruct((b, seq, ATTN_WIDTH), BF16),
        scratch_shapes=[
            pltpu.VMEM((seq, GROUP_WIDTH), F32),
            pltpu.VMEM((seq, GROUP_WIDTH), F32),
            pltpu.VMEM((N_GROUPS * tiles, seq, LANES), F32),
            pltpu.VMEM((N_GROUPS * tiles, seq, LANES), F32),
        ],
        compiler_params=_params(48, ("arbitrary",)),
        name="attention",
    )(*[a.reshape(b, seq, ATTN_WIDTH) for a in qkv_groups])


def _post_kernel(attn_ref, u_ref, vz_ref, ga_ref, gb_ref, x_ref, wa_ref, wb_ref, wo_ref,
                 lng_ref, lnb_ref, wcat_ref, bcat_ref, n2g_ref, wrh_ref, wrl_ref, br_ref,
                 x1_ref, hf_ref, e_ref, g_ref, cst_ref, cnt_ref, sgu_s, carry_s, merged_a, merged_b):
    step = pl.program_id(0)

    @pl.when(step == 0)
    def _():
        carry_s[...] = jnp.zeros_like(carry_s)
        merged_b[...] = jnp.zeros_like(merged_b)

    args = (attn_ref, u_ref, vz_ref, ga_ref, gb_ref, x_ref, wa_ref, wb_ref, wo_ref, lng_ref, lnb_ref, wcat_ref,
            bcat_ref, n2g_ref, wrh_ref, wrl_ref, br_ref, x1_ref, hf_ref, e_ref, g_ref, cst_ref, cnt_ref, sgu_s, carry_s)

    @pl.when(lax.rem(step, 2) == 0)
    def _():
        _post_step(*args, merged_a, merged_b)

    @pl.when(lax.rem(step, 2) == 1)
    def _():
        _post_step(*args, merged_b, merged_a)


def _post_step(attn_ref, u_ref, vz_ref, ga_ref, gb_ref, x_ref, wa_ref, wb_ref, wo_ref,
               lng_ref, lnb_ref, wcat_ref, bcat_ref, n2g_ref, wrh_ref, wrl_ref, br_ref,
               x1_ref, hf_ref, e_ref, g_ref, cst_ref, cnt_ref, sgu_s, carry_s, merged_w, merged_r):
    tm = x_ref.shape[0]
    counted = jnp.where(pl.program_id(0) > 0, 1.0, 0.0)

    lane = lax.broadcasted_iota(I32, (SGU_CHUNK, LANES), 1)
    low = lane < SGU_GROUP_DIM
    trow = lax.broadcasted_iota(I32, (SGU_CHUNK, 2 * SGU_CHUNK), 0)
    tcol = lax.broadcasted_iota(I32, (SGU_CHUNK, 2 * SGU_CHUNK), 1) % SGU_CHUNK
    causal = tcol <= trow

    def chunk(c):
        r0 = c * SGU_CHUNK
        u = _gelu(u_ref[pl.ds(r0, SGU_CHUNK), :].astype(F32))
        v = _gelu(vz_ref[pl.ds(r0, SGU_CHUNK), :].astype(F32))
        mu = jnp.mean(v, axis=-1, keepdims=True)
        vc = v - mu
        vn = vc * lax.rsqrt(jnp.mean(vc * vc, axis=-1, keepdims=True) + EPS)
        vn = (vn * lng_ref[...] + lnb_ref[...]).astype(BF16)
        zero = jnp.zeros((SGU_CHUNK, LANES), BF16)
        for j in range(SGU_WIDTH // LANES):
            vt = vn[:, j * LANES:(j + 1) * LANES]
            rhs = jnp.concatenate([jnp.where(low, vt, zero), jnp.where(low, zero, vt)], axis=0)
            wj = jnp.where(causal, wcat_ref[j], jnp.zeros((), BF16))
            mixed = _dot(wj, rhs) + bcat_ref[j]
            sgu_s[pl.ds(r0, SGU_CHUNK), j * LANES:(j + 1) * LANES] = (u[:, j * LANES:(j + 1) * LANES] * mixed).astype(BF16)

    eio = lax.broadcasted_iota(I32, (N_EXPERTS, tm), 0)
    state = {}
    top_v, onehots = [], []

    def route_logits(hi, lo):
        state["work"] = (_dot_nt(wrh_ref[...], hi) + _dot_nt(wrl_ref[...], hi) + _dot_nt(wrh_ref[...], lo)
                         + br_ref[...])

    def route_pick(k):
        work = state["work"]
        m = jnp.max(work, axis=0, keepdims=True)
        idx = jnp.min(jnp.where(work == m, eio, N_EXPERTS), axis=0, keepdims=True)
        oh = eio == idx
        state["work"] = jnp.where(oh, -jnp.inf, work)
        top_v.append(m)
        onehots.append(oh)
        e_ref[k:k + 1, :] = idx

    def route_finish():
        ex = [jnp.exp(v - top_v[0]) for v in top_v]
        inv = 1.0 / (ex[0] + ex[1] + ex[2] + ex[3])
        for k in range(TOP_K):
            g_ref[k:k + 1, :] = ex[k] * inv
        sel = jnp.zeros((N_EXPERTS, tm), F32)
        for oh in onehots:
            sel = jnp.where(oh, counted, sel)
        lane_e = lax.broadcasted_iota(I32, (N_EXPERTS, LANES), 1)
        run = carry_s[...]
        cst = jnp.zeros((N_EXPERTS, LANES), F32)
        for s in range(tm // TM_ROUTE):
            cst = jnp.where(lane_e == s, run, cst)
            run = run + jnp.sum(sel[:, s * TM_ROUTE:(s + 1) * TM_ROUTE], axis=1, keepdims=True)
        cst_ref[0] = cst.astype(I32)
        carry_s[...] = run
        cnt_ref[...] = run.astype(I32)

    n_parts = tm // SGU_CHUNK
    width = D_MODEL // n_parts
    x1_parts, y_a_parts = [], []
    for c in range(n_parts):
        cols = slice(c * width, (c + 1) * width)
        chunk(c)
        x1_parts.append(x_ref[:, cols] + _dot(merged_r[...], wo_ref[:, cols]))
        y_a_parts.append(_dot(attn_ref[...], wa_ref[:, cols]))
    x1 = jnp.concatenate(x1_parts, axis=1)
    x1_ref[...] = x1
    hf = x1 * lax.rsqrt(jnp.mean(x1 * x1, axis=-1, keepdims=True) + EPS) * n2g_ref[...]
    hi = hf.astype(BF16)
    hf_ref[...] = hi
    route_logits(hi, (hf - hi.astype(F32)).astype(BF16))
    for c in range(n_parts):
        cols = slice(c * width, (c + 1) * width)
        y_b = _dot(sgu_s[...], wb_ref[:, cols])
        merged_w[:, cols] = (jax.nn.sigmoid(ga_ref[:, cols].astype(F32)) * y_a_parts[c]
                             + jax.nn.sigmoid(gb_ref[:, cols].astype(F32)) * y_b).astype(BF16)
        if c < TOP_K:
            route_pick(c)
    for k in range(n_parts, TOP_K):
        route_pick(k)
    route_finish()


def _post(attn2, z2, gates2, x2, wa, wb, wo, lng, lnb, wcat, bcat, n2g, wrh, wrl, br):
    t = x2.shape[0]
    tm = TM_POST
    const = lambda *shape: pl.BlockSpec(shape, lambda i: (0,) * len(shape), pipeline_mode=pl.Buffered(1))
    n_tiles = t // tm
    cur = lambda i: jnp.minimum(i, n_tiles - 1)
    prev = lambda i: jnp.maximum(i - 1, 0)
    return pl.pallas_call(
        _post_kernel,
        grid=(n_tiles + 1,),
        in_specs=[
            pl.BlockSpec((tm, ATTN_WIDTH), lambda i: (cur(i), 0)),
            pl.BlockSpec((tm, SGU_WIDTH), lambda i: (cur(i), 0)),
            pl.BlockSpec((tm, SGU_WIDTH), lambda i: (cur(i), 1)),
            pl.BlockSpec((tm, D_MODEL), lambda i: (cur(i), 0)),
            pl.BlockSpec((tm, D_MODEL), lambda i: (cur(i), 1)),
            pl.BlockSpec((tm, D_MODEL), lambda i: (prev(i), 0)),
            const(ATTN_WIDTH, D_MODEL),
            const(SGU_WIDTH, D_MODEL),
            const(D_MODEL, D_MODEL),
            const(1, SGU_WIDTH),
            const(1, SGU_WIDTH),
            const(SGU_WIDTH // LANES, SGU_CHUNK, 2 * SGU_CHUNK),
            const(SGU_WIDTH // LANES, SGU_CHUNK, LANES),
            const(1, D_MODEL),
            const(N_EXPERTS, D_MODEL),
            const(N_EXPERTS, D_MODEL),
            const(N_EXPERTS, 1),
        ],
        out_specs=[
            pl.BlockSpec((tm, D_MODEL), lambda i: (prev(i), 0)),
            pl.BlockSpec((tm, D_MODEL), lambda i: (prev(i), 0)),
            pl.BlockSpec((TOP_K, tm), lambda i: (0, prev(i))),
            pl.BlockSpec((TOP_K, tm), lambda i: (0, prev(i))),
            pl.BlockSpec((1, N_EXPERTS, LANES), lambda i: (prev(i), 0, 0)),
            pl.BlockSpec((N_EXPERTS, LANES), lambda i: (0, 0)),
        ],
        out_shape=[
            jax.ShapeDtypeStruct((t, D_MODEL), F32),
            jax.ShapeDtypeStruct((t, D_MODEL), BF16),
            jax.ShapeDtypeStruct((TOP_K, t), I32),
            jax.ShapeDtypeStruct((TOP_K, t), F32),
            jax.ShapeDtypeStruct((t // tm, N_EXPERTS, LANES), I32),
            jax.ShapeDtypeStruct((N_EXPERTS, LANES), I32),
        ],
        scratch_shapes=[
            pltpu.VMEM((tm, SGU_WIDTH), BF16),
            pltpu.VMEM((N_EXPERTS, LANES), F32),
        ] + [pltpu.VMEM((tm, D_MODEL), BF16)] * 2,
        compiler_params=_params(48, ("arbitrary",)),
        name="post",
    )(attn2, z2, z2, gates2, gates2, x2, wa, wb, wo, lng, lnb, wcat, bcat, n2g, wrh, wrl, br)


def _route_rows(e_ref, seen, tm):
    e = e_ref[...]
    eio = lax.broadcasted_iota(I32, (N_EXPERTS, tm), 0)
    onehots = [eio == e[k:k + 1, :] for k in range(TOP_K)]
    sel = jnp.zeros((N_EXPERTS, tm), F32)
    for oh in onehots:
        sel = jnp.where(oh, 1.0, sel)
    ti = lax.broadcasted_iota(I32, (tm, tm), 0)
    tj = lax.broadcasted_iota(I32, (tm, tm), 1)
    before = jnp.where(ti < tj, 1.0, 0.0).astype(BF16)
    slot = _dot(sel.astype(BF16), before)
    cnt = jnp.sum(sel, axis=1, keepdims=True)
    head = seen - SUBLANES * jnp.floor(seen * (1.0 / SUBLANES))
    tiles = jnp.where(cnt > 0.0, jnp.floor((head + cnt + (SUBLANES - 1)) * (1.0 / SUBLANES)), 0.0)
    xi = lax.broadcasted_iota(I32, (N_EXPERTS, N_EXPERTS), 0)
    xj = lax.broadcasted_iota(I32, (N_EXPERTS, N_EXPERTS), 1)
    lower = jnp.where(xj < xi, 1.0, 0.0).astype(BF16)
    off = _dot(lower, jnp.broadcast_to(tiles, (N_EXPERTS, LANES)).astype(BF16))[:, 0:1] * float(SUBLANES)
    row = off + head + slot
    return [jnp.sum(jnp.where(oh, row, 0.0), axis=0, keepdims=True) for oh in onehots], cnt


def _pick(rows, values, default):
    tm = rows[0].shape[1]
    rio = lax.broadcasted_iota(I32, (ROUTE_ROWS, tm), 0).astype(F32)
    out = default
    for r, v in zip(rows, values):
        out = jnp.where(rio == r, v, out)
    return out


def _wait_rows(buf, hbm_ref, sem, n):
    n = pl.multiple_of(n, SUBLANES)

    @pl.when(n > 0)
    def _():
        pltpu.make_async_copy(buf.at[pl.ds(0, n), :], hbm_ref.at[pl.ds(0, n), :], sem).wait()


def _dispatch_kernel(start_ref, off_ref, staged_ref, full_ref, tot_ref, flush_ref, za_ref, zb_ref, hf_ref, e_ref, g_ref,
                     xs_ref, rows_ref, obuf, zbuf, seen_s, tail_s, first_s, sem, semz):
    tm = hf_ref.shape[0]
    i = pl.program_id(0)
    last = pl.num_programs(0) - 1
    cur = lax.rem(i, 2)
    spare = ROUTE_ROWS - SUBLANES

    @pl.when(i == 0)
    def _():
        seen_s[...] = jnp.zeros_like(seen_s)
        tail_s[...] = jnp.zeros_like(tail_s)

    @pl.when(i >= 2)
    def _():
        _wait_rows(obuf.at[cur], xs_ref, sem.at[cur], tot_ref[jnp.maximum(i - 2, 0)])

    rows, cnt = _route_rows(e_ref, seen_s[:, 0:1], tm)
    seen_s[...] = seen_s[...] + cnt
    for k in range(TOP_K):
        rows_ref[k:k + 1, :] = rows[k]
    g = g_ref[...]
    pt = _pick(rows, [1.0] * TOP_K, 0.0).astype(BF16)
    gate = jnp.sum(_pick(rows, [g[k:k + 1, :] for k in range(TOP_K)], 0.0), axis=1, keepdims=True)
    obuf[cur, :, 0:D_MODEL] = _dot(pt, hf_ref[...])
    obuf[cur, :, D_MODEL:XS_WIDTH] = jnp.broadcast_to(gate, (ROUTE_ROWS, LANES))

    for x in range(N_EXPERTS):
        k = i * N_EXPERTS + x
        staged, full = staged_ref[k], full_ref[k]
        has = jnp.where(staged > 0, 1.0, 0.0)
        partial = jnp.where(staged > full, 1.0, 0.0)
        alone = jnp.where(full == 0, 1.0, 0.0)
        first = pl.multiple_of(jnp.where(staged > 0, off_ref[k], spare), SUBLANES)
        rest = pl.multiple_of(jnp.where(staged > 0, off_ref[k] + full, spare), SUBLANES)
        kept = tail_s[x]
        merged = obuf[cur, pl.ds(first, SUBLANES), :] + kept * has
        first_s[cur, x] = merged
        ending = merged * alone + obuf[cur, pl.ds(rest, SUBLANES), :] * (1.0 - alone)
        tail_s[x] = ending * partial + kept * (1.0 - has)

    for x in range(N_EXPERTS):
        n = pl.multiple_of(full_ref[i * N_EXPERTS + x], SUBLANES)
        dst = pl.multiple_of(start_ref[i * N_EXPERTS + x], SUBLANES)

        @pl.when(n > 0)
        def _(x=x, dst=dst):
            pltpu.make_async_copy(first_s.at[cur, x], xs_ref.at[pl.ds(dst, SUBLANES), :], sem.at[cur]).start()

        @pl.when(n > SUBLANES)
        def _(x=x, n=n, dst=dst):
            src = pl.multiple_of(off_ref[i * N_EXPERTS + x] + SUBLANES, SUBLANES)
            more = pl.multiple_of(n - SUBLANES, SUBLANES)
            pltpu.make_async_copy(obuf.at[cur, pl.ds(src, more), :],
                                  xs_ref.at[pl.ds(pl.multiple_of(dst + SUBLANES, SUBLANES), more), :], sem.at[cur]).start()

    @pl.when(i == last)
    def _():
        @pl.when(i >= 1)
        def _():
            _wait_rows(obuf.at[1 - cur], xs_ref, sem.at[1 - cur], tot_ref[jnp.maximum(i - 1, 0)])

        _wait_rows(obuf.at[cur], xs_ref, sem.at[cur], tot_ref[i])
        for x in range(N_EXPERTS):
            row = flush_ref[x]

            @pl.when(row >= 0)
            def _(x=x, row=row):
                cp = pltpu.make_async_copy(tail_s.at[x], xs_ref.at[pl.ds(pl.multiple_of(row, SUBLANES), SUBLANES), :], semz)
                cp.start()
                cp.wait()
        zbuf[...] = jnp.zeros_like(zbuf)

        def zero_region(z, act):
            n = zb_ref[z] - za_ref[z]
            big = n // EXPERT_TILE
            rest = pl.multiple_of(n - big * EXPERT_TILE, SUBLANES)

            def piece(row, nrows):
                return pltpu.make_async_copy(zbuf.at[pl.ds(0, nrows), :],
                                             xs_ref.at[pl.ds(pl.multiple_of(row, SUBLANES), nrows), :], semz)

            lax.fori_loop(0, big, lambda m, c: (act(piece(za_ref[z] + m * EXPERT_TILE, EXPERT_TILE)), c)[1], 0)

            @pl.when(rest > 0)
            def _():
                act(piece(za_ref[z] + big * EXPERT_TILE, rest))

        for z in range(N_EXPERTS + 1):
            zero_region(z, lambda cp: cp.start())
        for z in range(N_EXPERTS + 1):
            zero_region(z, lambda cp: cp.wait())


def _dispatch(start, off, staged, full, tot, flush, za, zb, hf, e4, g4, cap):
    t = hf.shape[0]
    tm = TM_ROUTE
    tok = lambda i, *_: (0, i)
    return pl.pallas_call(
        _dispatch_kernel,
        grid_spec=pltpu.PrefetchScalarGridSpec(
            num_scalar_prefetch=8,
            grid=(t // tm,),
            in_specs=[
                pl.BlockSpec((tm, D_MODEL), lambda i, *_: (i, 0)),
                pl.BlockSpec((TOP_K, tm), tok),
                pl.BlockSpec((TOP_K, tm), tok),
            ],
            out_specs=[pl.BlockSpec(memory_space=pl.ANY), pl.BlockSpec((TOP_K, tm), tok)],
            scratch_shapes=[
                pltpu.VMEM((2, ROUTE_ROWS, XS_WIDTH), F32),
                pltpu.VMEM((EXPERT_TILE, XS_WIDTH), F32),
                pltpu.VMEM((N_EXPERTS, LANES), F32),
                pltpu.VMEM((N_EXPERTS, SUBLANES, XS_WIDTH), F32),
                pltpu.VMEM((2, N_EXPERTS, SUBLANES, XS_WIDTH), F32),
                pltpu.SemaphoreType.DMA((2,)),
                pltpu.SemaphoreType.DMA(()),
            ],
        ),
        out_shape=[jax.ShapeDtypeStruct((cap, XS_WIDTH), F32), jax.ShapeDtypeStruct((TOP_K, t), F32)],
        compiler_params=_params(48, ("arbitrary",)),
        name="dispatch",
    )(start, off, staged, full, tot, flush, za, zb, hf, e4, g4)


def _expert_kernel(te_ref, nu_ref, xs_ref, wgu_ref, bgu_ref, wdn_ref, bdn_ref, ys_ref, wgu_s, wdn_s):
    i = pl.program_id(0)

    @pl.when(i < nu_ref[0])
    def _():
        @pl.when((i == 0) | (te_ref[i] != te_ref[jnp.maximum(i - 1, 0)]))
        def _():
            wgu_s[...] = wgu_ref[0].astype(BF16)
            wdn_s[...] = wdn_ref[0].astype(BF16)

        x = xs_ref[:, 0:D_MODEL].astype(BF16)
        route_gate = xs_ref[:, D_MODEL:D_MODEL + 1]
        gu = _dot(x, wgu_s[...]) + bgu_ref[0]
        gate = jnp.minimum(gu[:, :D_EXPERT], SWIGLU_LIMIT)
        up = jnp.clip(gu[:, D_EXPERT:], -SWIGLU_LIMIT, SWIGLU_LIMIT)
        act = (up + 1.0) * (gate * jax.nn.sigmoid(SWIGLU_ALPHA * gate))
        ys_ref[...] = (_dot(act.astype(BF16), wdn_s[...]) + bdn_ref[0]) * route_gate

    @pl.when(i >= nu_ref[0])
    def _():
        ys_ref[...] = jnp.zeros_like(ys_ref)


def _experts(tile_expert, n_used, xs, wgu, bgu, wdn, bdn):
    cap = xs.shape[0]
    n_tiles = cap // EXPERT_TILE
    row = lambda i, te, nu: (jnp.minimum(i, nu[0] - 1), 0)
    by_expert = lambda i, te, nu: (te[i], 0, 0)
    return pl.pallas_call(
        _expert_kernel,
        grid_spec=pltpu.PrefetchScalarGridSpec(
            num_scalar_prefetch=2,
            grid=(n_tiles,),
            in_specs=[
                pl.BlockSpec((EXPERT_TILE, XS_WIDTH), row),
                pl.BlockSpec((1, D_MODEL, 2 * D_EXPERT), by_expert),
                pl.BlockSpec((1, 1, 2 * D_EXPERT), by_expert),
                pl.BlockSpec((1, D_EXPERT, D_MODEL), by_expert),
                pl.BlockSpec((1, 1, D_MODEL), by_expert),
            ],
            out_specs=pl.BlockSpec((EXPERT_TILE, D_MODEL), lambda i, te, nu: (i, 0)),
            scratch_shapes=[
                pltpu.VMEM((D_MODEL, 2 * D_EXPERT), BF16),
                pltpu.VMEM((D_EXPERT, D_MODEL), BF16),
            ],
        ),
        out_shape=jax.ShapeDtypeStruct((cap, D_MODEL), F32),
        compiler_params=_params(56, ("arbitrary",)),
        name="experts",
    )(tile_expert, n_used, xs, wgu, bgu, wdn, bdn)


def _combine_kernel(start_ref, off_ref, cnt_ref, tot_ref, x1_ref, rows_ref, ys_ref, o_ref, sbuf, sem):
    i = pl.program_id(0)
    last = pl.num_programs(0) - 1
    cur = lax.rem(i, 2)

    def fetch(step, b):
        for x in range(N_EXPERTS):
            n = pl.multiple_of(cnt_ref[step * N_EXPERTS + x], SUBLANES)

            @pl.when(n > 0)
            def _(x=x, n=n):
                src = pl.multiple_of(start_ref[step * N_EXPERTS + x], SUBLANES)
                dst = pl.multiple_of(off_ref[step * N_EXPERTS + x], SUBLANES)
                pltpu.make_async_copy(ys_ref.at[pl.ds(src, n), :], sbuf.at[b, pl.ds(dst, n), :], sem.at[b]).start()

    @pl.when(i == 0)
    def _():
        sbuf[...] = jnp.zeros_like(sbuf)
        fetch(i, cur)

    @pl.when(i < last)
    def _():
        fetch(i + 1, 1 - cur)

    rows = [rows_ref[k:k + 1, :] for k in range(TOP_K)]
    pt = _pick(rows, [1.0] * TOP_K, 0.0).astype(BF16)
    _wait_rows(sbuf.at[cur], ys_ref, sem.at[cur], tot_ref[i])
    picked = lax.dot_general(pt, sbuf[cur].astype(BF16), _TN, preferred_element_type=F32)
    o_ref[...] = x1_ref[...] + picked


def _combine(start, off, cnt, tot, x1, rows4, ys):
    t = x1.shape[0]
    tm = TM_ROUTE
    return pl.pallas_call(
        _combine_kernel,
        grid_spec=pltpu.PrefetchScalarGridSpec(
            num_scalar_prefetch=4,
            grid=(t // tm,),
            in_specs=[
                pl.BlockSpec((tm, D_MODEL), lambda i, *_: (i, 0)),
                pl.BlockSpec((TOP_K, tm), lambda i, *_: (0, i)),
                pl.BlockSpec(memory_space=pl.ANY),
            ],
            out_specs=pl.BlockSpec((tm, D_MODEL), lambda i, *_: (i, 0)),
            scratch_shapes=[
                pltpu.VMEM((2, ROUTE_ROWS, D_MODEL), F32),
                pltpu.SemaphoreType.DMA((2,)),
            ],
        ),
        out_shape=jax.ShapeDtypeStruct((t, D_MODEL), F32),
        compiler_params=_params(48, ("arbitrary",)),
        name="combine",
    )(start, off, cnt, tot, x1, rows4, ys)


def _layer(x, positions, norm1_g, w_in, q_norm_g, k_norm_g, sgu_ln_g, sgu_ln_b, w_spatial, b_spatial,
           w_branch_a, w_branch_b, w_out, norm2_g, w_router, b_router, w_gate_up, b_gate_up, w_down, b_down):
    b, seq, _ = x.shape
    t = b * seq
    x2 = x.reshape(t, D_MODEL)

    inv_freq = ROPE_THETA ** (-jnp.arange(ROT_HALF, dtype=F32) / ROT_HALF)
    invf = jnp.tile(inv_freq, LANES // ROT_HALF).reshape(1, LANES)
    dim_of_lane = [(l // (2 * ROT_HALF)) * ROT_HALF + l % ROT_HALF for l in range(LANES)]
    qg = q_norm_g[jnp.array(dim_of_lane)].reshape(1, LANES)
    kg = k_norm_g[jnp.array(dim_of_lane)].reshape(1, LANES)
    w_bf = w_in.astype(BF16)
    qk = w_bf[:, :2 * ATTN_WIDTH].reshape(D_MODEL, 2 * ATTN_WIDTH // LANES, 2, 2, ROT_HALF)
    qk = qk.transpose(0, 1, 3, 2, 4).reshape(D_MODEL, 2 * ATTN_WIDTH)
    w_in_p = jnp.concatenate([qk, w_bf[:, 2 * ATTN_WIDTH:]], axis=1)
    n_pairs = SGU_WIDTH // LANES
    wcat = w_spatial.reshape(n_pairs, 2, SGU_CHUNK, SGU_CHUNK).transpose(0, 2, 1, 3).reshape(n_pairs, SGU_CHUNK, 2 * SGU_CHUNK)
    bcat = jnp.repeat(b_spatial.reshape(n_pairs, 2, SGU_CHUNK).transpose(0, 2, 1), SGU_GROUP_DIM, axis=2)
    wr_t = w_router.T
    wr_hi = wr_t.astype(BF16)
    wr_lo = (wr_t - wr_hi.astype(F32)).astype(BF16)

    *qkv_groups, z, gates = _inproj(x2, norm1_g.reshape(1, D_MODEL), w_in_p, positions.reshape(-1, LANES // ROT_HALF), invf, qg, kg, b, seq)
    attn = _attention(qkv_groups)
    x1, hf, e4, g4, cst, counts = _post(
        attn.reshape(t, ATTN_WIDTH), z, gates, x2,
        w_branch_a.astype(BF16), w_branch_b.astype(BF16), w_out.astype(BF16),
        sgu_ln_g.reshape(1, SGU_WIDTH), sgu_ln_b.reshape(1, SGU_WIDTH), wcat.astype(BF16), bcat,
        norm2_g.reshape(1, D_MODEL), wr_hi, wr_lo, b_router.reshape(N_EXPERTS, 1))

    steps_per_tile = TM_POST // TM_ROUTE
    n_steps = t // TM_ROUTE
    before = cst[:, :, :steps_per_tile].transpose(0, 2, 1).reshape(n_steps, N_EXPERTS)
    total = counts[:, 0]
    cnt = jnp.concatenate([before[1:], total[None, :]], axis=0) - before
    head = before % SUBLANES
    staged = jnp.where(cnt > 0, (head + cnt + SUBLANES - 1) // SUBLANES * SUBLANES, 0)
    full = jnp.where(cnt > 0, (head + cnt) // SUBLANES * SUBLANES, 0)
    off = (jnp.cumsum(staged, axis=1) - staged).astype(I32).reshape(-1)
    tot_staged = jnp.sum(staged, axis=1).astype(I32)
    tot_full = jnp.sum(full, axis=1).astype(I32)
    region = (total + EXPERT_TILE - 1) // EXPERT_TILE * EXPERT_TILE
    rend = jnp.cumsum(region)
    rstart = rend - region
    start = (rstart[None, :] + before - head).astype(I32).reshape(-1)
    bound = t * TOP_K + N_EXPERTS * (EXPERT_TILE - 1)
    cap = (bound + EXPERT_TILE - 1) // EXPERT_TILE * EXPERT_TILE
    n_tiles = cap // EXPERT_TILE
    tile_expert = jnp.sum(jnp.arange(n_tiles, dtype=I32)[:, None] * EXPERT_TILE >= rend[None, :], axis=1)
    tile_expert = jnp.minimum(tile_expert, N_EXPERTS - 1).astype(I32)
    n_used = (rend[-1:] // EXPERT_TILE).astype(I32)
    filled = (total + SUBLANES - 1) // SUBLANES * SUBLANES
    flush = jnp.where(total % SUBLANES != 0, rstart + total // SUBLANES * SUBLANES, -1).astype(I32)
    za = jnp.concatenate([rstart + filled, rend[-1:]]).astype(I32)
    zb = jnp.concatenate([rend, jnp.full((1,), cap, rend.dtype)]).astype(I32)
    staged = staged.astype(I32).reshape(-1)
    full = full.astype(I32).reshape(-1)

    xs, rows4 = _dispatch(start, off, staged, full, tot_full, flush, za, zb, hf, e4, g4, cap)
    ys = _experts(tile_expert, n_used, xs,
                  w_gate_up, b_gate_up.reshape(N_EXPERTS, 1, 2 * D_EXPERT),
                  w_down, b_down.reshape(N_EXPERTS, 1, D_MODEL))
    out = _combine(start, off, staged, tot_staged, x1, rows4, ys)
    return out.reshape(b, seq, D_MODEL)


def kernel(x, positions, norm1_g, w_in, q_norm_g, k_norm_g, sgu_ln_g, sgu_ln_b, w_spatial, b_spatial, w_branch_a, w_branch_b, w_out, norm2_g, w_router, b_router, w_gate_up, b_gate_up, w_down, b_down):
    for layer in range(norm1_g.shape[0]):
        x = _layer(x, positions, norm1_g[layer], w_in[layer], q_norm_g[layer], k_norm_g[layer],
                   sgu_ln_g[layer], sgu_ln_b[layer], w_spatial[layer], b_spatial[layer],
                   w_branch_a[layer], w_branch_b[layer], w_out[layer], norm2_g[layer],
                   w_router[layer], b_router[layer], w_gate_up[layer], b_gate_up[layer],
                   w_down[layer], b_down[layer])
    return x
```

```python
import functools

import jax
import jax.numpy as jnp
from jax import lax
from jax.experimental import pallas as pl
from jax.experimental.pallas import tpu as pltpu

F32 = jnp.float32
BF16 = jnp.bfloat16
I32 = jnp.int32

D_MODEL = 1024
HEAD_DIM = 64
ROT_HALF = HEAD_DIM // 2
LOG2E = 1.4426950408889634
LN2 = 0.6931471805599453
Q_SCALE = HEAD_DIM ** -0.5 * LOG2E
ATTN_GROUPS = ((128, 1), (512, 4), (2048, 16))
N_GROUPS = len(ATTN_GROUPS)
GROUP_WIDTH = 256
ATTN_WIDTH = N_GROUPS * GROUP_WIDTH
BLK = 128
ROPE_THETA = 10000.0
SGU_CHUNK = 128
SGU_GROUP_DIM = 64
SGU_WIDTH = 768
GATE_WIDTH = 2 * D_MODEL
N_EXPERTS = 32
TOP_K = 4
D_EXPERT = D_MODEL
SWIGLU_LIMIT = 7.0
SWIGLU_ALPHA = 1.702
EPS = 1e-6
NEG_INF = -1e30
LANES = 128

TM_INPROJ = 512
TM_POST = 512
TM_ROUTE = 256
SUBLANES = 8
ROUTE_ROWS = TM_ROUTE * TOP_K + N_EXPERTS * 2 * SUBLANES
EXPERT_TILE = 512
ROW_CHUNK = 256

_NT = (((1,), (1,)), ((), ()))
_TN = (((0,), (0,)), ((), ()))


def _dot(a, b):
    return jnp.dot(a, b, preferred_element_type=F32)


def _dot_nt(a, b):
    return lax.dot_general(a, b, _NT, preferred_element_type=F32)


def _gelu(x):
    return 0.5 * x * (1.0 + lax.erf(x * 0.7071067811865476))


def _params(vmem_mb, sem):
    return pltpu.CompilerParams(dimension_semantics=sem, vmem_limit_bytes=vmem_mb * 1024 * 1024)


def _split3(x):
    a = x.astype(BF16)
    r = x - a.astype(F32)
    b = r.astype(BF16)
    return a, b, (r - b.astype(F32)).astype(BF16)


def _inproj_kernel(x_ref, g_ref, w_ref, pos_ref, invf_ref, qg_ref, kg_ref,
                   o0_ref, o1_ref, o2_ref, oz_ref, og_ref, raw_a, raw_b, stage_s, cos_s, sin_s):
    step = pl.program_id(0)

    @pl.when(step == 0)
    def _():
        raw_b[...] = jnp.zeros_like(raw_b)

    args = (x_ref, g_ref, w_ref, pos_ref, invf_ref, qg_ref, kg_ref, o0_ref, o1_ref, o2_ref, oz_ref, og_ref)

    @pl.when(lax.rem(step, 2) == 0)
    def _():
        _inproj_step(*args, raw_a, raw_b, stage_s, cos_s, sin_s)

    @pl.when(lax.rem(step, 2) == 1)
    def _():
        _inproj_step(*args, raw_b, raw_a, stage_s, cos_s, sin_s)


def _inproj_step(x_ref, g_ref, w_ref, pos_ref, invf_ref, qg_ref, kg_ref,
                 o0_ref, o1_ref, o2_ref, oz_ref, og_ref, raw_w, raw_r, stage_s, cos_s, sin_s):
    tm = x_ref.shape[0]
    x = x_ref[...]
    y = x * lax.rsqrt(jnp.mean(x * x, axis=-1, keepdims=True) + EPS)
    h = (y * g_ref[...]).astype(BF16)

    def project(lo):
        p = _dot(h, w_ref[:, lo:lo + GROUP_WIDTH])
        zc = 3 * ATTN_WIDTH
        gc = zc + 2 * SGU_WIDTH
        if lo < zc:
            raw_w[:, lo:lo + GROUP_WIDTH] = p
        elif lo < gc:
            oz_ref[:, lo - zc:lo - zc + GROUP_WIDTH] = p.astype(BF16)
        else:
            og_ref[:, lo - gc:lo - gc + GROUP_WIDTH] = p.astype(BF16)

    def rotary_tables():
        per_row = LANES // ROT_HALF
        lane = lax.broadcasted_iota(I32, (tm // per_row, LANES), 1)
        p4 = pos_ref[...].astype(F32)
        posd = p4[:, per_row - 1:per_row]
        for j in range(per_row - 2, -1, -1):
            posd = jnp.where(lane < (j + 1) * ROT_HALF, p4[:, j:j + 1], posd)
        ang = posd * invf_ref[...]
        cparts = _split3(jnp.cos(ang))
        sparts = _split3(jnp.sin(ang))
        src = lax.broadcasted_iota(I32, (LANES, LANES), 0)
        dst = lax.broadcasted_iota(I32, (LANES, LANES), 1)
        for j in range(per_row):
            pick = src == j * ROT_HALF + dst % ROT_HALF
            spread = jnp.where(pick, 1.0, 0.0).astype(BF16)
            spread_neg = jnp.where(pick, jnp.where(dst < 2 * ROT_HALF, -1.0, 1.0), 0.0).astype(BF16)
            cos_s[pl.ds(j, tm // per_row, stride=per_row), :] = sum(_dot(p, spread) for p in cparts)
            sin_s[pl.ds(j, tm // per_row, stride=per_row), :] = sum(_dot(p, spread_neg) for p in sparts)

    li = (lax.broadcasted_iota(I32, (LANES, LANES), 0) // ROT_HALF) % 2
    lj = (lax.broadcasted_iota(I32, (LANES, LANES), 1) // ROT_HALF) % 2
    head_sum = jnp.where(li == lj, 1.0, 0.0).astype(BF16)
    outs = (o0_ref, o1_ref, o2_ref)

    def finish(gi, d, ti, mode, gain_ref, c, stage):
        col = ti * ATTN_WIDTH + gi * GROUP_WIDTH
        t = raw_r[:, col + c * LANES:col + (c + 1) * LANES]
        if mode != "v":
            ss = t * t
            hi = ss.astype(BF16)
            lo = (ss - hi.astype(F32)).astype(BF16)
            ms = (_dot(hi, head_sum) + _dot(lo, head_sum)) * (1.0 / HEAD_DIM)
            t = t * lax.rsqrt(ms + EPS) * gain_ref[...]
            t = t * cos_s[...] + pltpu.roll(t, 2 * ROT_HALF, 1) * sin_s[...]
            if mode == "q":
                t = t * Q_SCALE
        lanes = slice(ti * GROUP_WIDTH + c * LANES, ti * GROUP_WIDTH + (c + 1) * LANES)
        if d == 1:
            outs[gi][0, 0, :, lanes] = t.astype(BF16)
        else:
            stage_s[stage] = t
            for r in range(d):
                outs[gi][0, r, :, lanes] = stage_s[stage, pl.ds(r, tm // d, stride=d), :].astype(BF16)

    pieces = [rotary_tables]
    stage = 0
    for gi, (_, d) in enumerate(ATTN_GROUPS):
        for ti, (mode, gain_ref) in enumerate((("q", qg_ref), ("k", kg_ref), ("v", None))):
            for c in range(GROUP_WIDTH // LANES):
                pieces.append(functools.partial(finish, gi, d, ti, mode, gain_ref, c, stage))
                stage += d > 1

    chunks = list(range(0, w_ref.shape[1], GROUP_WIDTH))
    for k in range(max(len(chunks), len(pieces))):
        if k < len(chunks):
            project(chunks[k])
        if k < len(pieces):
            pieces[k]()


def _inproj(x2, g, w, pos2, invf, qg, kg, b, seq):
    t = x2.shape[0]
    n = w.shape[1]
    tm = TM_INPROJ
    nj = seq // tm
    n_tiles = b * nj
    n_stage = sum(1 for _, d in ATTN_GROUPS if d > 1) * 3 * (GROUP_WIDTH // LANES)
    proj_row = lambda s: (jnp.minimum(s, n_tiles - 1), 0)
    done_row = lambda s: (jnp.maximum(s - 1, 0), 0)
    done_blk = lambda s: (jnp.maximum(s - 1, 0) // nj, 0, jnp.maximum(s - 1, 0) % nj, 0)
    const = lambda s: (0, 0)
    per_row = LANES // ROT_HALF
    return pl.pallas_call(
        _inproj_kernel,
        grid=(n_tiles + 1,),
        in_specs=[
            pl.BlockSpec((tm, D_MODEL), proj_row),
            pl.BlockSpec((1, D_MODEL), const),
            pl.BlockSpec((D_MODEL, n), const, pipeline_mode=pl.Buffered(1)),
            pl.BlockSpec((tm // per_row, per_row), done_row),
            pl.BlockSpec((1, LANES), const),
            pl.BlockSpec((1, LANES), const),
            pl.BlockSpec((1, LANES), const),
        ],
        out_specs=[pl.BlockSpec((1, d, tm // d, ATTN_WIDTH), done_blk) for _, d in ATTN_GROUPS] + [
            pl.BlockSpec((tm, 2 * SGU_WIDTH), proj_row),
            pl.BlockSpec((tm, GATE_WIDTH), proj_row),
        ],
        out_shape=[jax.ShapeDtypeStruct((b, d, seq // d, ATTN_WIDTH), BF16) for _, d in ATTN_GROUPS] + [
            jax.ShapeDtypeStruct((t, 2 * SGU_WIDTH), BF16),
            jax.ShapeDtypeStruct((t, GATE_WIDTH), BF16),
        ],
        scratch_shapes=[
            pltpu.VMEM((tm, 3 * ATTN_WIDTH), F32),
            pltpu.VMEM((tm, 3 * ATTN_WIDTH), F32),
            pltpu.VMEM((n_stage, tm, LANES), F32),
            pltpu.VMEM((tm, LANES), F32),
            pltpu.VMEM((tm, LANES), F32),
        ],
        compiler_params=_params(56, ("arbitrary",)),
        name="inproj",
    )(x2, g, w, pos2, invf, qg, kg)


def _attn_blocks(qkv_ref, ores_s, lres_s, seq, d):
    sub = seq // d
    nb = sub // BLK
    win = 2 * BLK if nb > 1 else BLK
    lane = lax.broadcasted_iota(I32, (BLK, GROUP_WIDTH), 1)
    q_head = 2 * (lane // LANES) + (lane // ROT_HALF) % 2
    qi = lax.broadcasted_iota(I32, (BLK, win), 0)
    kj = lax.broadcasted_iota(I32, (BLK, win), 1)
    masks = {shift: lax.bitcast_convert_type(qi - kj + shift, jnp.uint32) <= jnp.uint32(BLK) for shift in (0, BLK)}

    for idx in range(seq // BLK):
        r, n = idx % d, idx // d
        row0 = r * sub + n * BLK
        kn = max(n - 1, 0) if nb > 1 else n
        k0 = r * sub + kn * BLK
        valid = masks[(n - kn) * BLK]
        q = qkv_ref[0, pl.ds(row0, BLK), 0:GROUP_WIDTH]
        kw = qkv_ref[0, pl.ds(k0, win), GROUP_WIDTH:2 * GROUP_WIDTH]
        vw = qkv_ref[0, pl.ds(k0, win), 2 * GROUP_WIDTH:3 * GROUP_WIDTH]
        zero = jnp.zeros_like(q)
        qm = jnp.concatenate([jnp.where(q_head == h, q, zero) for h in range(4)], axis=0)
        s = _dot_nt(qm, kw)
        ps, ms, ls = [], [], []
        for h in range(4):
            sh = jnp.where(valid, s[h * BLK:(h + 1) * BLK], NEG_INF)
            m = jnp.max(sh, axis=-1, keepdims=True)
            p = jnp.exp2(sh - m)
            ls.append(jnp.sum(p, axis=-1, keepdims=True))
            ms.append(m)
            ps.append(p.astype(BF16))
        pv = _dot(jnp.concatenate(ps, axis=0), vw)
        for h in range(4):
            lanes = slice(h * HEAD_DIM, (h + 1) * HEAD_DIM)
            ores_s[pl.ds(row0, BLK), lanes] = pv[h * BLK:(h + 1) * BLK, lanes] * (1.0 / ls[h])
            lres_s[pl.ds(row0, BLK), lanes] = jnp.broadcast_to((ms[h] + jnp.log2(ls[h])) * LN2, (BLK, HEAD_DIM))


def _attn_kernel(g0_ref, g1_ref, g2_ref, o_ref, ores_s, lres_s, onat_s, lnat_s, *, seq):
    for gi, ((_, d), qkv_ref) in enumerate(zip(ATTN_GROUPS, (g0_ref, g1_ref, g2_ref))):
        _attn_blocks(qkv_ref, ores_s, lres_s, seq, d)
        sub = seq // d
        for r in range(d):
            for c in range(GROUP_WIDTH // LANES):
                t = gi * (GROUP_WIDTH // LANES) + c
                onat_s[t, pl.ds(r, sub, stride=d), :] = ores_s[r * sub:(r + 1) * sub, c * LANES:(c + 1) * LANES]
                lnat_s[t, pl.ds(r, sub, stride=d), :] = lres_s[r * sub:(r + 1) * sub, c * LANES:(c + 1) * LANES]

    tiles = GROUP_WIDTH // LANES

    def body(i, carry):
        r0 = pl.multiple_of(i * ROW_CHUNK, ROW_CHUNK)
        for c in range(tiles):
            ls = [lnat_s[gi * tiles + c, pl.ds(r0, ROW_CHUNK), :] for gi in range(N_GROUPS)]
            m = jnp.maximum(jnp.maximum(ls[0], ls[1]), ls[2])
            es = [jnp.exp(l - m) for l in ls]
            inv = 1.0 / (es[0] + es[1] + es[2])
            for gi in range(N_GROUPS):
                t = gi * tiles + c
                o_ref[0, pl.ds(r0, ROW_CHUNK), t * LANES:(t + 1) * LANES] = (
                    onat_s[t, pl.ds(r0, ROW_CHUNK), :] * (es[gi] * inv)).astype(BF16)
        return carry

    lax.fori_loop(0, seq // ROW_CHUNK, body, 0, unroll=2)


def _attention(qkv_groups):
    b = qkv_groups[0].shape[0]
    seq = qkv_groups[0].shape[1] * qkv_groups[0].shape[2]
    tiles = GROUP_WIDTH // LANES
    blk = pl.BlockSpec((1, seq, ATTN_WIDTH), lambda i: (i, 0, 0))
    return pl.pallas_call(
        functools.partial(_attn_kernel, seq=seq),
        grid=(b,),
        in_specs=[blk] * N_GROUPS,
        out_specs=blk,
        out_shape=jax.ShapeDtypeStruct((b, seq, ATTN_WIDTH), BF16),
        scratch_shapes=[
            pltpu.VMEM((seq, GROUP_WIDTH), F32),
            pltpu.VMEM((seq, GROUP_WIDTH), F32),
            pltpu.VMEM((N_GROUPS * tiles, seq, LANES), F32),
            pltpu.VMEM((N_GROUPS * tiles, seq, LANES), F32),
        ],
        compiler_params=_params(48, ("arbitrary",)),
        name="attention",
    )(*[a.reshape(b, seq, ATTN_WIDTH) for a in qkv_groups])


def _post_kernel(attn_ref, u_ref, vz_ref, ga_ref, gb_ref, x_ref, wa_ref, wb_ref, wo_ref,
                 lng_ref, lnb_ref, wcat_ref, bcat_ref, n2g_ref, wrh_ref, wrl_ref, br_ref,
                 x1_ref, hf_ref, e_ref, g_ref, cst_ref, cnt_ref, sgu_s, carry_s, merged_a, merged_b):
    step = pl.program_id(0)

    @pl.when(step == 0)
    def _():
        carry_s[...] = jnp.zeros_like(carry_s)
        merged_b[...] = jnp.zeros_like(merged_b)

    args = (attn_ref, u_ref, vz_ref, ga_ref, gb_ref, x_ref, wa_ref, wb_ref, wo_ref, lng_ref, lnb_ref, wcat_ref,
            bcat_ref, n2g_ref, wrh_ref, wrl_ref, br_ref, x1_ref, hf_ref, e_ref, g_ref, cst_ref, cnt_ref, sgu_s, carry_s)

    @pl.when(lax.rem(step, 2) == 0)
    def _():
        _post_step(*args, merged_a, merged_b)

    @pl.when(lax.rem(step, 2) == 1)
    def _():
        _post_step(*args, merged_b, merged_a)


def _post_step(attn_ref, u_ref, vz_ref, ga_ref, gb_ref, x_ref, wa_ref, wb_ref, wo_ref,
               lng_ref, lnb_ref, wcat_ref, bcat_ref, n2g_ref, wrh_ref, wrl_ref, br_ref,
               x1_ref, hf_ref, e_ref, g_ref, cst_ref, cnt_ref, sgu_s, carry_s, merged_w, merged_r):
    tm = x_ref.shape[0]
    counted = jnp.where(pl.program_id(0) > 0, 1.0, 0.0)

    lane = lax.broadcasted_iota(I32, (SGU_CHUNK, LANES), 1)
    low = lane < SGU_GROUP_DIM
    trow = lax.broadcasted_iota(I32, (SGU_CHUNK, 2 * SGU_CHUNK), 0)
    tcol = lax.broadcasted_iota(I32, (SGU_CHUNK, 2 * SGU_CHUNK), 1) % SGU_CHUNK
    causal = tcol <= trow

    def chunk(c):
        r0 = c * SGU_CHUNK
        u = _gelu(u_ref[pl.ds(r0, SGU_CHUNK), :].astype(F32))
        v = _gelu(vz_ref[pl.ds(r0, SGU_CHUNK), :].astype(F32))
        mu = jnp.mean(v, axis=-1, keepdims=True)
        vc = v - mu
        vn = vc * lax.rsqrt(jnp.mean(vc * vc, axis=-1, keepdims=True) + EPS)
        vn = (vn * lng_ref[...] + lnb_ref[...]).astype(BF16)
        zero = jnp.zeros((SGU_CHUNK, LANES), BF16)
        for j in range(SGU_WIDTH // LANES):
            vt = vn[:, j * LANES:(j + 1) * LANES]
            rhs = jnp.concatenate([jnp.where(low, vt, zero), jnp.where(low, zero, vt)], axis=0)
            wj = jnp.where(causal, wcat_ref[j], jnp.zeros((), BF16))
            mixed = _dot(wj, rhs) + bcat_ref[j]
            sgu_s[pl.ds(r0, SGU_CHUNK), j * LANES:(j + 1) * LANES] = (u[:, j * LANES:(j + 1) * LANES] * mixed).astype(BF16)

    eio = lax.broadcasted_iota(I32, (N_EXPERTS, tm), 0)
    state = {}
    top_v, onehots = [], []

    def route_logits(hi, lo):
        state["work"] = (_dot_nt(wrh_ref[...], hi) + _dot_nt(wrl_ref[...], hi) + _dot_nt(wrh_ref[...], lo)
                         + br_ref[...])

    def route_pick(k):
        work = state["work"]
        m = jnp.max(work, axis=0, keepdims=True)
        idx = jnp.min(jnp.where(work == m, eio, N_EXPERTS), axis=0, keepdims=True)
        oh = eio == idx
        state["work"] = jnp.where(oh, -jnp.inf, work)
        top_v.append(m)
        onehots.append(oh)
        e_ref[k:k + 1, :] = idx

    def route_finish():
        ex = [jnp.exp(v - top_v[0]) for v in top_v]
        inv = 1.0 / (ex[0] + ex[1] + ex[2] + ex[3])
        for k in range(TOP_K):
            g_ref[k:k + 1, :] = ex[k] * inv
        sel = jnp.zeros((N_EXPERTS, tm), F32)
        for oh in onehots:
            sel = jnp.where(oh, counted, sel)
        lane_e = lax.broadcasted_iota(I32, (N_EXPERTS, LANES), 1)
        run = carry_s[...]
        cst = jnp.zeros((N_EXPERTS, LANES), F32)
        for s in range(tm // TM_ROUTE):
            cst = jnp.where(lane_e == s, run, cst)
            run = run + jnp.sum(sel[:, s * TM_ROUTE:(s + 1) * TM_ROUTE], axis=1, keepdims=True)
        cst_ref[0] = cst.astype(I32)
        carry_s[...] = run
        cnt_ref[...] = run.astype(I32)

    n_parts = tm // SGU_CHUNK
    width = D_MODEL // n_parts
    x1_parts, y_a_parts = [], []
    for c in range(n_parts):
        cols = slice(c * width, (c + 1) * width)
        chunk(c)
        x1_parts.append(x_ref[:, cols] + _dot(merged_r[...], wo_ref[:, cols]))
        y_a_parts.append(_dot(attn_ref[...], wa_ref[:, cols]))
    x1 = jnp.concatenate(x1_parts, axis=1)
    x1_ref[...] = x1
    hf = x1 * lax.rsqrt(jnp.mean(x1 * x1, axis=-1, keepdims=True) + EPS) * n2g_ref[...]
    hi = hf.astype(BF16)
    hf_ref[...] = hi
    route_logits(hi, (hf - hi.astype(F32)).astype(BF16))
    for c in range(n_parts):
        cols = slice(c * width, (c + 1) * width)
        y_b = _dot(sgu_s[...], wb_ref[:, cols])
        merged_w[:, cols] = (jax.nn.sigmoid(ga_ref[:, cols].astype(F32)) * y_a_parts[c]
                             + jax.nn.sigmoid(gb_ref[:, cols].astype(F32)) * y_b).astype(BF16)
        if c < TOP_K:
            route_pick(c)
    for k in range(n_parts, TOP_K):
        route_pick(k)
    route_finish()


def _post(attn2, z2, gates2, x2, wa, wb, wo, lng, lnb, wcat, bcat, n2g, wrh, wrl, br):
    t = x2.shape[0]
    tm = TM_POST
    const = lambda *shape: pl.BlockSpec(shape, lambda i: (0,) * len(shape), pipeline_mode=pl.Buffered(1))
    n_tiles = t // tm
    cur = lambda i: jnp.minimum(i, n_tiles - 1)
    prev = lambda i: jnp.maximum(i - 1, 0)
    return pl.pallas_call(
        _post_kernel,
        grid=(n_tiles + 1,),
        in_specs=[
            pl.BlockSpec((tm, ATTN_WIDTH), lambda i: (cur(i), 0)),
            pl.BlockSpec((tm, SGU_WIDTH), lambda i: (cur(i), 0)),
            pl.BlockSpec((tm, SGU_WIDTH), lambda i: (cur(i), 1)),
            pl.BlockSpec((tm, D_MODEL), lambda i: (cur(i), 0)),
            pl.BlockSpec((tm, D_MODEL), lambda i: (cur(i), 1)),
            pl.BlockSpec((tm, D_MODEL), lambda i: (prev(i), 0)),
            const(ATTN_WIDTH, D_MODEL),
            const(SGU_WIDTH, D_MODEL),
            const(D_MODEL, D_MODEL),
            const(1, SGU_WIDTH),
            const(1, SGU_WIDTH),
            const(SGU_WIDTH // LANES, SGU_CHUNK, 2 * SGU_CHUNK),
            const(SGU_WIDTH // LANES, SGU_CHUNK, LANES),
            const(1, D_MODEL),
            const(N_EXPERTS, D_MODEL),
            const(N_EXPERTS, D_MODEL),
            const(N_EXPERTS, 1),
        ],
        out_specs=[
            pl.BlockSpec((tm, D_MODEL), lambda i: (prev(i), 0)),
            pl.BlockSpec((tm, D_MODEL), lambda i: (prev(i), 0)),
            pl.BlockSpec((TOP_K, tm), lambda i: (0, prev(i))),
            pl.BlockSpec((TOP_K, tm), lambda i: (0, prev(i))),
            pl.BlockSpec((1, N_EXPERTS, LANES), lambda i: (prev(i), 0, 0)),
            pl.BlockSpec((N_EXPERTS, LANES), lambda i: (0, 0)),
        ],
        out_shape=[
            jax.ShapeDtypeStruct((t, D_MODEL), F32),
            jax.ShapeDtypeStruct((t, D_MODEL), BF16),
            jax.ShapeDtypeStruct((TOP_K, t), I32),
            jax.ShapeDtypeStruct((TOP_K, t), F32),
            jax.ShapeDtypeStruct((t // tm, N_EXPERTS, LANES), I32),
            jax.ShapeDtypeStruct((N_EXPERTS, LANES), I32),
        ],
        scratch_shapes=[
            pltpu.VMEM((tm, SGU_WIDTH), BF16),
            pltpu.VMEM((N_EXPERTS, LANES), F32),
        ] + [pltpu.VMEM((tm, D_MODEL), BF16)] * 2,
        compiler_params=_params(48, ("arbitrary",)),
        name="post",
    )(attn2, z2, z2, gates2, gates2, x2, wa, wb, wo, lng, lnb, wcat, bcat, n2g, wrh, wrl, br)


def _route_rows(e_ref, seen, tm):
    e = e_ref[...]
    eio = lax.broadcasted_iota(I32, (N_EXPERTS, tm), 0)
    onehots = [eio == e[k:k + 1, :] for k in range(TOP_K)]
    sel = jnp.zeros((N_EXPERTS, tm), F32)
    for oh in onehots:
        sel = jnp.where(oh, 1.0, sel)
    ti = lax.broadcasted_iota(I32, (tm, tm), 0)
    tj = lax.broadcasted_iota(I32, (tm, tm), 1)
    before = jnp.where(ti < tj, 1.0, 0.0).astype(BF16)
    slot = _dot(sel.astype(BF16), before)
    cnt = jnp.sum(sel, axis=1, keepdims=True)
    head = seen - SUBLANES * jnp.floor(seen * (1.0 / SUBLANES))
    tiles = jnp.where(cnt > 0.0, jnp.floor((head + cnt + (SUBLANES - 1)) * (1.0 / SUBLANES)), 0.0)
    xi = lax.broadcasted_iota(I32, (N_EXPERTS, N_EXPERTS), 0)
    xj = lax.broadcasted_iota(I32, (N_EXPERTS, N_EXPERTS), 1)
    lower = jnp.where(xj < xi, 1.0, 0.0).astype(BF16)
    off = _dot(lower, jnp.broadcast_to(tiles, (N_EXPERTS, LANES)).astype(BF16))[:, 0:1] * float(SUBLANES)
    row = off + head + slot
    return [jnp.sum(jnp.where(oh, row, 0.0), axis=0, keepdims=True) for oh in onehots], cnt


def _pick(rows, values, default):
    tm = rows[0].shape[1]
    rio = lax.broadcasted_iota(I32, (ROUTE_ROWS, tm), 0).astype(F32)
    out = default
    for r, v in zip(rows, values):
        out = jnp.where(rio == r, v, out)
    return out


def _wait_rows(buf, hbm_ref, sem, n):
    n = pl.multiple_of(n, SUBLANES)

    @pl.when(n > 0)
    def _():
        pltpu.make_async_copy(buf.at[pl.ds(0, n), :], hbm_ref.at[pl.ds(0, n), :], sem).wait()


def _dispatch_kernel(start_ref, off_ref, staged_ref, full_ref, tot_ref, flush_ref, za_ref, zb_ref, hf_ref, e_ref,
                     xs_ref, rows_ref, obuf, zbuf, seen_s, tail_s, first_s, sem, semz):
    tm = hf_ref.shape[0]
    i = pl.program_id(0)
    last = pl.num_programs(0) - 1
    cur = lax.rem(i, 2)
    spare = ROUTE_ROWS - SUBLANES

    @pl.when(i == 0)
    def _():
        seen_s[...] = jnp.zeros_like(seen_s)
        tail_s[...] = jnp.zeros_like(tail_s)

    @pl.when(i >= 2)
    def _():
        _wait_rows(obuf.at[cur], xs_ref, sem.at[cur], tot_ref[jnp.maximum(i - 2, 0)])

    rows, cnt = _route_rows(e_ref, seen_s[:, 0:1], tm)
    seen_s[...] = seen_s[...] + cnt
    for k in range(TOP_K):
        rows_ref[k:k + 1, :] = rows[k]
    pt = _pick(rows, [1.0] * TOP_K, 0.0).astype(BF16)
    obuf[cur] = _dot(pt, hf_ref[...])

    for x in range(N_EXPERTS):
        k = i * N_EXPERTS + x
        staged, full = staged_ref[k], full_ref[k]
        has = jnp.where(staged > 0, 1.0, 0.0)
        partial = jnp.where(staged > full, 1.0, 0.0)
        alone = jnp.where(full == 0, 1.0, 0.0)
        first = pl.multiple_of(jnp.where(staged > 0, off_ref[k], spare), SUBLANES)
        rest = pl.multiple_of(jnp.where(staged > 0, off_ref[k] + full, spare), SUBLANES)
        kept = tail_s[x]
        merged = obuf[cur, pl.ds(first, SUBLANES), :] + kept * has
        first_s[cur, x] = merged
        ending = merged * alone + obuf[cur, pl.ds(rest, SUBLANES), :] * (1.0 - alone)
        tail_s[x] = ending * partial + kept * (1.0 - has)

    for x in range(N_EXPERTS):
        n = pl.multiple_of(full_ref[i * N_EXPERTS + x], SUBLANES)
        dst = pl.multiple_of(start_ref[i * N_EXPERTS + x], SUBLANES)

        @pl.when(n > 0)
        def _(x=x, dst=dst):
            pltpu.make_async_copy(first_s.at[cur, x], xs_ref.at[pl.ds(dst, SUBLANES), :], sem.at[cur]).start()

        @pl.when(n > SUBLANES)
        def _(x=x, n=n, dst=dst):
            src = pl.multiple_of(off_ref[i * N_EXPERTS + x] + SUBLANES, SUBLANES)
            more = pl.multiple_of(n - SUBLANES, SUBLANES)
            pltpu.make_async_copy(obuf.at[cur, pl.ds(src, more), :],
                                  xs_ref.at[pl.ds(pl.multiple_of(dst + SUBLANES, SUBLANES), more), :], sem.at[cur]).start()

    @pl.when(i == last)
    def _():
        @pl.when(i >= 1)
        def _():
            _wait_rows(obuf.at[1 - cur], xs_ref, sem.at[1 - cur], tot_ref[jnp.maximum(i - 1, 0)])

        _wait_rows(obuf.at[cur], xs_ref, sem.at[cur], tot_ref[i])
        for x in range(N_EXPERTS):
            row = flush_ref[x]

            @pl.when(row >= 0)
            def _(x=x, row=row):
                cp = pltpu.make_async_copy(tail_s.at[x], xs_ref.at[pl.ds(pl.multiple_of(row, SUBLANES), SUBLANES), :], semz)
                cp.start()
                cp.wait()
        zbuf[...] = jnp.zeros_like(zbuf)

        def zero_region(z, act):
            n = zb_ref[z] - za_ref[z]
            big = n // EXPERT_TILE
            rest = pl.multiple_of(n - big * EXPERT_TILE, SUBLANES)

            def piece(row, nrows):
                return pltpu.make_async_copy(zbuf.at[pl.ds(0, nrows), :],
                                             xs_ref.at[pl.ds(pl.multiple_of(row, SUBLANES), nrows), :], semz)

            lax.fori_loop(0, big, lambda m, c: (act(piece(za_ref[z] + m * EXPERT_TILE, EXPERT_TILE)), c)[1], 0)

            @pl.when(rest > 0)
            def _():
                act(piece(za_ref[z] + big * EXPERT_TILE, rest))

        for z in range(N_EXPERTS + 1):
            zero_region(z, lambda cp: cp.start())
        for z in range(N_EXPERTS + 1):
            zero_region(z, lambda cp: cp.wait())


def _dispatch(start, off, staged, full, tot, flush, za, zb, hf, e4, cap):
    t = hf.shape[0]
    tm = TM_ROUTE
    tok = lambda i, *_: (0, i)
    return pl.pallas_call(
        _dispatch_kernel,
        grid_spec=pltpu.PrefetchScalarGridSpec(
            num_scalar_prefetch=8,
            grid=(t // tm,),
            in_specs=[
                pl.BlockSpec((tm, D_MODEL), lambda i, *_: (i, 0)),
                pl.BlockSpec((TOP_K, tm), tok),
            ],
            out_specs=[pl.BlockSpec(memory_space=pl.ANY), pl.BlockSpec((TOP_K, tm), tok)],
            scratch_shapes=[
                pltpu.VMEM((2, ROUTE_ROWS, D_MODEL), F32),
                pltpu.VMEM((EXPERT_TILE, D_MODEL), F32),
                pltpu.VMEM((N_EXPERTS, LANES), F32),
                pltpu.VMEM((N_EXPERTS, SUBLANES, D_MODEL), F32),
                pltpu.VMEM((2, N_EXPERTS, SUBLANES, D_MODEL), F32),
                pltpu.SemaphoreType.DMA((2,)),
                pltpu.SemaphoreType.DMA(()),
            ],
        ),
        out_shape=[jax.ShapeDtypeStruct((cap, D_MODEL), F32), jax.ShapeDtypeStruct((TOP_K, t), F32)],
        compiler_params=_params(48, ("arbitrary",)),
        name="dispatch",
    )(start, off, staged, full, tot, flush, za, zb, hf, e4)


def _expert_kernel(te_ref, nu_ref, xs_ref, wgu_ref, bgu_ref, wdn_ref, bdn_ref, ys_ref, wgu_s, wdn_s):
    i = pl.program_id(0)

    @pl.when(i < nu_ref[0])
    def _():
        @pl.when((i == 0) | (te_ref[i] != te_ref[jnp.maximum(i - 1, 0)]))
        def _():
            wgu_s[...] = wgu_ref[0].astype(BF16)
            wdn_s[...] = wdn_ref[0].astype(BF16)

        x = xs_ref[...].astype(BF16)
        gu = _dot(x, wgu_s[...]) + bgu_ref[0]
        gate = jnp.minimum(gu[:, :D_EXPERT], SWIGLU_LIMIT)
        up = jnp.clip(gu[:, D_EXPERT:], -SWIGLU_LIMIT, SWIGLU_LIMIT)
        act = (up + 1.0) * (gate * jax.nn.sigmoid(SWIGLU_ALPHA * gate))
        ys_ref[...] = _dot(act.astype(BF16), wdn_s[...]) + bdn_ref[0]

    @pl.when(i >= nu_ref[0])
    def _():
        ys_ref[...] = jnp.zeros_like(ys_ref)


def _experts(tile_expert, n_used, xs, wgu, bgu, wdn, bdn):
    cap = xs.shape[0]
    n_tiles = cap // EXPERT_TILE
    row = lambda i, te, nu: (jnp.minimum(i, nu[0] - 1), 0)
    by_expert = lambda i, te, nu: (te[i], 0, 0)
    return pl.pallas_call(
        _expert_kernel,
        grid_spec=pltpu.PrefetchScalarGridSpec(
            num_scalar_prefetch=2,
            grid=(n_tiles,),
            in_specs=[
                pl.BlockSpec((EXPERT_TILE, D_MODEL), row),
                pl.BlockSpec((1, D_MODEL, 2 * D_EXPERT), by_expert),
                pl.BlockSpec((1, 1, 2 * D_EXPERT), by_expert),
                pl.BlockSpec((1, D_EXPERT, D_MODEL), by_expert),
                pl.BlockSpec((1, 1, D_MODEL), by_expert),
            ],
            out_specs=pl.BlockSpec((EXPERT_TILE, D_MODEL), lambda i, te, nu: (i, 0)),
            scratch_shapes=[
                pltpu.VMEM((D_MODEL, 2 * D_EXPERT), BF16),
                pltpu.VMEM((D_EXPERT, D_MODEL), BF16),
            ],
        ),
        out_shape=jax.ShapeDtypeStruct((cap, D_MODEL), F32),
        compiler_params=_params(56, ("arbitrary",)),
        name="experts",
    )(tile_expert, n_used, xs, wgu, bgu, wdn, bdn)


def _combine_kernel(start_ref, off_ref, cnt_ref, tot_ref, x1_ref, rows_ref, g_ref, ys_ref, o_ref, sbuf, sem):
    i = pl.program_id(0)
    last = pl.num_programs(0) - 1
    cur = lax.rem(i, 2)

    def fetch(step, b):
        for x in range(N_EXPERTS):
            n = pl.multiple_of(cnt_ref[step * N_EXPERTS + x], SUBLANES)

            @pl.when(n > 0)
            def _(x=x, n=n):
                src = pl.multiple_of(start_ref[step * N_EXPERTS + x], SUBLANES)
                dst = pl.multiple_of(off_ref[step * N_EXPERTS + x], SUBLANES)
                pltpu.make_async_copy(ys_ref.at[pl.ds(src, n), :], sbuf.at[b, pl.ds(dst, n), :], sem.at[b]).start()

    @pl.when(i == 0)
    def _():
        sbuf[...] = jnp.zeros_like(sbuf)
        fetch(i, cur)

    @pl.when(i < last)
    def _():
        fetch(i + 1, 1 - cur)

    rows = [rows_ref[k:k + 1, :] for k in range(TOP_K)]
    pt = _pick(rows, [1.0] * TOP_K, 0.0).astype(BF16)
    g = g_ref[...]
    gate = jnp.sum(_pick(rows, [g[k:k + 1, :] for k in range(TOP_K)], 0.0), axis=1, keepdims=True)
    _wait_rows(sbuf.at[cur], ys_ref, sem.at[cur], tot_ref[i])
    weighted = (sbuf[cur] * gate).astype(BF16)
    picked = lax.dot_general(pt, weighted, _TN, preferred_element_type=F32)
    o_ref[...] = x1_ref[...] + picked


def _combine(start, off, cnt, tot, x1, rows4, g4, ys):
    t = x1.shape[0]
    tm = TM_ROUTE
    return pl.pallas_call(
        _combine_kernel,
        grid_spec=pltpu.PrefetchScalarGridSpec(
            num_scalar_prefetch=4,
            grid=(t // tm,),
            in_specs=[
                pl.BlockSpec((tm, D_MODEL), lambda i, *_: (i, 0)),
                pl.BlockSpec((TOP_K, tm), lambda i, *_: (0, i)),
                pl.BlockSpec((TOP_K, tm), lambda i, *_: (0, i)),
                pl.BlockSpec(memory_space=pl.ANY),
            ],
            out_specs=pl.BlockSpec((tm, D_MODEL), lambda i, *_: (i, 0)),
            scratch_shapes=[
                pltpu.VMEM((2, ROUTE_ROWS, D_MODEL), F32),
                pltpu.SemaphoreType.DMA((2,)),
            ],
        ),
        out_shape=jax.ShapeDtypeStruct((t, D_MODEL), F32),
        compiler_params=_params(48, ("arbitrary",)),
        name="combine",
    )(start, off, cnt, tot, x1, rows4, g4, ys)


def _layer(x, positions, norm1_g, w_in, q_norm_g, k_norm_g, sgu_ln_g, sgu_ln_b, w_spatial, b_spatial,
           w_branch_a, w_branch_b, w_out, norm2_g, w_router, b_router, w_gate_up, b_gate_up, w_down, b_down):
    b, seq, _ = x.shape
    t = b * seq
    x2 = x.reshape(t, D_MODEL)

    inv_freq = ROPE_THETA ** (-jnp.arange(ROT_HALF, dtype=F32) / ROT_HALF)
    invf = jnp.tile(inv_freq, LANES // ROT_HALF).reshape(1, LANES)
    dim_of_lane = [(l // (2 * ROT_HALF)) * ROT_HALF + l % ROT_HALF for l in range(LANES)]
    qg = q_norm_g[jnp.array(dim_of_lane)].reshape(1, LANES)
    kg = k_norm_g[jnp.array(dim_of_lane)].reshape(1, LANES)
    w_bf = w_in.astype(BF16)
    qk = w_bf[:, :2 * ATTN_WIDTH].reshape(D_MODEL, 2 * ATTN_WIDTH // LANES, 2, 2, ROT_HALF)
    qk = qk.transpose(0, 1, 3, 2, 4).reshape(D_MODEL, 2 * ATTN_WIDTH)
    w_in_p = jnp.concatenate([qk, w_bf[:, 2 * ATTN_WIDTH:]], axis=1)
    n_pairs = SGU_WIDTH // LANES
    wcat = w_spatial.reshape(n_pairs, 2, SGU_CHUNK, SGU_CHUNK).transpose(0, 2, 1, 3).reshape(n_pairs, SGU_CHUNK, 2 * SGU_CHUNK)
    bcat = jnp.repeat(b_spatial.reshape(n_pairs, 2, SGU_CHUNK).transpose(0, 2, 1), SGU_GROUP_DIM, axis=2)
    wr_t = w_router.T
    wr_hi = wr_t.astype(BF16)
    wr_lo = (wr_t - wr_hi.astype(F32)).astype(BF16)

    *qkv_groups, z, gates = _inproj(x2, norm1_g.reshape(1, D_MODEL), w_in_p, positions.reshape(-1, LANES // ROT_HALF), invf, qg, kg, b, seq)
    attn = _attention(qkv_groups)
    x1, hf, e4, g4, cst, counts = _post(
        attn.reshape(t, ATTN_WIDTH), z, gates, x2,
        w_branch_a.astype(BF16), w_branch_b.astype(BF16), w_out.astype(BF16),
        sgu_ln_g.reshape(1, SGU_WIDTH), sgu_ln_b.reshape(1, SGU_WIDTH), wcat.astype(BF16), bcat,
        norm2_g.reshape(1, D_MODEL), wr_hi, wr_lo, b_router.reshape(N_EXPERTS, 1))

    steps_per_tile = TM_POST // TM_ROUTE
    n_steps = t // TM_ROUTE
    before = cst[:, :, :steps_per_tile].transpose(0, 2, 1).reshape(n_steps, N_EXPERTS)
    total = counts[:, 0]
    cnt = jnp.concatenate([before[1:], total[None, :]], axis=0) - before
    head = before % SUBLANES
    staged = jnp.where(cnt > 0, (head + cnt + SUBLANES - 1) // SUBLANES * SUBLANES, 0)
    full = jnp.where(cnt > 0, (head + cnt) // SUBLANES * SUBLANES, 0)
    off = (jnp.cumsum(staged, axis=1) - staged).astype(I32).reshape(-1)
    tot_staged = jnp.sum(staged, axis=1).astype(I32)
    tot_full = jnp.sum(full, axis=1).astype(I32)
    region = (total + EXPERT_TILE - 1) // EXPERT_TILE * EXPERT_TILE
    rend = jnp.cumsum(region)
    rstart = rend - region
    start = (rstart[None, :] + before - head).astype(I32).reshape(-1)
    bound = t * TOP_K + N_EXPERTS * (EXPERT_TILE - 1)
    cap = (bound + EXPERT_TILE - 1) // EXPERT_TILE * EXPERT_TILE
    n_tiles = cap // EXPERT_TILE
    tile_expert = jnp.sum(jnp.arange(n_tiles, dtype=I32)[:, None] * EXPERT_TILE >= rend[None, :], axis=1)
    tile_expert = jnp.minimum(tile_expert, N_EXPERTS - 1).astype(I32)
    n_used = (rend[-1:] // EXPERT_TILE).astype(I32)
    filled = (total + SUBLANES - 1) // SUBLANES * SUBLANES
    flush = jnp.where(total % SUBLANES != 0, rstart + total // SUBLANES * SUBLANES, -1).astype(I32)
    za = jnp.concatenate([rstart + filled, rend[-1:]]).astype(I32)
    zb = jnp.concatenate([rend, jnp.full((1,), cap, rend.dtype)]).astype(I32)
    staged = staged.astype(I32).reshape(-1)
    full = full.astype(I32).reshape(-1)

    xs, rows4 = _dispatch(start, off, staged, full, tot_full, flush, za, zb, hf, e4, cap)
    ys = _experts(tile_expert, n_used, xs,
                  w_gate_up, b_gate_up.reshape(N_EXPERTS, 1, 2 * D_EXPERT),
                  w_down, b_down.reshape(N_EXPERTS, 1, D_MODEL))
    out = _combine(start, off, staged, tot_staged, x1, rows4, g4, ys)
    return out.reshape(b, seq, D_MODEL)


def kernel(x, positions, norm1_g, w_in, q_norm_g, k_norm_g, sgu_ln_g, sgu_ln_b, w_spatial, b_spatial, w_branch_a, w_branch_b, w_out, norm2_g, w_router, b_router, w_gate_up, b_gate_up, w_down, b_down):
    for layer in range(norm1_g.shape[0]):
        x = _layer(x, positions, norm1_g[layer], w_in[layer], q_norm_g[layer], k_norm_g[layer],
                   sgu_ln_g[layer], sgu_ln_b[layer], w_spatial[layer], b_spatial[layer],
                   w_branch_a[layer], w_branch_b[layer], w_out[layer], norm2_g[layer],
                   w_router[layer], b_router[layer], w_gate_up[layer], b_gate_up[layer],
                   w_down[layer], b_down[layer])
    return x
```

```python
import functools

import jax
import jax.numpy as jnp
from jax import lax
from jax.experimental import pallas as pl
from jax.experimental.pallas import tpu as pltpu

F32 = jnp.float32
BF16 = jnp.bfloat16
I32 = jnp.int32

D_MODEL = 1024
HEAD_DIM = 64
ROT_HALF = HEAD_DIM // 2
LOG2E = 1.4426950408889634
LN2 = 0.6931471805599453
Q_SCALE = HEAD_DIM ** -0.5 * LOG2E
ATTN_GROUPS = ((128, 1), (512, 4), (2048, 16))
N_GROUPS = len(ATTN_GROUPS)
GROUP_WIDTH = 256
ATTN_WIDTH = N_GROUPS * GROUP_WIDTH
BLK = 128
ROPE_THETA = 10000.0
SGU_CHUNK = 128
SGU_GROUP_DIM = 64
SGU_WIDTH = 768
GATE_WIDTH = 2 * D_MODEL
N_EXPERTS = 32
TOP_K = 4
D_EXPERT = D_MODEL
SWIGLU_LIMIT = 7.0
SWIGLU_ALPHA = 1.702
EPS = 1e-6
NEG_INF = -1e30
LANES = 128

TM_INPROJ = 512
TM_POST = 512
TM_ROUTE = 256
SUBLANES = 8
ROUTE_ROWS = TM_ROUTE * TOP_K + N_EXPERTS * 2 * SUBLANES
EXPERT_TILE = 512
ROW_CHUNK = 256

_NT = (((1,), (1,)), ((), ()))
_TN = (((0,), (0,)), ((), ()))


def _dot(a, b):
    return jnp.dot(a, b, preferred_element_type=F32)


def _dot_nt(a, b):
    return lax.dot_general(a, b, _NT, preferred_element_type=F32)


def _gelu(x):
    return 0.5 * x * (1.0 + lax.erf(x * 0.7071067811865476))


def _params(vmem_mb, sem):
    return pltpu.CompilerParams(dimension_semantics=sem, vmem_limit_bytes=vmem_mb * 1024 * 1024)


def _split3(x):
    a = x.astype(BF16)
    r = x - a.astype(F32)
    b = r.astype(BF16)
    return a, b, (r - b.astype(F32)).astype(BF16)


def _inproj_kernel(x_ref, g_ref, w_ref, pos_ref, invf_ref, qg_ref, kg_ref,
                   o0_ref, o1_ref, o2_ref, oz_ref, og_ref, raw_a, raw_b, stage_s, cos_s, sin_s):
    step = pl.program_id(0)

    @pl.when(step == 0)
    def _():
        raw_b[...] = jnp.zeros_like(raw_b)

    args = (x_ref, g_ref, w_ref, pos_ref, invf_ref, qg_ref, kg_ref, o0_ref, o1_ref, o2_ref, oz_ref, og_ref)

    @pl.when(lax.rem(step, 2) == 0)
    def _():
        _inproj_step(*args, raw_a, raw_b, stage_s, cos_s, sin_s)

    @pl.when(lax.rem(step, 2) == 1)
    def _():
        _inproj_step(*args, raw_b, raw_a, stage_s, cos_s, sin_s)


def _inproj_step(x_ref, g_ref, w_ref, pos_ref, invf_ref, qg_ref, kg_ref,
                 o0_ref, o1_ref, o2_ref, oz_ref, og_ref, raw_w, raw_r, stage_s, cos_s, sin_s):
    tm = x_ref.shape[0]
    x = x_ref[...]
    y = x * lax.rsqrt(jnp.mean(x * x, axis=-1, keepdims=True) + EPS)
    h = (y * g_ref[...]).astype(BF16)

    def project(lo):
        p = _dot(h, w_ref[:, lo:lo + GROUP_WIDTH])
        zc = 3 * ATTN_WIDTH
        gc = zc + 2 * SGU_WIDTH
        if lo < zc:
            raw_w[:, lo:lo + GROUP_WIDTH] = p
        elif lo < gc:
            oz_ref[:, lo - zc:lo - zc + GROUP_WIDTH] = p.astype(BF16)
        else:
            og_ref[:, lo - gc:lo - gc + GROUP_WIDTH] = p.astype(BF16)

    def rotary_tables():
        per_row = LANES // ROT_HALF
        lane = lax.broadcasted_iota(I32, (tm // per_row, LANES), 1)
        p4 = pos_ref[...].astype(F32)
        posd = p4[:, per_row - 1:per_row]
        for j in range(per_row - 2, -1, -1):
            posd = jnp.where(lane < (j + 1) * ROT_HALF, p4[:, j:j + 1], posd)
        ang = posd * invf_ref[...]
        cparts = _split3(jnp.cos(ang))
        sparts = _split3(jnp.sin(ang))
        src = lax.broadcasted_iota(I32, (LANES, LANES), 0)
        dst = lax.broadcasted_iota(I32, (LANES, LANES), 1)
        for j in range(per_row):
            pick = src == j * ROT_HALF + dst % ROT_HALF
            spread = jnp.where(pick, 1.0, 0.0).astype(BF16)
            spread_neg = jnp.where(pick, jnp.where(dst < 2 * ROT_HALF, -1.0, 1.0), 0.0).astype(BF16)
            cos_s[pl.ds(j, tm // per_row, stride=per_row), :] = sum(_dot(p, spread) for p in cparts)
            sin_s[pl.ds(j, tm // per_row, stride=per_row), :] = sum(_dot(p, spread_neg) for p in sparts)

    li = (lax.broadcasted_iota(I32, (LANES, LANES), 0) // ROT_HALF) % 2
    lj = (lax.broadcasted_iota(I32, (LANES, LANES), 1) // ROT_HALF) % 2
    head_sum = jnp.where(li == lj, 1.0, 0.0).astype(BF16)
    outs = (o0_ref, o1_ref, o2_ref)

    def finish(gi, d, ti, mode, gain_ref, c, stage):
        col = ti * ATTN_WIDTH + gi * GROUP_WIDTH
        t = raw_r[:, col + c * LANES:col + (c + 1) * LANES]
        if mode != "v":
            ss = t * t
            hi = ss.astype(BF16)
            lo = (ss - hi.astype(F32)).astype(BF16)
            ms = (_dot(hi, head_sum) + _dot(lo, head_sum)) * (1.0 / HEAD_DIM)
            t = t * lax.rsqrt(ms + EPS) * gain_ref[...]
            t = t * cos_s[...] + pltpu.roll(t, 2 * ROT_HALF, 1) * sin_s[...]
            if mode == "q":
                t = t * Q_SCALE
        lanes = slice(ti * GROUP_WIDTH + c * LANES, ti * GROUP_WIDTH + (c + 1) * LANES)
        if d == 1:
            outs[gi][0, 0, :, lanes] = t.astype(BF16)
        else:
            stage_s[stage] = t
            for r in range(d):
                outs[gi][0, r, :, lanes] = stage_s[stage, pl.ds(r, tm // d, stride=d), :].astype(BF16)

    pieces = [rotary_tables]
    stage = 0
    for gi, (_, d) in enumerate(ATTN_GROUPS):
        for ti, (mode, gain_ref) in enumerate((("q", qg_ref), ("k", kg_ref), ("v", None))):
            for c in range(GROUP_WIDTH // LANES):
                pieces.append(functools.partial(finish, gi, d, ti, mode, gain_ref, c, stage))
                stage += d > 1

    chunks = list(range(0, w_ref.shape[1], GROUP_WIDTH))
    lead = 3
    for lo in chunks[:lead]:
        project(lo)
    rest = chunks[lead:]
    for k in range(max(len(rest), len(pieces))):
        if k < len(pieces):
            pieces[k]()
        if k < len(rest):
            project(rest[k])


def _inproj(x2, g, w, pos2, invf, qg, kg, b, seq):
    t = x2.shape[0]
    n = w.shape[1]
    tm = TM_INPROJ
    nj = seq // tm
    n_tiles = b * nj
    n_stage = sum(1 for _, d in ATTN_GROUPS if d > 1) * 3 * (GROUP_WIDTH // LANES)
    proj_row = lambda s: (jnp.minimum(s, n_tiles - 1), 0)
    done_row = lambda s: (jnp.maximum(s - 1, 0), 0)
    done_blk = lambda s: (jnp.maximum(s - 1, 0) // nj, 0, jnp.maximum(s - 1, 0) % nj, 0)
    const = lambda s: (0, 0)
    per_row = LANES // ROT_HALF
    return pl.pallas_call(
        _inproj_kernel,
        grid=(n_tiles + 1,),
        in_specs=[
            pl.BlockSpec((tm, D_MODEL), proj_row),
            pl.BlockSpec((1, D_MODEL), const),
            pl.BlockSpec((D_MODEL, n), const, pipeline_mode=pl.Buffered(1)),
            pl.BlockSpec((tm // per_row, per_row), done_row),
            pl.BlockSpec((1, LANES), const),
            pl.BlockSpec((1, LANES), const),
            pl.BlockSpec((1, LANES), const),
        ],
        out_specs=[pl.BlockSpec((1, d, tm // d, ATTN_WIDTH), done_blk) for _, d in ATTN_GROUPS] + [
            pl.BlockSpec((tm, 2 * SGU_WIDTH), proj_row),
            pl.BlockSpec((tm, GATE_WIDTH), proj_row),
        ],
        out_shape=[jax.ShapeDtypeStruct((b, d, seq // d, ATTN_WIDTH), BF16) for _, d in ATTN_GROUPS] + [
            jax.ShapeDtypeStruct((t, 2 * SGU_WIDTH), BF16),
            jax.ShapeDtypeStruct((t, GATE_WIDTH), BF16),
        ],
        scratch_shapes=[
            pltpu.VMEM((tm, 3 * ATTN_WIDTH), F32),
            pltpu.VMEM((tm, 3 * ATTN_WIDTH), F32),
            pltpu.VMEM((n_stage, tm, LANES), F32),
            pltpu.VMEM((tm, LANES), F32),
            pltpu.VMEM((tm, LANES), F32),
        ],
        compiler_params=_params(56, ("arbitrary",)),
        name="inproj",
    )(x2, g, w, pos2, invf, qg, kg)


def _attn_blocks(qkv_ref, ores_s, lres_s, seq, d):
    sub = seq // d
    nb = sub // BLK
    win = 2 * BLK if nb > 1 else BLK
    lane = lax.broadcasted_iota(I32, (BLK, GROUP_WIDTH), 1)
    q_head = 2 * (lane // LANES) + (lane // ROT_HALF) % 2
    qi = lax.broadcasted_iota(I32, (BLK, win), 0)
    kj = lax.broadcasted_iota(I32, (BLK, win), 1)
    masks = {shift: lax.bitcast_convert_type(qi - kj + shift, jnp.uint32) <= jnp.uint32(BLK) for shift in (0, BLK)}

    for idx in range(seq // BLK):
        r, n = idx % d, idx // d
        row0 = r * sub + n * BLK
        kn = max(n - 1, 0) if nb > 1 else n
        k0 = r * sub + kn * BLK
        valid = masks[(n - kn) * BLK]
        q = qkv_ref[0, pl.ds(row0, BLK), 0:GROUP_WIDTH]
        kw = qkv_ref[0, pl.ds(k0, win), GROUP_WIDTH:2 * GROUP_WIDTH]
        vw = qkv_ref[0, pl.ds(k0, win), 2 * GROUP_WIDTH:3 * GROUP_WIDTH]
        zero = jnp.zeros_like(q)
        qm = jnp.concatenate([jnp.where(q_head == h, q, zero) for h in range(4)], axis=0)
        s = _dot_nt(qm, kw)
        ps, ms, ls = [], [], []
        for h in range(4):
            sh = jnp.where(valid, s[h * BLK:(h + 1) * BLK], NEG_INF)
            m = jnp.max(sh, axis=-1, keepdims=True)
            p = jnp.exp2(sh - m)
            ls.append(jnp.sum(p, axis=-1, keepdims=True))
            ms.append(m)
            ps.append(p.astype(BF16))
        pv = _dot(jnp.concatenate(ps, axis=0), vw)
        for h in range(4):
            lanes = slice(h * HEAD_DIM, (h + 1) * HEAD_DIM)
            ores_s[pl.ds(row0, BLK), lanes] = pv[h * BLK:(h + 1) * BLK, lanes] * (1.0 / ls[h])
            lres_s[pl.ds(row0, BLK), lanes] = jnp.broadcast_to((ms[h] + jnp.log2(ls[h])) * LN2, (BLK, HEAD_DIM))


def _attn_kernel(g0_ref, g1_ref, g2_ref, o_ref, ores_s, lres_s, onat_s, lnat_s, *, seq):
    for gi, ((_, d), qkv_ref) in enumerate(zip(ATTN_GROUPS, (g0_ref, g1_ref, g2_ref))):
        _attn_blocks(qkv_ref, ores_s, lres_s, seq, d)
        sub = seq // d
        for r in range(d):
            for c in range(GROUP_WIDTH // LANES):
                t = gi * (GROUP_WIDTH // LANES) + c
                onat_s[t, pl.ds(r, sub, stride=d), :] = ores_s[r * sub:(r + 1) * sub, c * LANES:(c + 1) * LANES]
                lnat_s[t, pl.ds(r, sub, stride=d), :] = lres_s[r * sub:(r + 1) * sub, c * LANES:(c + 1) * LANES]

    tiles = GROUP_WIDTH // LANES

    def body(i, carry):
        r0 = pl.multiple_of(i * ROW_CHUNK, ROW_CHUNK)
        for c in range(tiles):
            ls = [lnat_s[gi * tiles + c, pl.ds(r0, ROW_CHUNK), :] for gi in range(N_GROUPS)]
            m = jnp.maximum(jnp.maximum(ls[0], ls[1]), ls[2])
            es = [jnp.exp(l - m) for l in ls]
            inv = 1.0 / (es[0] + es[1] + es[2])
            for gi in range(N_GROUPS):
                t = gi * tiles + c
                o_ref[0, pl.ds(r0, ROW_CHUNK), t * LANES:(t + 1) * LANES] = (
                    onat_s[t, pl.ds(r0, ROW_CHUNK), :] * (es[gi] * inv)).astype(BF16)
        return carry

    lax.fori_loop(0, seq // ROW_CHUNK, body, 0, unroll=2)


def _attention(qkv_groups):
    b = qkv_groups[0].shape[0]
    seq = qkv_groups[0].shape[1] * qkv_groups[0].shape[2]
    tiles = GROUP_WIDTH // LANES
    blk = pl.BlockSpec((1, seq, ATTN_WIDTH), lambda i: (i, 0, 0))
    return pl.pallas_call(
        functools.partial(_attn_kernel, seq=seq),
        grid=(b,),
        in_specs=[blk] * N_GROUPS,
        out_specs=blk,
        out_shape=jax.ShapeDtypeStruct((b, seq, ATTN_WIDTH), BF16),
        scratch_shapes=[
            pltpu.VMEM((seq, GROUP_WIDTH), F32),
            pltpu.VMEM((seq, GROUP_WIDTH), F32),
            pltpu.VMEM((N_GROUPS * tiles, seq, LANES), F32),
            pltpu.VMEM((N_GROUPS * tiles, seq, LANES), F32),
        ],
        compiler_params=_params(48, ("arbitrary",)),
        name="attention",
    )(*[a.reshape(b, seq, ATTN_WIDTH) for a in qkv_groups])


def _post_kernel(attn_ref, u_ref, vz_ref, ga_ref, gb_ref, x_ref, wa_ref, wb_ref, wo_ref,
                 lng_ref, lnb_ref, wcat_ref, bcat_ref, n2g_ref, wrh_ref, wrl_ref, br_ref,
                 x1_ref, hf_ref, e_ref, g_ref, cst_ref, cnt_ref, sgu_s, carry_s, merged_a, merged_b):
    step = pl.program_id(0)

    @pl.when(step == 0)
    def _():
        carry_s[...] = jnp.zeros_like(carry_s)
        merged_b[...] = jnp.zeros_like(merged_b)

    args = (attn_ref, u_ref, vz_ref, ga_ref, gb_ref, x_ref, wa_ref, wb_ref, wo_ref, lng_ref, lnb_ref, wcat_ref,
            bcat_ref, n2g_ref, wrh_ref, wrl_ref, br_ref, x1_ref, hf_ref, e_ref, g_ref, cst_ref, cnt_ref, sgu_s, carry_s)

    @pl.when(lax.rem(step, 2) == 0)
    def _():
        _post_step(*args, merged_a, merged_b)

    @pl.when(lax.rem(step, 2) == 1)
    def _():
        _post_step(*args, merged_b, merged_a)


def _post_step(attn_ref, u_ref, vz_ref, ga_ref, gb_ref, x_ref, wa_ref, wb_ref, wo_ref,
               lng_ref, lnb_ref, wcat_ref, bcat_ref, n2g_ref, wrh_ref, wrl_ref, br_ref,
               x1_ref, hf_ref, e_ref, g_ref, cst_ref, cnt_ref, sgu_s, carry_s, merged_w, merged_r):
    tm = x_ref.shape[0]
    counted = jnp.where(pl.program_id(0) > 0, 1.0, 0.0)

    lane = lax.broadcasted_iota(I32, (SGU_CHUNK, LANES), 1)
    low = lane < SGU_GROUP_DIM
    trow = lax.broadcasted_iota(I32, (SGU_CHUNK, 2 * SGU_CHUNK), 0)
    tcol = lax.broadcasted_iota(I32, (SGU_CHUNK, 2 * SGU_CHUNK), 1) % SGU_CHUNK
    causal = tcol <= trow

    def chunk(c):
        r0 = c * SGU_CHUNK
        u = _gelu(u_ref[pl.ds(r0, SGU_CHUNK), :].astype(F32))
        v = _gelu(vz_ref[pl.ds(r0, SGU_CHUNK), :].astype(F32))
        mu = jnp.mean(v, axis=-1, keepdims=True)
        vc = v - mu
        vn = vc * lax.rsqrt(jnp.mean(vc * vc, axis=-1, keepdims=True) + EPS)
        vn = (vn * lng_ref[...] + lnb_ref[...]).astype(BF16)
        zero = jnp.zeros((SGU_CHUNK, LANES), BF16)
        for j in range(SGU_WIDTH // LANES):
            vt = vn[:, j * LANES:(j + 1) * LANES]
            rhs = jnp.concatenate([jnp.where(low, vt, zero), jnp.where(low, zero, vt)], axis=0)
            wj = jnp.where(causal, wcat_ref[j], jnp.zeros((), BF16))
            mixed = _dot(wj, rhs) + bcat_ref[j]
            sgu_s[pl.ds(r0, SGU_CHUNK), j * LANES:(j + 1) * LANES] = (u[:, j * LANES:(j + 1) * LANES] * mixed).astype(BF16)

    eio = lax.broadcasted_iota(I32, (N_EXPERTS, tm), 0)
    state = {}
    top_v, onehots = [], []

    def route_logits(hi, lo):
        state["work"] = (_dot_nt(wrh_ref[...], hi) + _dot_nt(wrl_ref[...], hi) + _dot_nt(wrh_ref[...], lo)
                         + br_ref[...])

    def route_pick(k):
        work = state["work"]
        m = jnp.max(work, axis=0, keepdims=True)
        idx = jnp.min(jnp.where(work == m, eio, N_EXPERTS), axis=0, keepdims=True)
        oh = eio == idx
        state["work"] = jnp.where(oh, -jnp.inf, work)
        top_v.append(m)
        onehots.append(oh)
        e_ref[k:k + 1, :] = idx

    def route_finish():
        ex = [jnp.exp(v - top_v[0]) for v in top_v]
        inv = 1.0 / (ex[0] + ex[1] + ex[2] + ex[3])
        for k in range(TOP_K):
            g_ref[k:k + 1, :] = ex[k] * inv
        sel = jnp.zeros((N_EXPERTS, tm), F32)
        for oh in onehots:
            sel = jnp.where(oh, counted, sel)
        lane_e = lax.broadcasted_iota(I32, (N_EXPERTS, LANES), 1)
        run = carry_s[...]
        cst = jnp.zeros((N_EXPERTS, LANES), F32)
        for s in range(tm // TM_ROUTE):
            cst = jnp.where(lane_e == s, run, cst)
            run = run + jnp.sum(sel[:, s * TM_ROUTE:(s + 1) * TM_ROUTE], axis=1, keepdims=True)
        cst_ref[0] = cst.astype(I32)
        carry_s[...] = run
        cnt_ref[...] = run.astype(I32)

    n_parts = tm // SGU_CHUNK
    width = D_MODEL // n_parts
    x1_parts, y_a_parts = [], []
    for c in range(n_parts):
        cols = slice(c * width, (c + 1) * width)
        x1_parts.append(x_ref[:, cols] + _dot(merged_r[...], wo_ref[:, cols]))
        y_a_parts.append(_dot(attn_ref[...], wa_ref[:, cols]))
        chunk(c)
    x1 = jnp.concatenate(x1_parts, axis=1)
    x1_ref[...] = x1
    hf = x1 * lax.rsqrt(jnp.mean(x1 * x1, axis=-1, keepdims=True) + EPS) * n2g_ref[...]
    hi = hf.astype(BF16)
    hf_ref[...] = hi
    route_logits(hi, (hf - hi.astype(F32)).astype(BF16))
    for c in range(n_parts):
        cols = slice(c * width, (c + 1) * width)
        y_b = _dot(sgu_s[...], wb_ref[:, cols])
        merged_w[:, cols] = (jax.nn.sigmoid(ga_ref[:, cols].astype(F32)) * y_a_parts[c]
                             + jax.nn.sigmoid(gb_ref[:, cols].astype(F32)) * y_b).astype(BF16)
        if c < TOP_K:
            route_pick(c)
    for k in range(n_parts, TOP_K):
        route_pick(k)
    route_finish()


def _post(attn2, z2, gates2, x2, wa, wb, wo, lng, lnb, wcat, bcat, n2g, wrh, wrl, br):
    t = x2.shape[0]
    tm = TM_POST
    const = lambda *shape: pl.BlockSpec(shape, lambda i: (0,) * len(shape), pipeline_mode=pl.Buffered(1))
    n_tiles = t // tm
    cur = lambda i: jnp.minimum(i, n_tiles - 1)
    prev = lambda i: jnp.maximum(i - 1, 0)
    return pl.pallas_call(
        _post_kernel,
        grid=(n_tiles + 1,),
        in_specs=[
            pl.BlockSpec((tm, ATTN_WIDTH), lambda i: (cur(i), 0)),
            pl.BlockSpec((tm, SGU_WIDTH), lambda i: (cur(i), 0)),
            pl.BlockSpec((tm, SGU_WIDTH), lambda i: (cur(i), 1)),
            pl.BlockSpec((tm, D_MODEL), lambda i: (cur(i), 0)),
            pl.BlockSpec((tm, D_MODEL), lambda i: (cur(i), 1)),
            pl.BlockSpec((tm, D_MODEL), lambda i: (prev(i), 0)),
            const(ATTN_WIDTH, D_MODEL),
            const(SGU_WIDTH, D_MODEL),
            const(D_MODEL, D_MODEL),
            const(1, SGU_WIDTH),
            const(1, SGU_WIDTH),
            const(SGU_WIDTH // LANES, SGU_CHUNK, 2 * SGU_CHUNK),
            const(SGU_WIDTH // LANES, SGU_CHUNK, LANES),
            const(1, D_MODEL),
            const(N_EXPERTS, D_MODEL),
            const(N_EXPERTS, D_MODEL),
            const(N_EXPERTS, 1),
        ],
        out_specs=[
            pl.BlockSpec((tm, D_MODEL), lambda i: (prev(i), 0)),
            pl.BlockSpec((tm, D_MODEL), lambda i: (prev(i), 0)),
            pl.BlockSpec((TOP_K, tm), lambda i: (0, prev(i))),
            pl.BlockSpec((TOP_K, tm), lambda i: (0, prev(i))),
            pl.BlockSpec((1, N_EXPERTS, LANES), lambda i: (prev(i), 0, 0)),
            pl.BlockSpec((N_EXPERTS, LANES), lambda i: (0, 0)),
        ],
        out_shape=[
            jax.ShapeDtypeStruct((t, D_MODEL), F32),
            jax.ShapeDtypeStruct((t, D_MODEL), BF16),
            jax.ShapeDtypeStruct((TOP_K, t), I32),
            jax.ShapeDtypeStruct((TOP_K, t), F32),
            jax.ShapeDtypeStruct((t // tm, N_EXPERTS, LANES), I32),
            jax.ShapeDtypeStruct((N_EXPERTS, LANES), I32),
        ],
        scratch_shapes=[
            pltpu.VMEM((tm, SGU_WIDTH), BF16),
            pltpu.VMEM((N_EXPERTS, LANES), F32),
        ] + [pltpu.VMEM((tm, D_MODEL), BF16)] * 2,
        compiler_params=_params(48, ("arbitrary",)),
        name="post",
    )(attn2, z2, z2, gates2, gates2, x2, wa, wb, wo, lng, lnb, wcat, bcat, n2g, wrh, wrl, br)


def _route_rows(e_ref, seen, tm):
    e = e_ref[...]
    eio = lax.broadcasted_iota(I32, (N_EXPERTS, tm), 0)
    onehots = [eio == e[k:k + 1, :] for k in range(TOP_K)]
    sel = jnp.zeros((N_EXPERTS, tm), F32)
    for oh in onehots:
        sel = jnp.where(oh, 1.0, sel)
    ti = lax.broadcasted_iota(I32, (tm, tm), 0)
    tj = lax.broadcasted_iota(I32, (tm, tm), 1)
    before = jnp.where(ti < tj, 1.0, 0.0).astype(BF16)
    slot = _dot(sel.astype(BF16), before)
    cnt = jnp.sum(sel, axis=1, keepdims=True)
    head = seen - SUBLANES * jnp.floor(seen * (1.0 / SUBLANES))
    tiles = jnp.where(cnt > 0.0, jnp.floor((head + cnt + (SUBLANES - 1)) * (1.0 / SUBLANES)), 0.0)
    xi = lax.broadcasted_iota(I32, (N_EXPERTS, N_EXPERTS), 0)
    xj = lax.broadcasted_iota(I32, (N_EXPERTS, N_EXPERTS), 1)
    lower = jnp.where(xj < xi, 1.0, 0.0).astype(BF16)
    off = _dot(lower, jnp.broadcast_to(tiles, (N_EXPERTS, LANES)).astype(BF16))[:, 0:1] * float(SUBLANES)
    row = off + head + slot
    return [jnp.sum(jnp.where(oh, row, 0.0), axis=0, keepdims=True) for oh in onehots], cnt


def _pick(rows, values, default):
    tm = rows[0].shape[1]
    rio = lax.broadcasted_iota(I32, (ROUTE_ROWS, tm), 0).astype(F32)
    out = default
    for r, v in zip(rows, values):
        out = jnp.where(rio == r, v, out)
    return out


def _wait_rows(buf, hbm_ref, sem, n):
    n = pl.multiple_of(n, SUBLANES)

    @pl.when(n > 0)
    def _():
        pltpu.make_async_copy(buf.at[pl.ds(0, n), :], hbm_ref.at[pl.ds(0, n), :], sem).wait()


def _dispatch_kernel(start_ref, off_ref, staged_ref, full_ref, tot_ref, flush_ref, za_ref, zb_ref, hf_ref, e_ref,
                     xs_ref, rows_ref, obuf, zbuf, seen_s, tail_s, first_s, sem, semz):
    tm = hf_ref.shape[0]
    i = pl.program_id(0)
    last = pl.num_programs(0) - 1
    cur = lax.rem(i, 2)
    spare = ROUTE_ROWS - SUBLANES

    @pl.when(i == 0)
    def _():
        seen_s[...] = jnp.zeros_like(seen_s)
        tail_s[...] = jnp.zeros_like(tail_s)

    @pl.when(i >= 2)
    def _():
        _wait_rows(obuf.at[cur], xs_ref, sem.at[cur], tot_ref[jnp.maximum(i - 2, 0)])

    rows, cnt = _route_rows(e_ref, seen_s[:, 0:1], tm)
    seen_s[...] = seen_s[...] + cnt
    for k in range(TOP_K):
        rows_ref[k:k + 1, :] = rows[k]
    pt = _pick(rows, [1.0] * TOP_K, 0.0).astype(BF16)
    obuf[cur] = _dot(pt, hf_ref[...])

    for x in range(N_EXPERTS):
        k = i * N_EXPERTS + x
        staged, full = staged_ref[k], full_ref[k]
        has = jnp.where(staged > 0, 1.0, 0.0)
        partial = jnp.where(staged > full, 1.0, 0.0)
        alone = jnp.where(full == 0, 1.0, 0.0)
        first = pl.multiple_of(jnp.where(staged > 0, off_ref[k], spare), SUBLANES)
        rest = pl.multiple_of(jnp.where(staged > 0, off_ref[k] + full, spare), SUBLANES)
        kept = tail_s[x]
        merged = obuf[cur, pl.ds(first, SUBLANES), :] + kept * has
        first_s[cur, x] = merged
        ending = merged * alone + obuf[cur, pl.ds(rest, SUBLANES), :] * (1.0 - alone)
        tail_s[x] = ending * partial + kept * (1.0 - has)

    for x in range(N_EXPERTS):
        n = pl.multiple_of(full_ref[i * N_EXPERTS + x], SUBLANES)
        dst = pl.multiple_of(start_ref[i * N_EXPERTS + x], SUBLANES)

        @pl.when(n > 0)
        def _(x=x, dst=dst):
            pltpu.make_async_copy(first_s.at[cur, x], xs_ref.at[pl.ds(dst, SUBLANES), :], sem.at[cur]).start()

        @pl.when(n > SUBLANES)
        def _(x=x, n=n, dst=dst):
            src = pl.multiple_of(off_ref[i * N_EXPERTS + x] + SUBLANES, SUBLANES)
            more = pl.multiple_of(n - SUBLANES, SUBLANES)
            pltpu.make_async_copy(obuf.at[cur, pl.ds(src, more), :],
                                  xs_ref.at[pl.ds(pl.multiple_of(dst + SUBLANES, SUBLANES), more), :], sem.at[cur]).start()

    @pl.when(i == last)
    def _():
        @pl.when(i >= 1)
        def _():
            _wait_rows(obuf.at[1 - cur], xs_ref, sem.at[1 - cur], tot_ref[jnp.maximum(i - 1, 0)])

        _wait_rows(obuf.at[cur], xs_ref, sem.at[cur], tot_ref[i])
        for x in range(N_EXPERTS):
            row = flush_ref[x]

            @pl.when(row >= 0)
            def _(x=x, row=row):
                cp = pltpu.make_async_copy(tail_s.at[x], xs_ref.at[pl.ds(pl.multiple_of(row, SUBLANES), SUBLANES), :], semz)
                cp.start()
                cp.wait()
        zbuf[...] = jnp.zeros_like(zbuf)

        def zero_region(z, act):
            n = zb_ref[z] - za_ref[z]
            big = n // EXPERT_TILE
            rest = pl.multiple_of(n - big * EXPERT_TILE, SUBLANES)

            def piece(row, nrows):
                return pltpu.make_async_copy(zbuf.at[pl.ds(0, nrows), :],
                                             xs_ref.at[pl.ds(pl.multiple_of(row, SUBLANES), nrows), :], semz)

            lax.fori_loop(0, big, lambda m, c: (act(piece(za_ref[z] + m * EXPERT_TILE, EXPERT_TILE)), c)[1], 0)

            @pl.when(rest > 0)
            def _():
                act(piece(za_ref[z] + big * EXPERT_TILE, rest))

        for z in range(N_EXPERTS + 1):
            zero_region(z, lambda cp: cp.start())
        for z in range(N_EXPERTS + 1):
            zero_region(z, lambda cp: cp.wait())


def _dispatch(start, off, staged, full, tot, flush, za, zb, hf, e4, cap):
    t = hf.shape[0]
    tm = TM_ROUTE
    tok = lambda i, *_: (0, i)
    return pl.pallas_call(
        _dispatch_kernel,
        grid_spec=pltpu.PrefetchScalarGridSpec(
            num_scalar_prefetch=8,
            grid=(t // tm,),
            in_specs=[
                pl.BlockSpec((tm, D_MODEL), lambda i, *_: (i, 0)),
                pl.BlockSpec((TOP_K, tm), tok),
            ],
            out_specs=[pl.BlockSpec(memory_space=pl.ANY), pl.BlockSpec((TOP_K, tm), tok)],
            scratch_shapes=[
                pltpu.VMEM((2, ROUTE_ROWS, D_MODEL), F32),
                pltpu.VMEM((EXPERT_TILE, D_MODEL), F32),
                pltpu.VMEM((N_EXPERTS, LANES), F32),
                pltpu.VMEM((N_EXPERTS, SUBLANES, D_MODEL), F32),
                pltpu.VMEM((2, N_EXPERTS, SUBLANES, D_MODEL), F32),
                pltpu.SemaphoreType.DMA((2,)),
                pltpu.SemaphoreType.DMA(()),
            ],
        ),
        out_shape=[jax.ShapeDtypeStruct((cap, D_MODEL), F32), jax.ShapeDtypeStruct((TOP_K, t), F32)],
        compiler_params=_params(48, ("arbitrary",)),
        name="dispatch",
    )(start, off, staged, full, tot, flush, za, zb, hf, e4)


def _expert_kernel(te_ref, nu_ref, xs_ref, wgu_ref, bgu_ref, wdn_ref, bdn_ref, ys_ref, wgu_s, wdn_s):
    i = pl.program_id(0)

    @pl.when(i < nu_ref[0])
    def _():
        @pl.when((i == 0) | (te_ref[i] != te_ref[jnp.maximum(i - 1, 0)]))
        def _():
            wgu_s[...] = wgu_ref[0].astype(BF16)
            wdn_s[...] = wdn_ref[0].astype(BF16)

        x = xs_ref[...].astype(BF16)
        gu = _dot(x, wgu_s[...]) + bgu_ref[0]
        gate = jnp.minimum(gu[:, :D_EXPERT], SWIGLU_LIMIT)
        up = jnp.clip(gu[:, D_EXPERT:], -SWIGLU_LIMIT, SWIGLU_LIMIT)
        act = (up + 1.0) * (gate * jax.nn.sigmoid(SWIGLU_ALPHA * gate))
        ys_ref[...] = _dot(act.astype(BF16), wdn_s[...]) + bdn_ref[0]

    @pl.when(i >= nu_ref[0])
    def _():
        ys_ref[...] = jnp.zeros_like(ys_ref)


def _experts(tile_expert, n_used, xs, wgu, bgu, wdn, bdn):
    cap = xs.shape[0]
    n_tiles = cap // EXPERT_TILE
    row = lambda i, te, nu: (jnp.minimum(i, nu[0] - 1), 0)
    by_expert = lambda i, te, nu: (te[i], 0, 0)
    return pl.pallas_call(
        _expert_kernel,
        grid_spec=pltpu.PrefetchScalarGridSpec(
            num_scalar_prefetch=2,
            grid=(n_tiles,),
            in_specs=[
                pl.BlockSpec((EXPERT_TILE, D_MODEL), row),
                pl.BlockSpec((1, D_MODEL, 2 * D_EXPERT), by_expert),
                pl.BlockSpec((1, 1, 2 * D_EXPERT), by_expert),
                pl.BlockSpec((1, D_EXPERT, D_MODEL), by_expert),
                pl.BlockSpec((1, 1, D_MODEL), by_expert),
            ],
            out_specs=pl.BlockSpec((EXPERT_TILE, D_MODEL), lambda i, te, nu: (i, 0)),
            scratch_shapes=[
                pltpu.VMEM((D_MODEL, 2 * D_EXPERT), BF16),
                pltpu.VMEM((D_EXPERT, D_MODEL), BF16),
            ],
        ),
        out_shape=jax.ShapeDtypeStruct((cap, D_MODEL), F32),
        compiler_params=_params(56, ("arbitrary",)),
        name="experts",
    )(tile_expert, n_used, xs, wgu, bgu, wdn, bdn)


def _combine_kernel(start_ref, off_ref, cnt_ref, tot_ref, x1_ref, rows_ref, g_ref, ys_ref, o_ref, sbuf, sem):
    i = pl.program_id(0)
    last = pl.num_programs(0) - 1
    cur = lax.rem(i, 2)

    def fetch(step, b):
        for x in range(N_EXPERTS):
            n = pl.multiple_of(cnt_ref[step * N_EXPERTS + x], SUBLANES)

            @pl.when(n > 0)
            def _(x=x, n=n):
                src = pl.multiple_of(start_ref[step * N_EXPERTS + x], SUBLANES)
                dst = pl.multiple_of(off_ref[step * N_EXPERTS + x], SUBLANES)
                pltpu.make_async_copy(ys_ref.at[pl.ds(src, n), :], sbuf.at[b, pl.ds(dst, n), :], sem.at[b]).start()

    @pl.when(i == 0)
    def _():
        sbuf[...] = jnp.zeros_like(sbuf)
        fetch(i, cur)

    @pl.when(i < last)
    def _():
        fetch(i + 1, 1 - cur)

    rows = [rows_ref[k:k + 1, :] for k in range(TOP_K)]
    pt = _pick(rows, [1.0] * TOP_K, 0.0).astype(BF16)
    g = g_ref[...]
    gate = jnp.sum(_pick(rows, [g[k:k + 1, :] for k in range(TOP_K)], 0.0), axis=1, keepdims=True)
    _wait_rows(sbuf.at[cur], ys_ref, sem.at[cur], tot_ref[i])
    weighted = (sbuf[cur] * gate).astype(BF16)
    picked = lax.dot_general(pt, weighted, _TN, preferred_element_type=F32)
    o_ref[...] = x1_ref[...] + picked


def _combine(start, off, cnt, tot, x1, rows4, g4, ys):
    t = x1.shape[0]
    tm = TM_ROUTE
    return pl.pallas_call(
        _combine_kernel,
        grid_spec=pltpu.PrefetchScalarGridSpec(
            num_scalar_prefetch=4,
            grid=(t // tm,),
            in_specs=[
                pl.BlockSpec((tm, D_MODEL), lambda i, *_: (i, 0)),
                pl.BlockSpec((TOP_K, tm), lambda i, *_: (0, i)),
                pl.BlockSpec((TOP_K, tm), lambda i, *_: (0, i)),
                pl.BlockSpec(memory_space=pl.ANY),
            ],
            out_specs=pl.BlockSpec((tm, D_MODEL), lambda i, *_: (i, 0)),
            scratch_shapes=[
                pltpu.VMEM((2, ROUTE_ROWS, D_MODEL), F32),
                pltpu.SemaphoreType.DMA((2,)),
            ],
        ),
        out_shape=jax.ShapeDtypeStruct((t, D_MODEL), F32),
        compiler_params=_params(48, ("arbitrary",)),
        name="combine",
    )(start, off, cnt, tot, x1, rows4, g4, ys)


def _layer(x, positions, norm1_g, w_in, q_norm_g, k_norm_g, sgu_ln_g, sgu_ln_b, w_spatial, b_spatial,
           w_branch_a, w_branch_b, w_out, norm2_g, w_router, b_router, w_gate_up, b_gate_up, w_down, b_down):
    b, seq, _ = x.shape
    t = b * seq
    x2 = x.reshape(t, D_MODEL)

    inv_freq = ROPE_THETA ** (-jnp.arange(ROT_HALF, dtype=F32) / ROT_HALF)
    invf = jnp.tile(inv_freq, LANES // ROT_HALF).reshape(1, LANES)
    dim_of_lane = [(l // (2 * ROT_HALF)) * ROT_HALF + l % ROT_HALF for l in range(LANES)]
    qg = q_norm_g[jnp.array(dim_of_lane)].reshape(1, LANES)
    kg = k_norm_g[jnp.array(dim_of_lane)].reshape(1, LANES)
    w_bf = w_in.astype(BF16)
    qk = w_bf[:, :2 * ATTN_WIDTH].reshape(D_MODEL, 2 * ATTN_WIDTH // LANES, 2, 2, ROT_HALF)
    qk = qk.transpose(0, 1, 3, 2, 4).reshape(D_MODEL, 2 * ATTN_WIDTH)
    w_in_p = jnp.concatenate([qk, w_bf[:, 2 * ATTN_WIDTH:]], axis=1)
    n_pairs = SGU_WIDTH // LANES
    wcat = w_spatial.reshape(n_pairs, 2, SGU_CHUNK, SGU_CHUNK).transpose(0, 2, 1, 3).reshape(n_pairs, SGU_CHUNK, 2 * SGU_CHUNK)
    bcat = jnp.repeat(b_spatial.reshape(n_pairs, 2, SGU_CHUNK).transpose(0, 2, 1), SGU_GROUP_DIM, axis=2)
    wr_t = w_router.T
    wr_hi = wr_t.astype(BF16)
    wr_lo = (wr_t - wr_hi.astype(F32)).astype(BF16)

    *qkv_groups, z, gates = _inproj(x2, norm1_g.reshape(1, D_MODEL), w_in_p, positions.reshape(-1, LANES // ROT_HALF), invf, qg, kg, b, seq)
    attn = _attention(qkv_groups)
    x1, hf, e4, g4, cst, counts = _post(
        attn.reshape(t, ATTN_WIDTH), z, gates, x2,
        w_branch_a.astype(BF16), w_branch_b.astype(BF16), w_out.astype(BF16),
        sgu_ln_g.reshape(1, SGU_WIDTH), sgu_ln_b.reshape(1, SGU_WIDTH), wcat.astype(BF16), bcat,
        norm2_g.reshape(1, D_MODEL), wr_hi, wr_lo, b_router.reshape(N_EXPERTS, 1))

    steps_per_tile = TM_POST // TM_ROUTE
    n_steps = t // TM_ROUTE
    before = cst[:, :, :steps_per_tile].transpose(0, 2, 1).reshape(n_steps, N_EXPERTS)
    total = counts[:, 0]
    cnt = jnp.concatenate([before[1:], total[None, :]], axis=0) - before
    head = before % SUBLANES
    staged = jnp.where(cnt > 0, (head + cnt + SUBLANES - 1) // SUBLANES * SUBLANES, 0)
    full = jnp.where(cnt > 0, (head + cnt) // SUBLANES * SUBLANES, 0)
    off = (jnp.cumsum(staged, axis=1) - staged).astype(I32).reshape(-1)
    tot_staged = jnp.sum(staged, axis=1).astype(I32)
    tot_full = jnp.sum(full, axis=1).astype(I32)
    region = (total + EXPERT_TILE - 1) // EXPERT_TILE * EXPERT_TILE
    rend = jnp.cumsum(region)
    rstart = rend - region
    start = (rstart[None, :] + before - head).astype(I32).reshape(-1)
    bound = t * TOP_K + N_EXPERTS * (EXPERT_TILE - 1)
    cap = (bound + EXPERT_TILE - 1) // EXPERT_TILE * EXPERT_TILE
    n_tiles = cap // EXPERT_TILE
    tile_expert = jnp.sum(jnp.arange(n_tiles, dtype=I32)[:, None] * EXPERT_TILE >= rend[None, :], axis=1)
    tile_expert = jnp.minimum(tile_expert, N_EXPERTS - 1).astype(I32)
    n_used = (rend[-1:] // EXPERT_TILE).astype(I32)
    filled = (total + SUBLANES - 1) // SUBLANES * SUBLANES
    flush = jnp.where(total % SUBLANES != 0, rstart + total // SUBLANES * SUBLANES, -1).astype(I32)
    za = jnp.concatenate([rstart + filled, rend[-1:]]).astype(I32)
    zb = jnp.concatenate([rend, jnp.full((1,), cap, rend.dtype)]).astype(I32)
    staged = staged.astype(I32).reshape(-1)
    full = full.astype(I32).reshape(-1)

    xs, rows4 = _dispatch(start, off, staged, full, tot_full, flush, za, zb, hf, e4, cap)
    ys = _experts(tile_expert, n_used, xs,
                  w_gate_up, b_gate_up.reshape(N_EXPERTS, 1, 2 * D_EXPERT),
                  w_down, b_down.reshape(N_EXPERTS, 1, D_MODEL))
    out = _combine(start, off, staged, tot_staged, x1, rows4, g4, ys)
    return out.reshape(b, seq, D_MODEL)


def kernel(x, positions, norm1_g, w_in, q_norm_g, k_norm_g, sgu_ln_g, sgu_ln_b, w_spatial, b_spatial, w_branch_a, w_branch_b, w_out, norm2_g, w_router, b_router, w_gate_up, b_gate_up, w_down, b_down):
    for layer in range(norm1_g.shape[0]):
        x = _layer(x, positions, norm1_g[layer], w_in[layer], q_norm_g[layer], k_norm_g[layer],
                   sgu_ln_g[layer], sgu_ln_b[layer], w_spatial[layer], b_spatial[layer],
                   w_branch_a[layer], w_branch_b[layer], w_out[layer], norm2_g[layer],
                   w_router[layer], b_router[layer], w_gate_up[layer], b_gate_up[layer],
                   w_down[layer], b_down[layer])
    return x
```

```python
import functools

import jax
import jax.numpy as jnp
from jax import lax
from jax.experimental import pallas as pl
from jax.experimental.pallas import tpu as pltpu

F32 = jnp.float32
BF16 = jnp.bfloat16
I32 = jnp.int32

D_MODEL = 1024
HEAD_DIM = 64
ROT_HALF = HEAD_DIM // 2
LOG2E = 1.4426950408889634
LN2 = 0.6931471805599453
Q_SCALE = HEAD_DIM ** -0.5 * LOG2E
ATTN_GROUPS = ((128, 1), (512, 4), (2048, 16))
N_GROUPS = len(ATTN_GROUPS)
GROUP_WIDTH = 256
ATTN_WIDTH = N_GROUPS * GROUP_WIDTH
BLK = 128
ROPE_THETA = 10000.0
SGU_CHUNK = 128
SGU_GROUP_DIM = 64
SGU_WIDTH = 768
GATE_WIDTH = 2 * D_MODEL
N_EXPERTS = 32
TOP_K = 4
D_EXPERT = D_MODEL
SWIGLU_LIMIT = 7.0
SWIGLU_ALPHA = 1.702
EPS = 1e-6
NEG_INF = -1e30
LANES = 128

TM_INPROJ = 512
TM_POST = 512
TM_ROUTE = 256
SUBLANES = 8
ROUTE_ROWS = TM_ROUTE * TOP_K + N_EXPERTS * 2 * SUBLANES
EXPERT_TILE = 512
ROW_CHUNK = 256

_NT = (((1,), (1,)), ((), ()))
_TN = (((0,), (0,)), ((), ()))


def _dot(a, b):
    return jnp.dot(a, b, preferred_element_type=F32)


def _dot_nt(a, b):
    return lax.dot_general(a, b, _NT, preferred_element_type=F32)


def _gelu(x):
    return 0.5 * x * (1.0 + lax.erf(x * 0.7071067811865476))


VMEM_MIB = {"inproj": 56, "attention": 48, "post": 48, "dispatch": 48, "experts": 56, "combine": 48}


def _params(name):
    return pltpu.CompilerParams(dimension_semantics=("arbitrary",), vmem_limit_bytes=VMEM_MIB[name] * 1024 * 1024)


def _split3(x):
    a = x.astype(BF16)
    r = x - a.astype(F32)
    b = r.astype(BF16)
    return a, b, (r - b.astype(F32)).astype(BF16)


def _inproj_kernel(x_ref, g_ref, w_ref, pos_ref, invf_ref, qg_ref, kg_ref,
                   o0_ref, o1_ref, o2_ref, oz_ref, og_ref, raw_a, raw_b, stage_s, cos_s, sin_s):
    step = pl.program_id(0)

    @pl.when(step == 0)
    def _():
        raw_b[...] = jnp.zeros_like(raw_b)

    args = (x_ref, g_ref, w_ref, pos_ref, invf_ref, qg_ref, kg_ref, o0_ref, o1_ref, o2_ref, oz_ref, og_ref)

    @pl.when(lax.rem(step, 2) == 0)
    def _():
        _inproj_step(*args, raw_a, raw_b, stage_s, cos_s, sin_s)

    @pl.when(lax.rem(step, 2) == 1)
    def _():
        _inproj_step(*args, raw_b, raw_a, stage_s, cos_s, sin_s)


def _inproj_step(x_ref, g_ref, w_ref, pos_ref, invf_ref, qg_ref, kg_ref,
                 o0_ref, o1_ref, o2_ref, oz_ref, og_ref, raw_w, raw_r, stage_s, cos_s, sin_s):
    tm = x_ref.shape[0]
    x = x_ref[...]
    y = x * lax.rsqrt(jnp.mean(x * x, axis=-1, keepdims=True) + EPS)
    h = (y * g_ref[...]).astype(BF16)

    def project(lo):
        p = _dot(h, w_ref[:, lo:lo + GROUP_WIDTH])
        zc = 3 * ATTN_WIDTH
        gc = zc + 2 * SGU_WIDTH
        if lo < zc:
            raw_w[:, lo:lo + GROUP_WIDTH] = p
        elif lo < gc:
            oz_ref[:, lo - zc:lo - zc + GROUP_WIDTH] = p.astype(BF16)
        else:
            og_ref[:, lo - gc:lo - gc + GROUP_WIDTH] = p.astype(BF16)

    def rotary_tables():
        per_row = LANES // ROT_HALF
        lane = lax.broadcasted_iota(I32, (tm // per_row, LANES), 1)
        p4 = pos_ref[...].astype(F32)
        posd = p4[:, per_row - 1:per_row]
        for j in range(per_row - 2, -1, -1):
            posd = jnp.where(lane < (j + 1) * ROT_HALF, p4[:, j:j + 1], posd)
        ang = posd * invf_ref[...]
        cparts = _split3(jnp.cos(ang))
        sparts = _split3(jnp.sin(ang))
        src = lax.broadcasted_iota(I32, (LANES, LANES), 0)
        dst = lax.broadcasted_iota(I32, (LANES, LANES), 1)
        for j in range(per_row):
            pick = src == j * ROT_HALF + dst % ROT_HALF
            spread = jnp.where(pick, 1.0, 0.0).astype(BF16)
            spread_neg = jnp.where(pick, jnp.where(dst < 2 * ROT_HALF, -1.0, 1.0), 0.0).astype(BF16)
            cos_s[pl.ds(j, tm // per_row, stride=per_row), :] = sum(_dot(p, spread) for p in cparts)
            sin_s[pl.ds(j, tm // per_row, stride=per_row), :] = sum(_dot(p, spread_neg) for p in sparts)

    li = (lax.broadcasted_iota(I32, (LANES, LANES), 0) // ROT_HALF) % 2
    lj = (lax.broadcasted_iota(I32, (LANES, LANES), 1) // ROT_HALF) % 2
    head_sum = jnp.where(li == lj, 1.0, 0.0).astype(BF16)
    outs = (o0_ref, o1_ref, o2_ref)

    def finish(gi, d, ti, mode, gain_ref, c, stage):
        col = ti * ATTN_WIDTH + gi * GROUP_WIDTH
        t = raw_r[:, col + c * LANES:col + (c + 1) * LANES]
        if mode != "v":
            ss = t * t
            hi = ss.astype(BF16)
            lo = (ss - hi.astype(F32)).astype(BF16)
            ms = (_dot(hi, head_sum) + _dot(lo, head_sum)) * (1.0 / HEAD_DIM)
            t = t * lax.rsqrt(ms + EPS) * gain_ref[...]
            t = t * cos_s[...] + pltpu.roll(t, 2 * ROT_HALF, 1) * sin_s[...]
            if mode == "q":
                t = t * Q_SCALE
        lanes = slice(ti * GROUP_WIDTH + c * LANES, ti * GROUP_WIDTH + (c + 1) * LANES)
        if d == 1:
            outs[gi][0, 0, :, lanes] = t.astype(BF16)
        else:
            stage_s[stage] = t
            for r in range(d):
                outs[gi][0, r, :, lanes] = stage_s[stage, pl.ds(r, tm // d, stride=d), :].astype(BF16)

    pieces = [rotary_tables]
    stage = 0
    for gi, (_, d) in enumerate(ATTN_GROUPS):
        for ti, (mode, gain_ref) in enumerate((("q", qg_ref), ("k", kg_ref), ("v", None))):
            for c in range(GROUP_WIDTH // LANES):
                pieces.append(functools.partial(finish, gi, d, ti, mode, gain_ref, c, stage))
                stage += d > 1

    chunks = list(range(0, w_ref.shape[1], GROUP_WIDTH))
    lead = 3
    for lo in chunks[:lead]:
        project(lo)
    rest = chunks[lead:]
    for k in range(max(len(rest), len(pieces))):
        if k < len(pieces):
            pieces[k]()
        if k < len(rest):
            project(rest[k])


def _inproj(x2, g, w, pos2, invf, qg, kg, b, seq):
    t = x2.shape[0]
    n = w.shape[1]
    tm = TM_INPROJ
    nj = seq // tm
    n_tiles = b * nj
    n_stage = sum(1 for _, d in ATTN_GROUPS if d > 1) * 3 * (GROUP_WIDTH // LANES)
    proj_row = lambda s: (jnp.minimum(s, n_tiles - 1), 0)
    done_row = lambda s: (jnp.maximum(s - 1, 0), 0)
    done_blk = lambda s: (jnp.maximum(s - 1, 0) // nj, 0, jnp.maximum(s - 1, 0) % nj, 0)
    const = lambda s: (0, 0)
    per_row = LANES // ROT_HALF
    return pl.pallas_call(
        _inproj_kernel,
        grid=(n_tiles + 1,),
        in_specs=[
            pl.BlockSpec((tm, D_MODEL), proj_row),
            pl.BlockSpec((1, D_MODEL), const),
            pl.BlockSpec((D_MODEL, n), const, pipeline_mode=pl.Buffered(1)),
            pl.BlockSpec((tm // per_row, per_row), done_row),
            pl.BlockSpec((1, LANES), const),
            pl.BlockSpec((1, LANES), const),
            pl.BlockSpec((1, LANES), const),
        ],
        out_specs=[pl.BlockSpec((1, d, tm // d, ATTN_WIDTH), done_blk) for _, d in ATTN_GROUPS] + [
            pl.BlockSpec((tm, 2 * SGU_WIDTH), proj_row),
            pl.BlockSpec((tm, GATE_WIDTH), proj_row),
        ],
        out_shape=[jax.ShapeDtypeStruct((b, d, seq // d, ATTN_WIDTH), BF16) for _, d in ATTN_GROUPS] + [
            jax.ShapeDtypeStruct((t, 2 * SGU_WIDTH), BF16),
            jax.ShapeDtypeStruct((t, GATE_WIDTH), BF16),
        ],
        scratch_shapes=[
            pltpu.VMEM((tm, 3 * ATTN_WIDTH), F32),
            pltpu.VMEM((tm, 3 * ATTN_WIDTH), F32),
            pltpu.VMEM((n_stage, tm, LANES), F32),
            pltpu.VMEM((tm, LANES), F32),
            pltpu.VMEM((tm, LANES), F32),
        ],
        compiler_params=_params("inproj"),
        name="inproj",
    )(x2, g, w, pos2, invf, qg, kg)


def _attn_blocks(qkv_ref, ores_s, lres_s, seq, d):
    sub = seq // d
    nb = sub // BLK
    win = 2 * BLK if nb > 1 else BLK
    lane = lax.broadcasted_iota(I32, (BLK, GROUP_WIDTH), 1)
    q_head = 2 * (lane // LANES) + (lane // ROT_HALF) % 2
    qi = lax.broadcasted_iota(I32, (BLK, win), 0)
    kj = lax.broadcasted_iota(I32, (BLK, win), 1)
    masks = {shift: lax.bitcast_convert_type(qi - kj + shift, jnp.uint32) <= jnp.uint32(BLK) for shift in (0, BLK)}

    for idx in range(seq // BLK):
        r, n = idx % d, idx // d
        row0 = r * sub + n * BLK
        kn = max(n - 1, 0) if nb > 1 else n
        k0 = r * sub + kn * BLK
        valid = masks[(n - kn) * BLK]
        q = qkv_ref[0, pl.ds(row0, BLK), 0:GROUP_WIDTH]
        kw = qkv_ref[0, pl.ds(k0, win), GROUP_WIDTH:2 * GROUP_WIDTH]
        vw = qkv_ref[0, pl.ds(k0, win), 2 * GROUP_WIDTH:3 * GROUP_WIDTH]
        zero = jnp.zeros_like(q)
        qm = jnp.concatenate([jnp.where(q_head == h, q, zero) for h in range(4)], axis=0)
        s = _dot_nt(qm, kw)
        ps, ms, ls = [], [], []
        for h in range(4):
            sh = jnp.where(valid, s[h * BLK:(h + 1) * BLK], NEG_INF)
            m = jnp.max(sh, axis=-1, keepdims=True)
            p = jnp.exp2(sh - m)
            ls.append(jnp.sum(p, axis=-1, keepdims=True))
            ms.append(m)
            ps.append(p.astype(BF16))
        pv = _dot(jnp.concatenate(ps, axis=0), vw)
        for h in range(4):
            lanes = slice(h * HEAD_DIM, (h + 1) * HEAD_DIM)
            ores_s[pl.ds(row0, BLK), lanes] = pv[h * BLK:(h + 1) * BLK, lanes] * (1.0 / ls[h])
            lres_s[pl.ds(row0, BLK), lanes] = jnp.broadcast_to((ms[h] + jnp.log2(ls[h])) * LN2, (BLK, HEAD_DIM))


def _attn_kernel(g0_ref, g1_ref, g2_ref, o_ref, ores_s, lres_s, onat_s, lnat_s, *, seq):
    for gi, ((_, d), qkv_ref) in enumerate(zip(ATTN_GROUPS, (g0_ref, g1_ref, g2_ref))):
        _attn_blocks(qkv_ref, ores_s, lres_s, seq, d)
        sub = seq // d
        for r in range(d):
            for c in range(GROUP_WIDTH // LANES):
                t = gi * (GROUP_WIDTH // LANES) + c
                onat_s[t, pl.ds(r, sub, stride=d), :] = ores_s[r * sub:(r + 1) * sub, c * LANES:(c + 1) * LANES]
                lnat_s[t, pl.ds(r, sub, stride=d), :] = lres_s[r * sub:(r + 1) * sub, c * LANES:(c + 1) * LANES]

    tiles = GROUP_WIDTH // LANES

    def body(i, carry):
        r0 = pl.multiple_of(i * ROW_CHUNK, ROW_CHUNK)
        for c in range(tiles):
            ls = [lnat_s[gi * tiles + c, pl.ds(r0, ROW_CHUNK), :] for gi in range(N_GROUPS)]
            m = jnp.maximum(jnp.maximum(ls[0], ls[1]), ls[2])
            es = [jnp.exp(l - m) for l in ls]
            inv = 1.0 / (es[0] + es[1] + es[2])
            for gi in range(N_GROUPS):
                t = gi * tiles + c
                o_ref[0, pl.ds(r0, ROW_CHUNK), t * LANES:(t + 1) * LANES] = (
                    onat_s[t, pl.ds(r0, ROW_CHUNK), :] * (es[gi] * inv)).astype(BF16)
        return carry

    lax.fori_loop(0, seq // ROW_CHUNK, body, 0, unroll=2)


def _attention(qkv_groups):
    b = qkv_groups[0].shape[0]
    seq = qkv_groups[0].shape[1] * qkv_groups[0].shape[2]
    tiles = GROUP_WIDTH // LANES
    blk = pl.BlockSpec((1, seq, ATTN_WIDTH), lambda i: (i, 0, 0))
    return pl.pallas_call(
        functools.partial(_attn_kernel, seq=seq),
        grid=(b,),
        in_specs=[blk] * N_GROUPS,
        out_specs=blk,
        out_shape=jax.ShapeDtypeStruct((b, seq, ATTN_WIDTH), BF16),
        scratch_shapes=[
            pltpu.VMEM((seq, GROUP_WIDTH), F32),
            pltpu.VMEM((seq, GROUP_WIDTH), F32),
            pltpu.VMEM((N_GROUPS * tiles, seq, LANES), F32),
            pltpu.VMEM((N_GROUPS * tiles, seq, LANES), F32),
        ],
        compiler_params=_params("attention"),
        name="attention",
    )(*[a.reshape(b, seq, ATTN_WIDTH) for a in qkv_groups])


def _post_kernel(attn_ref, u_ref, vz_ref, ga_ref, gb_ref, x_ref, wa_ref, wb_ref, wo_ref,
                 lng_ref, lnb_ref, wcat_ref, bcat_ref, n2g_ref, wrh_ref, wrl_ref, br_ref,
                 x1_ref, hf_ref, e_ref, g_ref, cst_ref, cnt_ref, sgu_s, carry_s, merged_a, merged_b):
    step = pl.program_id(0)

    @pl.when(step == 0)
    def _():
        carry_s[...] = jnp.zeros_like(carry_s)
        merged_b[...] = jnp.zeros_like(merged_b)

    args = (attn_ref, u_ref, vz_ref, ga_ref, gb_ref, x_ref, wa_ref, wb_ref, wo_ref, lng_ref, lnb_ref, wcat_ref,
            bcat_ref, n2g_ref, wrh_ref, wrl_ref, br_ref, x1_ref, hf_ref, e_ref, g_ref, cst_ref, cnt_ref, sgu_s, carry_s)

    @pl.when(lax.rem(step, 2) == 0)
    def _():
        _post_step(*args, merged_a, merged_b)

    @pl.when(lax.rem(step, 2) == 1)
    def _():
        _post_step(*args, merged_b, merged_a)


def _post_step(attn_ref, u_ref, vz_ref, ga_ref, gb_ref, x_ref, wa_ref, wb_ref, wo_ref,
               lng_ref, lnb_ref, wcat_ref, bcat_ref, n2g_ref, wrh_ref, wrl_ref, br_ref,
               x1_ref, hf_ref, e_ref, g_ref, cst_ref, cnt_ref, sgu_s, carry_s, merged_w, merged_r):
    tm = x_ref.shape[0]
    counted = jnp.where(pl.program_id(0) > 0, 1.0, 0.0)

    lane = lax.broadcasted_iota(I32, (SGU_CHUNK, LANES), 1)
    low = lane < SGU_GROUP_DIM
    trow = lax.broadcasted_iota(I32, (SGU_CHUNK, 2 * SGU_CHUNK), 0)
    tcol = lax.broadcasted_iota(I32, (SGU_CHUNK, 2 * SGU_CHUNK), 1) % SGU_CHUNK
    causal = tcol <= trow

    def chunk(c):
        r0 = c * SGU_CHUNK
        u = _gelu(u_ref[pl.ds(r0, SGU_CHUNK), :].astype(F32))
        v = _gelu(vz_ref[pl.ds(r0, SGU_CHUNK), :].astype(F32))
        mu = jnp.mean(v, axis=-1, keepdims=True)
        vc = v - mu
        vn = vc * lax.rsqrt(jnp.mean(vc * vc, axis=-1, keepdims=True) + EPS)
        vn = (vn * lng_ref[...] + lnb_ref[...]).astype(BF16)
        zero = jnp.zeros((SGU_CHUNK, LANES), BF16)
        for j in range(SGU_WIDTH // LANES):
            vt = vn[:, j * LANES:(j + 1) * LANES]
            rhs = jnp.concatenate([jnp.where(low, vt, zero), jnp.where(low, zero, vt)], axis=0)
            wj = jnp.where(causal, wcat_ref[j], jnp.zeros((), BF16))
            mixed = _dot(wj, rhs) + bcat_ref[j]
            sgu_s[pl.ds(r0, SGU_CHUNK), j * LANES:(j + 1) * LANES] = (u[:, j * LANES:(j + 1) * LANES] * mixed).astype(BF16)

    eio = lax.broadcasted_iota(I32, (N_EXPERTS, tm), 0)
    state = {}
    top_v, onehots = [], []

    def route_logits(hi, lo):
        state["work"] = (_dot_nt(wrh_ref[...], hi) + _dot_nt(wrl_ref[...], hi) + _dot_nt(wrh_ref[...], lo)
                         + br_ref[...])

    def route_pick(k):
        work = state["work"]
        m = jnp.max(work, axis=0, keepdims=True)
        idx = jnp.min(jnp.where(work == m, eio, N_EXPERTS), axis=0, keepdims=True)
        oh = eio == idx
        state["work"] = jnp.where(oh, -jnp.inf, work)
        top_v.append(m)
        onehots.append(oh)
        e_ref[k:k + 1, :] = idx

    def route_finish():
        ex = [jnp.exp(v - top_v[0]) for v in top_v]
        inv = 1.0 / (ex[0] + ex[1] + ex[2] + ex[3])
        for k in range(TOP_K):
            g_ref[k:k + 1, :] = ex[k] * inv
        sel = jnp.zeros((N_EXPERTS, tm), F32)
        for oh in onehots:
            sel = jnp.where(oh, counted, sel)
        lane_e = lax.broadcasted_iota(I32, (N_EXPERTS, LANES), 1)
        run = carry_s[...]
        cst = jnp.zeros((N_EXPERTS, LANES), F32)
        for s in range(tm // TM_ROUTE):
            cst = jnp.where(lane_e == s, run, cst)
            run = run + jnp.sum(sel[:, s * TM_ROUTE:(s + 1) * TM_ROUTE], axis=1, keepdims=True)
        cst_ref[0] = cst.astype(I32)
        carry_s[...] = run
        cnt_ref[...] = run.astype(I32)

    n_parts = tm // SGU_CHUNK
    width = D_MODEL // n_parts
    x1_parts, y_a_parts = [], []
    for c in range(n_parts):
        cols = slice(c * width, (c + 1) * width)
        x1_parts.append(x_ref[:, cols] + _dot(merged_r[...], wo_ref[:, cols]))
        y_a_parts.append(_dot(attn_ref[...], wa_ref[:, cols]))
        chunk(c)
    x1 = jnp.concatenate(x1_parts, axis=1)
    x1_ref[...] = x1
    hf = x1 * lax.rsqrt(jnp.mean(x1 * x1, axis=-1, keepdims=True) + EPS) * n2g_ref[...]
    hi = hf.astype(BF16)
    hf_ref[...] = hi
    route_logits(hi, (hf - hi.astype(F32)).astype(BF16))
    for c in range(n_parts):
        cols = slice(c * width, (c + 1) * width)
        y_b = _dot(sgu_s[...], wb_ref[:, cols])
        merged_w[:, cols] = (jax.nn.sigmoid(ga_ref[:, cols].astype(F32)) * y_a_parts[c]
                             + jax.nn.sigmoid(gb_ref[:, cols].astype(F32)) * y_b).astype(BF16)
        if c < TOP_K:
            route_pick(c)
    for k in range(n_parts, TOP_K):
        route_pick(k)
    route_finish()


def _post(attn2, z2, gates2, x2, wa, wb, wo, lng, lnb, wcat, bcat, n2g, wrh, wrl, br):
    t = x2.shape[0]
    tm = TM_POST
    const = lambda *shape: pl.BlockSpec(shape, lambda i: (0,) * len(shape), pipeline_mode=pl.Buffered(1))
    n_tiles = t // tm
    cur = lambda i: jnp.minimum(i, n_tiles - 1)
    prev = lambda i: jnp.maximum(i - 1, 0)
    return pl.pallas_call(
        _post_kernel,
        grid=(n_tiles + 1,),
        in_specs=[
            pl.BlockSpec((tm, ATTN_WIDTH), lambda i: (cur(i), 0)),
            pl.BlockSpec((tm, SGU_WIDTH), lambda i: (cur(i), 0)),
            pl.BlockSpec((tm, SGU_WIDTH), lambda i: (cur(i), 1)),
            pl.BlockSpec((tm, D_MODEL), lambda i: (cur(i), 0)),
            pl.BlockSpec((tm, D_MODEL), lambda i: (cur(i), 1)),
            pl.BlockSpec((tm, D_MODEL), lambda i: (prev(i), 0)),
            const(ATTN_WIDTH, D_MODEL),
            const(SGU_WIDTH, D_MODEL),
            const(D_MODEL, D_MODEL),
            const(1, SGU_WIDTH),
            const(1, SGU_WIDTH),
            const(SGU_WIDTH // LANES, SGU_CHUNK, 2 * SGU_CHUNK),
            const(SGU_WIDTH // LANES, SGU_CHUNK, LANES),
            const(1, D_MODEL),
            const(N_EXPERTS, D_MODEL),
            const(N_EXPERTS, D_MODEL),
            const(N_EXPERTS, 1),
        ],
        out_specs=[
            pl.BlockSpec((tm, D_MODEL), lambda i: (prev(i), 0)),
            pl.BlockSpec((tm, D_MODEL), lambda i: (prev(i), 0)),
            pl.BlockSpec((TOP_K, tm), lambda i: (0, prev(i))),
            pl.BlockSpec((TOP_K, tm), lambda i: (0, prev(i))),
            pl.BlockSpec((1, N_EXPERTS, LANES), lambda i: (prev(i), 0, 0)),
            pl.BlockSpec((N_EXPERTS, LANES), lambda i: (0, 0)),
        ],
        out_shape=[
            jax.ShapeDtypeStruct((t, D_MODEL), F32),
            jax.ShapeDtypeStruct((t, D_MODEL), BF16),
            jax.ShapeDtypeStruct((TOP_K, t), I32),
            jax.ShapeDtypeStruct((TOP_K, t), F32),
            jax.ShapeDtypeStruct((t // tm, N_EXPERTS, LANES), I32),
            jax.ShapeDtypeStruct((N_EXPERTS, LANES), I32),
        ],
        scratch_shapes=[
            pltpu.VMEM((tm, SGU_WIDTH), BF16),
            pltpu.VMEM((N_EXPERTS, LANES), F32),
        ] + [pltpu.VMEM((tm, D_MODEL), BF16)] * 2,
        compiler_params=_params("post"),
        name="post",
    )(attn2, z2, z2, gates2, gates2, x2, wa, wb, wo, lng, lnb, wcat, bcat, n2g, wrh, wrl, br)


def _route_rows(e_ref, seen, tm):
    e = e_ref[...]
    eio = lax.broadcasted_iota(I32, (N_EXPERTS, tm), 0)
    onehots = [eio == e[k:k + 1, :] for k in range(TOP_K)]
    sel = jnp.zeros((N_EXPERTS, tm), F32)
    for oh in onehots:
        sel = jnp.where(oh, 1.0, sel)
    ti = lax.broadcasted_iota(I32, (tm, tm), 0)
    tj = lax.broadcasted_iota(I32, (tm, tm), 1)
    before = jnp.where(ti < tj, 1.0, 0.0).astype(BF16)
    slot = _dot(sel.astype(BF16), before)
    cnt = jnp.sum(sel, axis=1, keepdims=True)
    head = seen - SUBLANES * jnp.floor(seen * (1.0 / SUBLANES))
    tiles = jnp.where(cnt > 0.0, jnp.floor((head + cnt + (SUBLANES - 1)) * (1.0 / SUBLANES)), 0.0)
    xi = lax.broadcasted_iota(I32, (N_EXPERTS, N_EXPERTS), 0)
    xj = lax.broadcasted_iota(I32, (N_EXPERTS, N_EXPERTS), 1)
    lower = jnp.where(xj < xi, 1.0, 0.0).astype(BF16)
    off = _dot(lower, jnp.broadcast_to(tiles, (N_EXPERTS, LANES)).astype(BF16))[:, 0:1] * float(SUBLANES)
    row = off + head + slot
    return [jnp.sum(jnp.where(oh, row, 0.0), axis=0, keepdims=True) for oh in onehots], cnt


def _pick(rows, values, default):
    tm = rows[0].shape[1]
    rio = lax.broadcasted_iota(I32, (ROUTE_ROWS, tm), 0).astype(F32)
    out = default
    for r, v in zip(rows, values):
        out = jnp.where(rio == r, v, out)
    return out


def _wait_rows(buf, hbm_ref, sem, n):
    n = pl.multiple_of(n, SUBLANES)

    @pl.when(n > 0)
    def _():
        pltpu.make_async_copy(buf.at[pl.ds(0, n), :], hbm_ref.at[pl.ds(0, n), :], sem).wait()


def _dispatch_kernel(start_ref, off_ref, staged_ref, full_ref, tot_ref, flush_ref, za_ref, zb_ref, hf_ref, e_ref,
                     xs_ref, rows_ref, obuf, zbuf, seen_s, tail_s, first_s, sem, semz):
    tm = hf_ref.shape[0]
    i = pl.program_id(0)
    last = pl.num_programs(0) - 1
    cur = lax.rem(i, 2)
    spare = ROUTE_ROWS - SUBLANES

    @pl.when(i == 0)
    def _():
        seen_s[...] = jnp.zeros_like(seen_s)
        tail_s[...] = jnp.zeros_like(tail_s)

    @pl.when(i >= 2)
    def _():
        _wait_rows(obuf.at[cur], xs_ref, sem.at[cur], tot_ref[jnp.maximum(i - 2, 0)])

    rows, cnt = _route_rows(e_ref, seen_s[:, 0:1], tm)
    seen_s[...] = seen_s[...] + cnt
    for k in range(TOP_K):
        rows_ref[k:k + 1, :] = rows[k]
    pt = _pick(rows, [1.0] * TOP_K, 0.0).astype(BF16)
    obuf[cur] = _dot(pt, hf_ref[...])

    for x in range(N_EXPERTS):
        k = i * N_EXPERTS + x
        staged, full = staged_ref[k], full_ref[k]
        has = jnp.where(staged > 0, 1.0, 0.0)
        partial = jnp.where(staged > full, 1.0, 0.0)
        alone = jnp.where(full == 0, 1.0, 0.0)
        first = pl.multiple_of(jnp.where(staged > 0, off_ref[k], spare), SUBLANES)
        rest = pl.multiple_of(jnp.where(staged > 0, off_ref[k] + full, spare), SUBLANES)
        kept = tail_s[x]
        merged = obuf[cur, pl.ds(first, SUBLANES), :] + kept * has
        first_s[cur, x] = merged
        ending = merged * alone + obuf[cur, pl.ds(rest, SUBLANES), :] * (1.0 - alone)
        tail_s[x] = ending * partial + kept * (1.0 - has)

    for x in range(N_EXPERTS):
        n = pl.multiple_of(full_ref[i * N_EXPERTS + x], SUBLANES)
        dst = pl.multiple_of(start_ref[i * N_EXPERTS + x], SUBLANES)

        @pl.when(n > 0)
        def _(x=x, dst=dst):
            pltpu.make_async_copy(first_s.at[cur, x], xs_ref.at[pl.ds(dst, SUBLANES), :], sem.at[cur]).start()

        @pl.when(n > SUBLANES)
        def _(x=x, n=n, dst=dst):
            src = pl.multiple_of(off_ref[i * N_EXPERTS + x] + SUBLANES, SUBLANES)
            more = pl.multiple_of(n - SUBLANES, SUBLANES)
            pltpu.make_async_copy(obuf.at[cur, pl.ds(src, more), :],
                                  xs_ref.at[pl.ds(pl.multiple_of(dst + SUBLANES, SUBLANES), more), :], sem.at[cur]).start()

    @pl.when(i == last)
    def _():
        @pl.when(i >= 1)
        def _():
            _wait_rows(obuf.at[1 - cur], xs_ref, sem.at[1 - cur], tot_ref[jnp.maximum(i - 1, 0)])

        _wait_rows(obuf.at[cur], xs_ref, sem.at[cur], tot_ref[i])
        for x in range(N_EXPERTS):
            row = flush_ref[x]

            @pl.when(row >= 0)
            def _(x=x, row=row):
                cp = pltpu.make_async_copy(tail_s.at[x], xs_ref.at[pl.ds(pl.multiple_of(row, SUBLANES), SUBLANES), :], semz)
                cp.start()
                cp.wait()
        zbuf[...] = jnp.zeros_like(zbuf)

        def zero_region(z, act):
            n = zb_ref[z] - za_ref[z]
            big = n // EXPERT_TILE
            rest = pl.multiple_of(n - big * EXPERT_TILE, SUBLANES)

            def piece(row, nrows):
                return pltpu.make_async_copy(zbuf.at[pl.ds(0, nrows), :],
                                             xs_ref.at[pl.ds(pl.multiple_of(row, SUBLANES), nrows), :], semz)

            lax.fori_loop(0, big, lambda m, c: (act(piece(za_ref[z] + m * EXPERT_TILE, EXPERT_TILE)), c)[1], 0)

            @pl.when(rest > 0)
            def _():
                act(piece(za_ref[z] + big * EXPERT_TILE, rest))

        for z in range(N_EXPERTS + 1):
            zero_region(z, lambda cp: cp.start())
        for z in range(N_EXPERTS + 1):
            zero_region(z, lambda cp: cp.wait())


def _dispatch(start, off, staged, full, tot, flush, za, zb, hf, e4, cap):
    t = hf.shape[0]
    tm = TM_ROUTE
    tok = lambda i, *_: (0, i)
    return pl.pallas_call(
        _dispatch_kernel,
        grid_spec=pltpu.PrefetchScalarGridSpec(
            num_scalar_prefetch=8,
            grid=(t // tm,),
            in_specs=[
                pl.BlockSpec((tm, D_MODEL), lambda i, *_: (i, 0)),
                pl.BlockSpec((TOP_K, tm), tok),
            ],
            out_specs=[pl.BlockSpec(memory_space=pl.ANY), pl.BlockSpec((TOP_K, tm), tok)],
            scratch_shapes=[
                pltpu.VMEM((2, ROUTE_ROWS, D_MODEL), F32),
                pltpu.VMEM((EXPERT_TILE, D_MODEL), F32),
                pltpu.VMEM((N_EXPERTS, LANES), F32),
                pltpu.VMEM((N_EXPERTS, SUBLANES, D_MODEL), F32),
                pltpu.VMEM((2, N_EXPERTS, SUBLANES, D_MODEL), F32),
                pltpu.SemaphoreType.DMA((2,)),
                pltpu.SemaphoreType.DMA(()),
            ],
        ),
        out_shape=[jax.ShapeDtypeStruct((cap, D_MODEL), F32), jax.ShapeDtypeStruct((TOP_K, t), F32)],
        compiler_params=_params("dispatch"),
        name="dispatch",
    )(start, off, staged, full, tot, flush, za, zb, hf, e4)


def _expert_kernel(te_ref, nu_ref, xs_ref, wgu_ref, bgu_ref, wdn_ref, bdn_ref, ys_ref, wgu_s, wdn_s):
    i = pl.program_id(0)

    @pl.when(i < nu_ref[0])
    def _():
        @pl.when((i == 0) | (te_ref[i] != te_ref[jnp.maximum(i - 1, 0)]))
        def _():
            wgu_s[...] = wgu_ref[0].astype(BF16)
            wdn_s[...] = wdn_ref[0].astype(BF16)

        x = xs_ref[...].astype(BF16)
        gu = _dot(x, wgu_s[...]) + bgu_ref[0]
        gate = jnp.minimum(gu[:, :D_EXPERT], SWIGLU_LIMIT)
        up = jnp.clip(gu[:, D_EXPERT:], -SWIGLU_LIMIT, SWIGLU_LIMIT)
        act = (up + 1.0) * (gate * jax.nn.sigmoid(SWIGLU_ALPHA * gate))
        ys_ref[...] = _dot(act.astype(BF16), wdn_s[...]) + bdn_ref[0]

    @pl.when(i >= nu_ref[0])
    def _():
        ys_ref[...] = jnp.zeros_like(ys_ref)


def _experts(tile_expert, n_used, xs, wgu, bgu, wdn, bdn):
    cap = xs.shape[0]
    n_tiles = cap // EXPERT_TILE
    row = lambda i, te, nu: (jnp.minimum(i, nu[0] - 1), 0)
    by_expert = lambda i, te, nu: (te[i], 0, 0)
    return pl.pallas_call(
        _expert_kernel,
        grid_spec=pltpu.PrefetchScalarGridSpec(
            num_scalar_prefetch=2,
            grid=(n_tiles,),
            in_specs=[
                pl.BlockSpec((EXPERT_TILE, D_MODEL), row),
                pl.BlockSpec((1, D_MODEL, 2 * D_EXPERT), by_expert),
                pl.BlockSpec((1, 1, 2 * D_EXPERT), by_expert),
                pl.BlockSpec((1, D_EXPERT, D_MODEL), by_expert),
                pl.BlockSpec((1, 1, D_MODEL), by_expert),
            ],
            out_specs=pl.BlockSpec((EXPERT_TILE, D_MODEL), lambda i, te, nu: (i, 0)),
            scratch_shapes=[
                pltpu.VMEM((D_MODEL, 2 * D_EXPERT), BF16),
                pltpu.VMEM((D_EXPERT, D_MODEL), BF16),
            ],
        ),
        out_shape=jax.ShapeDtypeStruct((cap, D_MODEL), F32),
        compiler_params=_params("experts"),
        name="experts",
    )(tile_expert, n_used, xs, wgu, bgu, wdn, bdn)


def _combine_kernel(start_ref, off_ref, cnt_ref, tot_ref, x1_ref, rows_ref, g_ref, ys_ref, o_ref, sbuf, sem):
    i = pl.program_id(0)
    last = pl.num_programs(0) - 1
    cur = lax.rem(i, 2)

    def fetch(step, b):
        for x in range(N_EXPERTS):
            n = pl.multiple_of(cnt_ref[step * N_EXPERTS + x], SUBLANES)

            @pl.when(n > 0)
            def _(x=x, n=n):
                src = pl.multiple_of(start_ref[step * N_EXPERTS + x], SUBLANES)
                dst = pl.multiple_of(off_ref[step * N_EXPERTS + x], SUBLANES)
                pltpu.make_async_copy(ys_ref.at[pl.ds(src, n), :], sbuf.at[b, pl.ds(dst, n), :], sem.at[b]).start()

    @pl.when(i == 0)
    def _():
        sbuf[...] = jnp.zeros_like(sbuf)
        fetch(i, cur)

    @pl.when(i < last)
    def _():
        fetch(i + 1, 1 - cur)

    rows = [rows_ref[k:k + 1, :] for k in range(TOP_K)]
    pt = _pick(rows, [1.0] * TOP_K, 0.0).astype(BF16)
    g = g_ref[...]
    gate = jnp.sum(_pick(rows, [g[k:k + 1, :] for k in range(TOP_K)], 0.0), axis=1, keepdims=True)
    _wait_rows(sbuf.at[cur], ys_ref, sem.at[cur], tot_ref[i])
    weighted = (sbuf[cur] * gate).astype(BF16)
    picked = lax.dot_general(pt, weighted, _TN, preferred_element_type=F32)
    o_ref[...] = x1_ref[...] + picked


def _combine(start, off, cnt, tot, x1, rows4, g4, ys):
    t = x1.shape[0]
    tm = TM_ROUTE
    return pl.pallas_call(
        _combine_kernel,
        grid_spec=pltpu.PrefetchScalarGridSpec(
            num_scalar_prefetch=4,
            grid=(t // tm,),
            in_specs=[
                pl.BlockSpec((tm, D_MODEL), lambda i, *_: (i, 0)),
                pl.BlockSpec((TOP_K, tm), lambda i, *_: (0, i)),
                pl.BlockSpec((TOP_K, tm), lambda i, *_: (0, i)),
                pl.BlockSpec(memory_space=pl.ANY),
            ],
            out_specs=pl.BlockSpec((tm, D_MODEL), lambda i, *_: (i, 0)),
            scratch_shapes=[
                pltpu.VMEM((2, ROUTE_ROWS, D_MODEL), F32),
                pltpu.SemaphoreType.DMA((2,)),
            ],
        ),
        out_shape=jax.ShapeDtypeStruct((t, D_MODEL), F32),
        compiler_params=_params("combine"),
        name="combine",
    )(start, off, cnt, tot, x1, rows4, g4, ys)


def _layer(x, positions, norm1_g, w_in, q_norm_g, k_norm_g, sgu_ln_g, sgu_ln_b, w_spatial, b_spatial,
           w_branch_a, w_branch_b, w_out, norm2_g, w_router, b_router, w_gate_up, b_gate_up, w_down, b_down):
    b, seq, _ = x.shape
    t = b * seq
    x2 = x.reshape(t, D_MODEL)

    inv_freq = ROPE_THETA ** (-jnp.arange(ROT_HALF, dtype=F32) / ROT_HALF)
    invf = jnp.tile(inv_freq, LANES // ROT_HALF).reshape(1, LANES)
    dim_of_lane = [(l // (2 * ROT_HALF)) * ROT_HALF + l % ROT_HALF for l in range(LANES)]
    qg = q_norm_g[jnp.array(dim_of_lane)].reshape(1, LANES)
    kg = k_norm_g[jnp.array(dim_of_lane)].reshape(1, LANES)
    w_bf = w_in.astype(BF16)
    qk = w_bf[:, :2 * ATTN_WIDTH].reshape(D_MODEL, 2 * ATTN_WIDTH // LANES, 2, 2, ROT_HALF)
    qk = qk.transpose(0, 1, 3, 2, 4).reshape(D_MODEL, 2 * ATTN_WIDTH)
    w_in_p = jnp.concatenate([qk, w_bf[:, 2 * ATTN_WIDTH:]], axis=1)
    n_pairs = SGU_WIDTH // LANES
    wcat = w_spatial.reshape(n_pairs, 2, SGU_CHUNK, SGU_CHUNK).transpose(0, 2, 1, 3).reshape(n_pairs, SGU_CHUNK, 2 * SGU_CHUNK)
    bcat = jnp.repeat(b_spatial.reshape(n_pairs, 2, SGU_CHUNK).transpose(0, 2, 1), SGU_GROUP_DIM, axis=2)
    wr_t = w_router.T
    wr_hi = wr_t.astype(BF16)
    wr_lo = (wr_t - wr_hi.astype(F32)).astype(BF16)

    *qkv_groups, z, gates = _inproj(x2, norm1_g.reshape(1, D_MODEL), w_in_p, positions.reshape(-1, LANES // ROT_HALF), invf, qg, kg, b, seq)
    attn = _attention(qkv_groups)
    x1, hf, e4, g4, cst, counts = _post(
        attn.reshape(t, ATTN_WIDTH), z, gates, x2,
        w_branch_a.astype(BF16), w_branch_b.astype(BF16), w_out.astype(BF16),
        sgu_ln_g.reshape(1, SGU_WIDTH), sgu_ln_b.reshape(1, SGU_WIDTH), wcat.astype(BF16), bcat,
        norm2_g.reshape(1, D_MODEL), wr_hi, wr_lo, b_router.reshape(N_EXPERTS, 1))

    steps_per_tile = TM_POST // TM_ROUTE
    n_steps = t // TM_ROUTE
    before = cst[:, :, :steps_per_tile].transpose(0, 2, 1).reshape(n_steps, N_EXPERTS)
    total = counts[:, 0]
    cnt = jnp.concatenate([before[1:], total[None, :]], axis=0) - before
    head = before % SUBLANES
    staged = jnp.where(cnt > 0, (head + cnt + SUBLANES - 1) // SUBLANES * SUBLANES, 0)
    full = jnp.where(cnt > 0, (head + cnt) // SUBLANES * SUBLANES, 0)
    off = (jnp.cumsum(staged, axis=1) - staged).astype(I32).reshape(-1)
    tot_staged = jnp.sum(staged, axis=1).astype(I32)
    tot_full = jnp.sum(full, axis=1).astype(I32)
    region = (total + EXPERT_TILE - 1) // EXPERT_TILE * EXPERT_TILE
    rend = jnp.cumsum(region)
    rstart = rend - region
    start = (rstart[None, :] + before - head).astype(I32).reshape(-1)
    bound = t * TOP_K + N_EXPERTS * (EXPERT_TILE - 1)
    cap = (bound + EXPERT_TILE - 1) // EXPERT_TILE * EXPERT_TILE
    n_tiles = cap // EXPERT_TILE
    tile_expert = jnp.sum(jnp.arange(n_tiles, dtype=I32)[:, None] * EXPERT_TILE >= rend[None, :], axis=1)
    tile_expert = jnp.minimum(tile_expert, N_EXPERTS - 1).astype(I32)
    n_used = (rend[-1:] // EXPERT_TILE).astype(I32)
    filled = (total + SUBLANES - 1) // SUBLANES * SUBLANES
    flush = jnp.where(total % SUBLANES != 0, rstart + total // SUBLANES * SUBLANES, -1).astype(I32)
    za = jnp.concatenate([rstart + filled, rend[-1:]]).astype(I32)
    zb = jnp.concatenate([rend, jnp.full((1,), cap, rend.dtype)]).astype(I32)
    staged = staged.astype(I32).reshape(-1)
    full = full.astype(I32).reshape(-1)

    xs, rows4 = _dispatch(start, off, staged, full, tot_full, flush, za, zb, hf, e4, cap)
    ys = _experts(tile_expert, n_used, xs,
                  w_gate_up, b_gate_up.reshape(N_EXPERTS, 1, 2 * D_EXPERT),
                  w_down, b_down.reshape(N_EXPERTS, 1, D_MODEL))
    out = _combine(start, off, staged, tot_staged, x1, rows4, g4, ys)
    return out.reshape(b, seq, D_MODEL)


def kernel(x, positions, norm1_g, w_in, q_norm_g, k_norm_g, sgu_ln_g, sgu_ln_b, w_spatial, b_spatial, w_branch_a, w_branch_b, w_out, norm2_g, w_router, b_router, w_gate_up, b_gate_up, w_down, b_down):
    for layer in range(norm1_g.shape[0]):
        x = _layer(x, positions, norm1_g[layer], w_in[layer], q_norm_g[layer], k_norm_g[layer],
                   sgu_ln_g[layer], sgu_ln_b[layer], w_spatial[layer], b_spatial[layer],
                   w_branch_a[layer], w_branch_b[layer], w_out[layer], norm2_g[layer],
                   w_router[layer], b_router[layer], w_gate_up[layer], b_gate_up[layer],
                   w_down[layer], b_down[layer])
    return x
```

```python
import functools

import jax
import jax.numpy as jnp
from jax import lax
from jax.experimental import pallas as pl
from jax.experimental.pallas import tpu as pltpu

F32 = jnp.float32
BF16 = jnp.bfloat16
I32 = jnp.int32

D_MODEL = 1024
HEAD_DIM = 64
ROT_HALF = HEAD_DIM // 2
LOG2E = 1.4426950408889634
Q_SCALE = HEAD_DIM ** -0.5 * LOG2E
ATTN_GROUPS = ((128, 1), (512, 4), (2048, 16))
N_GROUPS = len(ATTN_GROUPS)
GROUP_WIDTH = 256
ATTN_WIDTH = N_GROUPS * GROUP_WIDTH
BLK = 128
ROPE_THETA = 10000.0
SGU_CHUNK = 128
SGU_GROUP_DIM = 64
SGU_WIDTH = 768
GATE_WIDTH = 2 * D_MODEL
N_EXPERTS = 32
TOP_K = 4
D_EXPERT = D_MODEL
SWIGLU_LIMIT = 7.0
SWIGLU_ALPHA = 1.702
EPS = 1e-6
NEG_INF = -1e30
LANES = 128

TM_INPROJ = 512
TM_POST = 512
TM_ROUTE = 256
SUBLANES = 8
ROUTE_ROWS = TM_ROUTE * TOP_K + N_EXPERTS * 2 * SUBLANES
EXPERT_TILE = 512
ROW_CHUNK = 256

_NT = (((1,), (1,)), ((), ()))
_TN = (((0,), (0,)), ((), ()))


def _dot(a, b):
    return jnp.dot(a, b, preferred_element_type=F32)


def _dot_nt(a, b):
    return lax.dot_general(a, b, _NT, preferred_element_type=F32)


def _gelu(x):
    return 0.5 * x * (1.0 + lax.erf(x * 0.7071067811865476))


VMEM_MIB = {"inproj": 56, "attention": 48, "post": 48, "dispatch": 48, "experts": 56, "combine": 48}


def _params(name):
    return pltpu.CompilerParams(dimension_semantics=("arbitrary",), vmem_limit_bytes=VMEM_MIB[name] * 1024 * 1024)


def _split3(x):
    a = x.astype(BF16)
    r = x - a.astype(F32)
    b = r.astype(BF16)
    return a, b, (r - b.astype(F32)).astype(BF16)


def _inproj_kernel(x_ref, g_ref, w_ref, pos_ref, invf_ref, qg_ref, kg_ref,
                   o0_ref, o1_ref, o2_ref, oz_ref, og_ref, raw_a, raw_b, stage_s, cos_s, sin_s):
    step = pl.program_id(0)

    @pl.when(step == 0)
    def _():
        raw_b[...] = jnp.zeros_like(raw_b)

    args = (x_ref, g_ref, w_ref, pos_ref, invf_ref, qg_ref, kg_ref, o0_ref, o1_ref, o2_ref, oz_ref, og_ref)

    @pl.when(lax.rem(step, 2) == 0)
    def _():
        _inproj_step(*args, raw_a, raw_b, stage_s, cos_s, sin_s)

    @pl.when(lax.rem(step, 2) == 1)
    def _():
        _inproj_step(*args, raw_b, raw_a, stage_s, cos_s, sin_s)


def _inproj_step(x_ref, g_ref, w_ref, pos_ref, invf_ref, qg_ref, kg_ref,
                 o0_ref, o1_ref, o2_ref, oz_ref, og_ref, raw_w, raw_r, stage_s, cos_s, sin_s):
    tm = x_ref.shape[0]
    x = x_ref[...]
    y = x * lax.rsqrt(jnp.mean(x * x, axis=-1, keepdims=True) + EPS)
    h = (y * g_ref[...]).astype(BF16)

    def project(lo):
        p = _dot(h, w_ref[:, lo:lo + GROUP_WIDTH])
        zc = 3 * ATTN_WIDTH
        gc = zc + 2 * SGU_WIDTH
        if lo < zc:
            raw_w[:, lo:lo + GROUP_WIDTH] = p
        elif lo < gc:
            oz_ref[:, lo - zc:lo - zc + GROUP_WIDTH] = p.astype(BF16)
        else:
            og_ref[:, lo - gc:lo - gc + GROUP_WIDTH] = p.astype(BF16)

    def rotary_tables():
        per_row = LANES // ROT_HALF
        lane = lax.broadcasted_iota(I32, (tm // per_row, LANES), 1)
        p4 = pos_ref[...].astype(F32)
        posd = p4[:, per_row - 1:per_row]
        for j in range(per_row - 2, -1, -1):
            posd = jnp.where(lane < (j + 1) * ROT_HALF, p4[:, j:j + 1], posd)
        ang = posd * invf_ref[...]
        cparts = _split3(jnp.cos(ang))
        sparts = _split3(jnp.sin(ang))
        src = lax.broadcasted_iota(I32, (LANES, LANES), 0)
        dst = lax.broadcasted_iota(I32, (LANES, LANES), 1)
        for j in range(per_row):
            pick = src == j * ROT_HALF + dst % ROT_HALF
            spread = jnp.where(pick, 1.0, 0.0).astype(BF16)
            spread_neg = jnp.where(pick, jnp.where(dst < 2 * ROT_HALF, -1.0, 1.0), 0.0).astype(BF16)
            cos_s[pl.ds(j, tm // per_row, stride=per_row), :] = sum(_dot(p, spread) for p in cparts)
            sin_s[pl.ds(j, tm // per_row, stride=per_row), :] = sum(_dot(p, spread_neg) for p in sparts)

    li = (lax.broadcasted_iota(I32, (LANES, LANES), 0) // ROT_HALF) % 2
    lj = (lax.broadcasted_iota(I32, (LANES, LANES), 1) // ROT_HALF) % 2
    head_sum = jnp.where(li == lj, 1.0, 0.0).astype(BF16)
    outs = (o0_ref, o1_ref, o2_ref)

    def finish(gi, d, ti, mode, gain_ref, c, stage):
        col = ti * ATTN_WIDTH + gi * GROUP_WIDTH
        t = raw_r[:, col + c * LANES:col + (c + 1) * LANES]
        if mode != "v":
            ss = t * t
            hi = ss.astype(BF16)
            lo = (ss - hi.astype(F32)).astype(BF16)
            ms = (_dot(hi, head_sum) + _dot(lo, head_sum)) * (1.0 / HEAD_DIM)
            t = t * lax.rsqrt(ms + EPS) * gain_ref[...]
            t = t * cos_s[...] + pltpu.roll(t, 2 * ROT_HALF, 1) * sin_s[...]
            if mode == "q":
                t = t * Q_SCALE
        lanes = slice(ti * GROUP_WIDTH + c * LANES, ti * GROUP_WIDTH + (c + 1) * LANES)
        if d == 1:
            outs[gi][0, 0, :, lanes] = t.astype(BF16)
        else:
            stage_s[stage] = t
            for r in range(d):
                outs[gi][0, r, :, lanes] = stage_s[stage, pl.ds(r, tm // d, stride=d), :].astype(BF16)

    pieces = [rotary_tables]
    stage = 0
    for gi, (_, d) in enumerate(ATTN_GROUPS):
        for ti, (mode, gain_ref) in enumerate((("q", qg_ref), ("k", kg_ref), ("v", None))):
            for c in range(GROUP_WIDTH // LANES):
                pieces.append(functools.partial(finish, gi, d, ti, mode, gain_ref, c, stage))
                stage += d > 1

    chunks = list(range(0, w_ref.shape[1], GROUP_WIDTH))
    lead = 3
    for lo in chunks[:lead]:
        project(lo)
    rest = chunks[lead:]
    for k in range(max(len(rest), len(pieces))):
        if k < len(pieces):
            pieces[k]()
        if k < len(rest):
            project(rest[k])


def _inproj(x2, g, w, pos2, invf, qg, kg, b, seq):
    t = x2.shape[0]
    n = w.shape[1]
    tm = TM_INPROJ
    nj = seq // tm
    n_tiles = b * nj
    n_stage = sum(1 for _, d in ATTN_GROUPS if d > 1) * 3 * (GROUP_WIDTH // LANES)
    proj_row = lambda s: (jnp.minimum(s, n_tiles - 1), 0)
    done_row = lambda s: (jnp.maximum(s - 1, 0), 0)
    done_blk = lambda s: (jnp.maximum(s - 1, 0) // nj, 0, jnp.maximum(s - 1, 0) % nj, 0)
    const = lambda s: (0, 0)
    per_row = LANES // ROT_HALF
    return pl.pallas_call(
        _inproj_kernel,
        grid=(n_tiles + 1,),
        in_specs=[
            pl.BlockSpec((tm, D_MODEL), proj_row),
            pl.BlockSpec((1, D_MODEL), const),
            pl.BlockSpec((D_MODEL, n), const, pipeline_mode=pl.Buffered(1)),
            pl.BlockSpec((tm // per_row, per_row), done_row),
            pl.BlockSpec((1, LANES), const),
            pl.BlockSpec((1, LANES), const),
            pl.BlockSpec((1, LANES), const),
        ],
        out_specs=[pl.BlockSpec((1, d, tm // d, ATTN_WIDTH), done_blk) for _, d in ATTN_GROUPS] + [
            pl.BlockSpec((tm, 2 * SGU_WIDTH), proj_row),
            pl.BlockSpec((tm, GATE_WIDTH), proj_row),
        ],
        out_shape=[jax.ShapeDtypeStruct((b, d, seq // d, ATTN_WIDTH), BF16) for _, d in ATTN_GROUPS] + [
            jax.ShapeDtypeStruct((t, 2 * SGU_WIDTH), BF16),
            jax.ShapeDtypeStruct((t, GATE_WIDTH), BF16),
        ],
        scratch_shapes=[
            pltpu.VMEM((tm, 3 * ATTN_WIDTH), F32),
            pltpu.VMEM((tm, 3 * ATTN_WIDTH), F32),
            pltpu.VMEM((n_stage, tm, LANES), F32),
            pltpu.VMEM((tm, LANES), F32),
            pltpu.VMEM((tm, LANES), F32),
        ],
        compiler_params=_params("inproj"),
        name="inproj",
    )(x2, g, w, pos2, invf, qg, kg)


def _attn_blocks(qkv_ref, ores_s, lres_s, seq, d):
    sub = seq // d
    nb = sub // BLK
    win = 2 * BLK if nb > 1 else BLK
    lane = lax.broadcasted_iota(I32, (BLK, GROUP_WIDTH), 1)
    q_head = 2 * (lane // LANES) + (lane // ROT_HALF) % 2
    qi = lax.broadcasted_iota(I32, (BLK, win), 0)
    kj = lax.broadcasted_iota(I32, (BLK, win), 1)
    masks = {shift: lax.bitcast_convert_type(qi - kj + shift, jnp.uint32) <= jnp.uint32(BLK) for shift in (0, BLK)}

    for idx in range(seq // BLK):
        r, n = idx % d, idx // d
        row0 = r * sub + n * BLK
        kn = max(n - 1, 0) if nb > 1 else n
        k0 = r * sub + kn * BLK
        valid = masks[(n - kn) * BLK]
        q = qkv_ref[0, pl.ds(row0, BLK), 0:GROUP_WIDTH]
        kw = qkv_ref[0, pl.ds(k0, win), GROUP_WIDTH:2 * GROUP_WIDTH]
        vw = qkv_ref[0, pl.ds(k0, win), 2 * GROUP_WIDTH:3 * GROUP_WIDTH]
        zero = jnp.zeros_like(q)
        qm = jnp.concatenate([jnp.where(q_head == h, q, zero) for h in range(4)], axis=0)
        s = _dot_nt(qm, kw)
        ps, ms, ls = [], [], []
        for h in range(4):
            sh = jnp.where(valid, s[h * BLK:(h + 1) * BLK], NEG_INF)
            m = jnp.max(sh, axis=-1, keepdims=True)
            p = jnp.exp2(sh - m)
            ls.append(jnp.sum(p, axis=-1, keepdims=True))
            ms.append(m)
            ps.append(p.astype(BF16))
        pv = _dot(jnp.concatenate(ps, axis=0), vw)
        for h in range(4):
            lanes = slice(h * HEAD_DIM, (h + 1) * HEAD_DIM)
            ores_s[pl.ds(row0, BLK), lanes] = pv[h * BLK:(h + 1) * BLK, lanes] * (1.0 / ls[h])
            lres_s[pl.ds(row0, BLK), lanes] = jnp.broadcast_to(ms[h] + jnp.log2(ls[h]), (BLK, HEAD_DIM))


def _attn_kernel(g0_ref, g1_ref, g2_ref, o_ref, ores_s, lres_s, onat_s, lnat_s, *, seq):
    for gi, ((_, d), qkv_ref) in enumerate(zip(ATTN_GROUPS, (g0_ref, g1_ref, g2_ref))):
        _attn_blocks(qkv_ref, ores_s, lres_s, seq, d)
        sub = seq // d
        for r in range(d):
            for c in range(GROUP_WIDTH // LANES):
                t = gi * (GROUP_WIDTH // LANES) + c
                onat_s[t, pl.ds(r, sub, stride=d), :] = ores_s[r * sub:(r + 1) * sub, c * LANES:(c + 1) * LANES]
                lnat_s[t, pl.ds(r, sub, stride=d), :] = lres_s[r * sub:(r + 1) * sub, c * LANES:(c + 1) * LANES]

    tiles = GROUP_WIDTH // LANES

    def body(i, carry):
        r0 = pl.multiple_of(i * ROW_CHUNK, ROW_CHUNK)
        for c in range(tiles):
            ls = [lnat_s[gi * tiles + c, pl.ds(r0, ROW_CHUNK), :] for gi in range(N_GROUPS)]
            m = jnp.maximum(jnp.maximum(ls[0], ls[1]), ls[2])
            es = [jnp.exp2(l - m) for l in ls]
            inv = 1.0 / (es[0] + es[1] + es[2])
            for gi in range(N_GROUPS):
                t = gi * tiles + c
                o_ref[0, pl.ds(r0, ROW_CHUNK), t * LANES:(t + 1) * LANES] = (
                    onat_s[t, pl.ds(r0, ROW_CHUNK), :] * (es[gi] * inv)).astype(BF16)
        return carry

    lax.fori_loop(0, seq // ROW_CHUNK, body, 0, unroll=2)


def _attention(qkv_groups):
    b = qkv_groups[0].shape[0]
    seq = qkv_groups[0].shape[1] * qkv_groups[0].shape[2]
    tiles = GROUP_WIDTH // LANES
    blk = pl.BlockSpec((1, seq, ATTN_WIDTH), lambda i: (i, 0, 0))
    return pl.pallas_call(
        functools.partial(_attn_kernel, seq=seq),
        grid=(b,),
        in_specs=[blk] * N_GROUPS,
        out_specs=blk,
        out_shape=jax.ShapeDtypeStruct((b, seq, ATTN_WIDTH), BF16),
        scratch_shapes=[
            pltpu.VMEM((seq, GROUP_WIDTH), F32),
            pltpu.VMEM((seq, GROUP_WIDTH), F32),
            pltpu.VMEM((N_GROUPS * tiles, seq, LANES), F32),
            pltpu.VMEM((N_GROUPS * tiles, seq, LANES), F32),
        ],
        compiler_params=_params("attention"),
        name="attention",
    )(*[a.reshape(b, seq, ATTN_WIDTH) for a in qkv_groups])


def _post_kernel(attn_ref, u_ref, vz_ref, ga_ref, gb_ref, x_ref, wa_ref, wb_ref, wo_ref,
                 lng_ref, lnb_ref, wcat_ref, bcat_ref, n2g_ref, wrh_ref, wrl_ref, br_ref,
                 x1_ref, hf_ref, e_ref, g_ref, cst_ref, cnt_ref, sgu_s, carry_s, merged_a, merged_b):
    step = pl.program_id(0)

    @pl.when(step == 0)
    def _():
        carry_s[...] = jnp.zeros_like(carry_s)
        merged_b[...] = jnp.zeros_like(merged_b)

    args = (attn_ref, u_ref, vz_ref, ga_ref, gb_ref, x_ref, wa_ref, wb_ref, wo_ref, lng_ref, lnb_ref, wcat_ref,
            bcat_ref, n2g_ref, wrh_ref, wrl_ref, br_ref, x1_ref, hf_ref, e_ref, g_ref, cst_ref, cnt_ref, sgu_s, carry_s)

    @pl.when(lax.rem(step, 2) == 0)
    def _():
        _post_step(*args, merged_a, merged_b)

    @pl.when(lax.rem(step, 2) == 1)
    def _():
        _post_step(*args, merged_b, merged_a)


def _post_step(attn_ref, u_ref, vz_ref, ga_ref, gb_ref, x_ref, wa_ref, wb_ref, wo_ref,
               lng_ref, lnb_ref, wcat_ref, bcat_ref, n2g_ref, wrh_ref, wrl_ref, br_ref,
               x1_ref, hf_ref, e_ref, g_ref, cst_ref, cnt_ref, sgu_s, carry_s, merged_w, merged_r):
    tm = x_ref.shape[0]
    counted = jnp.where(pl.program_id(0) > 0, 1.0, 0.0)

    lane = lax.broadcasted_iota(I32, (SGU_CHUNK, LANES), 1)
    low = lane < SGU_GROUP_DIM
    trow = lax.broadcasted_iota(I32, (SGU_CHUNK, 2 * SGU_CHUNK), 0)
    tcol = lax.broadcasted_iota(I32, (SGU_CHUNK, 2 * SGU_CHUNK), 1) % SGU_CHUNK
    causal = tcol <= trow

    def chunk(c):
        r0 = c * SGU_CHUNK
        u = _gelu(u_ref[pl.ds(r0, SGU_CHUNK), :].astype(F32))
        v = _gelu(vz_ref[pl.ds(r0, SGU_CHUNK), :].astype(F32))
        mu = jnp.mean(v, axis=-1, keepdims=True)
        vc = v - mu
        vn = vc * lax.rsqrt(jnp.mean(vc * vc, axis=-1, keepdims=True) + EPS)
        vn = (vn * lng_ref[...] + lnb_ref[...]).astype(BF16)
        zero = jnp.zeros((SGU_CHUNK, LANES), BF16)
        for j in range(SGU_WIDTH // LANES):
            vt = vn[:, j * LANES:(j + 1) * LANES]
            rhs = jnp.concatenate([jnp.where(low, vt, zero), jnp.where(low, zero, vt)], axis=0)
            wj = jnp.where(causal, wcat_ref[j], jnp.zeros((), BF16))
            mixed = _dot(wj, rhs) + bcat_ref[j]
            sgu_s[pl.ds(r0, SGU_CHUNK), j * LANES:(j + 1) * LANES] = (u[:, j * LANES:(j + 1) * LANES] * mixed).astype(BF16)

    eio = lax.broadcasted_iota(I32, (N_EXPERTS, tm), 0)
    state = {}
    top_v, onehots = [], []

    def route_logits(hi, lo):
        state["work"] = (_dot_nt(wrh_ref[...], hi) + _dot_nt(wrl_ref[...], hi) + _dot_nt(wrh_ref[...], lo)
                         + br_ref[...])

    def route_pick(k):
        work = state["work"]
        m = jnp.max(work, axis=0, keepdims=True)
        idx = jnp.min(jnp.where(work == m, eio, N_EXPERTS), axis=0, keepdims=True)
        oh = eio == idx
        state["work"] = jnp.where(oh, -jnp.inf, work)
        top_v.append(m)
        onehots.append(oh)
        e_ref[k:k + 1, :] = idx

    def route_finish():
        ex = [jnp.exp(v - top_v[0]) for v in top_v]
        inv = 1.0 / (ex[0] + ex[1] + ex[2] + ex[3])
        for k in range(TOP_K):
            g_ref[k:k + 1, :] = ex[k] * inv
        sel = jnp.zeros((N_EXPERTS, tm), F32)
        for oh in onehots:
            sel = jnp.where(oh, counted, sel)
        lane_e = lax.broadcasted_iota(I32, (N_EXPERTS, LANES), 1)
        run = carry_s[...]
        cst = jnp.zeros((N_EXPERTS, LANES), F32)
        for s in range(tm // TM_ROUTE):
            cst = jnp.where(lane_e == s, run, cst)
            run = run + jnp.sum(sel[:, s * TM_ROUTE:(s + 1) * TM_ROUTE], axis=1, keepdims=True)
        cst_ref[0] = cst.astype(I32)
        carry_s[...] = run
        cnt_ref[...] = run.astype(I32)

    n_parts = tm // SGU_CHUNK
    width = D_MODEL // n_parts
    x1_parts, y_a_parts = [], []
    for c in range(n_parts):
        cols = slice(c * width, (c + 1) * width)
        x1_parts.append(x_ref[:, cols] + _dot(merged_r[...], wo_ref[:, cols]))
        y_a_parts.append(_dot(attn_ref[...], wa_ref[:, cols]))
        chunk(c)
    x1 = jnp.concatenate(x1_parts, axis=1)
    x1_ref[...] = x1
    hf = x1 * lax.rsqrt(jnp.mean(x1 * x1, axis=-1, keepdims=True) + EPS) * n2g_ref[...]
    hi = hf.astype(BF16)
    hf_ref[...] = hi
    route_logits(hi, (hf - hi.astype(F32)).astype(BF16))
    for c in range(n_parts):
        cols = slice(c * width, (c + 1) * width)
        y_b = _dot(sgu_s[...], wb_ref[:, cols])
        merged_w[:, cols] = (jax.nn.sigmoid(ga_ref[:, cols].astype(F32)) * y_a_parts[c]
                             + jax.nn.sigmoid(gb_ref[:, cols].astype(F32)) * y_b).astype(BF16)
        if c < TOP_K:
            route_pick(c)
    for k in range(n_parts, TOP_K):
        route_pick(k)
    route_finish()


def _post(attn2, z2, gates2, x2, wa, wb, wo, lng, lnb, wcat, bcat, n2g, wrh, wrl, br):
    t = x2.shape[0]
    tm = TM_POST
    const = lambda *shape: pl.BlockSpec(shape, lambda i: (0,) * len(shape), pipeline_mode=pl.Buffered(1))
    n_tiles = t // tm
    cur = lambda i: jnp.minimum(i, n_tiles - 1)
    prev = lambda i: jnp.maximum(i - 1, 0)
    return pl.pallas_call(
        _post_kernel,
        grid=(n_tiles + 1,),
        in_specs=[
            pl.BlockSpec((tm, ATTN_WIDTH), lambda i: (cur(i), 0)),
            pl.BlockSpec((tm, SGU_WIDTH), lambda i: (cur(i), 0)),
            pl.BlockSpec((tm, SGU_WIDTH), lambda i: (cur(i), 1)),
            pl.BlockSpec((tm, D_MODEL), lambda i: (cur(i), 0)),
            pl.BlockSpec((tm, D_MODEL), lambda i: (cur(i), 1)),
            pl.BlockSpec((tm, D_MODEL), lambda i: (prev(i), 0)),
            const(ATTN_WIDTH, D_MODEL),
            const(SGU_WIDTH, D_MODEL),
            const(D_MODEL, D_MODEL),
            const(1, SGU_WIDTH),
            const(1, SGU_WIDTH),
            const(SGU_WIDTH // LANES, SGU_CHUNK, 2 * SGU_CHUNK),
            const(SGU_WIDTH // LANES, SGU_CHUNK, LANES),
            const(1, D_MODEL),
            const(N_EXPERTS, D_MODEL),
            const(N_EXPERTS, D_MODEL),
            const(N_EXPERTS, 1),
        ],
        out_specs=[
            pl.BlockSpec((tm, D_MODEL), lambda i: (prev(i), 0)),
            pl.BlockSpec((tm, D_MODEL), lambda i: (prev(i), 0)),
            pl.BlockSpec((TOP_K, tm), lambda i: (0, prev(i))),
            pl.BlockSpec((TOP_K, tm), lambda i: (0, prev(i))),
            pl.BlockSpec((1, N_EXPERTS, LANES), lambda i: (prev(i), 0, 0)),
            pl.BlockSpec((N_EXPERTS, LANES), lambda i: (0, 0)),
        ],
        out_shape=[
            jax.ShapeDtypeStruct((t, D_MODEL), F32),
            jax.ShapeDtypeStruct((t, D_MODEL), BF16),
            jax.ShapeDtypeStruct((TOP_K, t), I32),
            jax.ShapeDtypeStruct((TOP_K, t), F32),
            jax.ShapeDtypeStruct((t // tm, N_EXPERTS, LANES), I32),
            jax.ShapeDtypeStruct((N_EXPERTS, LANES), I32),
        ],
        scratch_shapes=[
            pltpu.VMEM((tm, SGU_WIDTH), BF16),
            pltpu.VMEM((N_EXPERTS, LANES), F32),
        ] + [pltpu.VMEM((tm, D_MODEL), BF16)] * 2,
        compiler_params=_params("post"),
        name="post",
    )(attn2, z2, z2, gates2, gates2, x2, wa, wb, wo, lng, lnb, wcat, bcat, n2g, wrh, wrl, br)


def _route_rows(e_ref, seen, tm):
    e = e_ref[...]
    eio = lax.broadcasted_iota(I32, (N_EXPERTS, tm), 0)
    onehots = [eio == e[k:k + 1, :] for k in range(TOP_K)]
    sel = jnp.zeros((N_EXPERTS, tm), F32)
    for oh in onehots:
        sel = jnp.where(oh, 1.0, sel)
    ti = lax.broadcasted_iota(I32, (tm, tm), 0)
    tj = lax.broadcasted_iota(I32, (tm, tm), 1)
    before = jnp.where(ti < tj, 1.0, 0.0).astype(BF16)
    slot = _dot(sel.astype(BF16), before)
    cnt = jnp.sum(sel, axis=1, keepdims=True)
    head = seen - SUBLANES * jnp.floor(seen * (1.0 / SUBLANES))
    tiles = jnp.where(cnt > 0.0, jnp.floor((head + cnt + (SUBLANES - 1)) * (1.0 / SUBLANES)), 0.0)
    xi = lax.broadcasted_iota(I32, (N_EXPERTS, N_EXPERTS), 0)
    xj = lax.broadcasted_iota(I32, (N_EXPERTS, N_EXPERTS), 1)
    lower = jnp.where(xj < xi, 1.0, 0.0).astype(BF16)
    off = _dot(lower, jnp.broadcast_to(tiles, (N_EXPERTS, LANES)).astype(BF16))[:, 0:1] * float(SUBLANES)
    row = off + head + slot
    return [jnp.sum(jnp.where(oh, row, 0.0), axis=0, keepdims=True) for oh in onehots], cnt


def _pick(rows, values, default):
    tm = rows[0].shape[1]
    rio = lax.broadcasted_iota(I32, (ROUTE_ROWS, tm), 0).astype(F32)
    out = default
    for r, v in zip(rows, values):
        out = jnp.where(rio == r, v, out)
    return out


def _wait_rows(buf, hbm_ref, sem, n):
    n = pl.multiple_of(n, SUBLANES)

    @pl.when(n > 0)
    def _():
        pltpu.make_async_copy(buf.at[pl.ds(0, n), :], hbm_ref.at[pl.ds(0, n), :], sem).wait()


def _dispatch_kernel(start_ref, off_ref, staged_ref, full_ref, tot_ref, flush_ref, za_ref, zb_ref, hf_ref, e_ref,
                     xs_ref, rows_ref, obuf, zbuf, seen_s, tail_s, first_s, sem, semz):
    tm = hf_ref.shape[0]
    i = pl.program_id(0)
    last = pl.num_programs(0) - 1
    cur = lax.rem(i, 2)
    spare = ROUTE_ROWS - SUBLANES

    @pl.when(i == 0)
    def _():
        seen_s[...] = jnp.zeros_like(seen_s)
        tail_s[...] = jnp.zeros_like(tail_s)

    @pl.when(i >= 2)
    def _():
        _wait_rows(obuf.at[cur], xs_ref, sem.at[cur], tot_ref[jnp.maximum(i - 2, 0)])

    rows, cnt = _route_rows(e_ref, seen_s[:, 0:1], tm)
    seen_s[...] = seen_s[...] + cnt
    for k in range(TOP_K):
        rows_ref[k:k + 1, :] = rows[k]
    pt = _pick(rows, [1.0] * TOP_K, 0.0).astype(BF16)
    obuf[cur] = _dot(pt, hf_ref[...])

    for x in range(N_EXPERTS):
        k = i * N_EXPERTS + x
        staged, full = staged_ref[k], full_ref[k]
        has = jnp.where(staged > 0, 1.0, 0.0)
        partial = jnp.where(staged > full, 1.0, 0.0)
        alone = jnp.where(full == 0, 1.0, 0.0)
        first = pl.multiple_of(jnp.where(staged > 0, off_ref[k], spare), SUBLANES)
        rest = pl.multiple_of(jnp.where(staged > 0, off_ref[k] + full, spare), SUBLANES)
        kept = tail_s[x]
        merged = obuf[cur, pl.ds(first, SUBLANES), :] + kept * has
        first_s[cur, x] = merged
        ending = merged * alone + obuf[cur, pl.ds(rest, SUBLANES), :] * (1.0 - alone)
        tail_s[x] = ending * partial + kept * (1.0 - has)

    for x in range(N_EXPERTS):
        n = pl.multiple_of(full_ref[i * N_EXPERTS + x], SUBLANES)
        dst = pl.multiple_of(start_ref[i * N_EXPERTS + x], SUBLANES)

        @pl.when(n > 0)
        def _(x=x, dst=dst):
            pltpu.make_async_copy(first_s.at[cur, x], xs_ref.at[pl.ds(dst, SUBLANES), :], sem.at[cur]).start()

        @pl.when(n > SUBLANES)
        def _(x=x, n=n, dst=dst):
            src = pl.multiple_of(off_ref[i * N_EXPERTS + x] + SUBLANES, SUBLANES)
            more = pl.multiple_of(n - SUBLANES, SUBLANES)
            pltpu.make_async_copy(obuf.at[cur, pl.ds(src, more), :],
                                  xs_ref.at[pl.ds(pl.multiple_of(dst + SUBLANES, SUBLANES), more), :], sem.at[cur]).start()

    @pl.when(i == last)
    def _():
        @pl.when(i >= 1)
        def _():
            _wait_rows(obuf.at[1 - cur], xs_ref, sem.at[1 - cur], tot_ref[jnp.maximum(i - 1, 0)])

        _wait_rows(obuf.at[cur], xs_ref, sem.at[cur], tot_ref[i])
        for x in range(N_EXPERTS):
            row = flush_ref[x]

            @pl.when(row >= 0)
            def _(x=x, row=row):
                cp = pltpu.make_async_copy(tail_s.at[x], xs_ref.at[pl.ds(pl.multiple_of(row, SUBLANES), SUBLANES), :], semz)
                cp.start()
                cp.wait()
        zbuf[...] = jnp.zeros_like(zbuf)

        def zero_region(z, act):
            n = zb_ref[z] - za_ref[z]
            big = n // EXPERT_TILE
            rest = pl.multiple_of(n - big * EXPERT_TILE, SUBLANES)

            def piece(row, nrows):
                return pltpu.make_async_copy(zbuf.at[pl.ds(0, nrows), :],
                                             xs_ref.at[pl.ds(pl.multiple_of(row, SUBLANES), nrows), :], semz)

            lax.fori_loop(0, big, lambda m, c: (act(piece(za_ref[z] + m * EXPERT_TILE, EXPERT_TILE)), c)[1], 0)

            @pl.when(rest > 0)
            def _():
                act(piece(za_ref[z] + big * EXPERT_TILE, rest))

        for z in range(N_EXPERTS + 1):
            zero_region(z, lambda cp: cp.start())
        for z in range(N_EXPERTS + 1):
            zero_region(z, lambda cp: cp.wait())


def _dispatch(start, off, staged, full, tot, flush, za, zb, hf, e4, cap):
    t = hf.shape[0]
    tm = TM_ROUTE
    tok = lambda i, *_: (0, i)
    return pl.pallas_call(
        _dispatch_kernel,
        grid_spec=pltpu.PrefetchScalarGridSpec(
            num_scalar_prefetch=8,
            grid=(t // tm,),
            in_specs=[
                pl.BlockSpec((tm, D_MODEL), lambda i, *_: (i, 0)),
                pl.BlockSpec((TOP_K, tm), tok),
            ],
            out_specs=[pl.BlockSpec(memory_space=pl.ANY), pl.BlockSpec((TOP_K, tm), tok)],
            scratch_shapes=[
                pltpu.VMEM((2, ROUTE_ROWS, D_MODEL), F32),
                pltpu.VMEM((EXPERT_TILE, D_MODEL), F32),
                pltpu.VMEM((N_EXPERTS, LANES), F32),
                pltpu.VMEM((N_EXPERTS, SUBLANES, D_MODEL), F32),
                pltpu.VMEM((2, N_EXPERTS, SUBLANES, D_MODEL), F32),
                pltpu.SemaphoreType.DMA((2,)),
                pltpu.SemaphoreType.DMA(()),
            ],
        ),
        out_shape=[jax.ShapeDtypeStruct((cap, D_MODEL), F32), jax.ShapeDtypeStruct((TOP_K, t), F32)],
        compiler_params=_params("dispatch"),
        name="dispatch",
    )(start, off, staged, full, tot, flush, za, zb, hf, e4)


def _expert_kernel(te_ref, nu_ref, xs_ref, wgu_ref, bgu_ref, wdn_ref, bdn_ref, ys_ref, wgu_s, wdn_s):
    i = pl.program_id(0)

    @pl.when(i < nu_ref[0])
    def _():
        @pl.when((i == 0) | (te_ref[i] != te_ref[jnp.maximum(i - 1, 0)]))
        def _():
            wgu_s[...] = wgu_ref[0].astype(BF16)
            wdn_s[...] = wdn_ref[0].astype(BF16)

        x = xs_ref[...].astype(BF16)
        gu = _dot(x, wgu_s[...]) + bgu_ref[0]
        gate = jnp.minimum(gu[:, :D_EXPERT], SWIGLU_LIMIT)
        up = jnp.clip(gu[:, D_EXPERT:], -SWIGLU_LIMIT, SWIGLU_LIMIT)
        act = (up + 1.0) * (gate * jax.nn.sigmoid(SWIGLU_ALPHA * gate))
        ys_ref[...] = _dot(act.astype(BF16), wdn_s[...]) + bdn_ref[0]

    @pl.when(i >= nu_ref[0])
    def _():
        ys_ref[...] = jnp.zeros_like(ys_ref)


def _experts(tile_expert, n_used, xs, wgu, bgu, wdn, bdn):
    cap = xs.shape[0]
    n_tiles = cap // EXPERT_TILE
    row = lambda i, te, nu: (jnp.minimum(i, nu[0] - 1), 0)
    by_expert = lambda i, te, nu: (te[i], 0, 0)
    return pl.pallas_call(
        _expert_kernel,
        grid_spec=pltpu.PrefetchScalarGridSpec(
            num_scalar_prefetch=2,
            grid=(n_tiles,),
            in_specs=[
                pl.BlockSpec((EXPERT_TILE, D_MODEL), row),
                pl.BlockSpec((1, D_MODEL, 2 * D_EXPERT), by_expert),
                pl.BlockSpec((1, 1, 2 * D_EXPERT), by_expert),
                pl.BlockSpec((1, D_EXPERT, D_MODEL), by_expert),
                pl.BlockSpec((1, 1, D_MODEL), by_expert),
            ],
            out_specs=pl.BlockSpec((EXPERT_TILE, D_MODEL), lambda i, te, nu: (i, 0)),
            scratch_shapes=[
                pltpu.VMEM((D_MODEL, 2 * D_EXPERT), BF16),
                pltpu.VMEM((D_EXPERT, D_MODEL), BF16),
            ],
        ),
        out_shape=jax.ShapeDtypeStruct((cap, D_MODEL), F32),
        compiler_params=_params("experts"),
        name="experts",
    )(tile_expert, n_used, xs, wgu, bgu, wdn, bdn)


def _combine_kernel(start_ref, off_ref, cnt_ref, tot_ref, x1_ref, rows_ref, g_ref, ys_ref, o_ref, sbuf, sem):
    i = pl.program_id(0)
    last = pl.num_programs(0) - 1
    cur = lax.rem(i, 2)

    def fetch(step, b):
        for x in range(N_EXPERTS):
            n = pl.multiple_of(cnt_ref[step * N_EXPERTS + x], SUBLANES)

            @pl.when(n > 0)
            def _(x=x, n=n):
                src = pl.multiple_of(start_ref[step * N_EXPERTS + x], SUBLANES)
                dst = pl.multiple_of(off_ref[step * N_EXPERTS + x], SUBLANES)
                pltpu.make_async_copy(ys_ref.at[pl.ds(src, n), :], sbuf.at[b, pl.ds(dst, n), :], sem.at[b]).start()

    @pl.when(i == 0)
    def _():
        sbuf[...] = jnp.zeros_like(sbuf)
        fetch(i, cur)

    @pl.when(i < last)
    def _():
        fetch(i + 1, 1 - cur)

    rows = [rows_ref[k:k + 1, :] for k in range(TOP_K)]
    pt = _pick(rows, [1.0] * TOP_K, 0.0).astype(BF16)
    g = g_ref[...]
    gate = jnp.sum(_pick(rows, [g[k:k + 1, :] for k in range(TOP_K)], 0.0), axis=1, keepdims=True)
    _wait_rows(sbuf.at[cur], ys_ref, sem.at[cur], tot_ref[i])
    weighted = (sbuf[cur] * gate).astype(BF16)
    picked = lax.dot_general(pt, weighted, _TN, preferred_element_type=F32)
    o_ref[...] = x1_ref[...] + picked


def _combine(start, off, cnt, tot, x1, rows4, g4, ys):
    t = x1.shape[0]
    tm = TM_ROUTE
    return pl.pallas_call(
        _combine_kernel,
        grid_spec=pltpu.PrefetchScalarGridSpec(
            num_scalar_prefetch=4,
            grid=(t // tm,),
            in_specs=[
                pl.BlockSpec((tm, D_MODEL), lambda i, *_: (i, 0)),
                pl.BlockSpec((TOP_K, tm), lambda i, *_: (0, i)),
                pl.BlockSpec((TOP_K, tm), lambda i, *_: (0, i)),
                pl.BlockSpec(memory_space=pl.ANY),
            ],
            out_specs=pl.BlockSpec((tm, D_MODEL), lambda i, *_: (i, 0)),
            scratch_shapes=[
                pltpu.VMEM((2, ROUTE_ROWS, D_MODEL), F32),
                pltpu.SemaphoreType.DMA((2,)),
            ],
        ),
        out_shape=jax.ShapeDtypeStruct((t, D_MODEL), F32),
        compiler_params=_params("combine"),
        name="combine",
    )(start, off, cnt, tot, x1, rows4, g4, ys)


def _layer(x, positions, norm1_g, w_in, q_norm_g, k_norm_g, sgu_ln_g, sgu_ln_b, w_spatial, b_spatial,
           w_branch_a, w_branch_b, w_out, norm2_g, w_router, b_router, w_gate_up, b_gate_up, w_down, b_down):
    b, seq, _ = x.shape
    t = b * seq
    x2 = x.reshape(t, D_MODEL)

    inv_freq = ROPE_THETA ** (-jnp.arange(ROT_HALF, dtype=F32) / ROT_HALF)
    invf = jnp.tile(inv_freq, LANES // ROT_HALF).reshape(1, LANES)
    dim_of_lane = [(l // (2 * ROT_HALF)) * ROT_HALF + l % ROT_HALF for l in range(LANES)]
    qg = q_norm_g[jnp.array(dim_of_lane)].reshape(1, LANES)
    kg = k_norm_g[jnp.array(dim_of_lane)].reshape(1, LANES)
    w_bf = w_in.astype(BF16)
    qk = w_bf[:, :2 * ATTN_WIDTH].reshape(D_MODEL, 2 * ATTN_WIDTH // LANES, 2, 2, ROT_HALF)
    qk = qk.transpose(0, 1, 3, 2, 4).reshape(D_MODEL, 2 * ATTN_WIDTH)
    w_in_p = jnp.concatenate([qk, w_bf[:, 2 * ATTN_WIDTH:]], axis=1)
    n_pairs = SGU_WIDTH // LANES
    wcat = w_spatial.reshape(n_pairs, 2, SGU_CHUNK, SGU_CHUNK).transpose(0, 2, 1, 3).reshape(n_pairs, SGU_CHUNK, 2 * SGU_CHUNK)
    bcat = jnp.repeat(b_spatial.reshape(n_pairs, 2, SGU_CHUNK).transpose(0, 2, 1), SGU_GROUP_DIM, axis=2)
    wr_t = w_router.T
    wr_hi = wr_t.astype(BF16)
    wr_lo = (wr_t - wr_hi.astype(F32)).astype(BF16)

    *qkv_groups, z, gates = _inproj(x2, norm1_g.reshape(1, D_MODEL), w_in_p, positions.reshape(-1, LANES // ROT_HALF), invf, qg, kg, b, seq)
    attn = _attention(qkv_groups)
    x1, hf, e4, g4, cst, counts = _post(
        attn.reshape(t, ATTN_WIDTH), z, gates, x2,
        w_branch_a.astype(BF16), w_branch_b.astype(BF16), w_out.astype(BF16),
        sgu_ln_g.reshape(1, SGU_WIDTH), sgu_ln_b.reshape(1, SGU_WIDTH), wcat.astype(BF16), bcat,
        norm2_g.reshape(1, D_MODEL), wr_hi, wr_lo, b_router.reshape(N_EXPERTS, 1))

    steps_per_tile = TM_POST // TM_ROUTE
    n_steps = t // TM_ROUTE
    before = cst[:, :, :steps_per_tile].transpose(0, 2, 1).reshape(n_steps, N_EXPERTS)
    total = counts[:, 0]
    cnt = jnp.concatenate([before[1:], total[None, :]], axis=0) - before
    head = before % SUBLANES
    staged = jnp.where(cnt > 0, (head + cnt + SUBLANES - 1) // SUBLANES * SUBLANES, 0)
    full = jnp.where(cnt > 0, (head + cnt) // SUBLANES * SUBLANES, 0)
    off = (jnp.cumsum(staged, axis=1) - staged).astype(I32).reshape(-1)
    tot_staged = jnp.sum(staged, axis=1).astype(I32)
    tot_full = jnp.sum(full, axis=1).astype(I32)
    region = (total + EXPERT_TILE - 1) // EXPERT_TILE * EXPERT_TILE
    rend = jnp.cumsum(region)
    rstart = rend - region
    start = (rstart[None, :] + before - head).astype(I32).reshape(-1)
    bound = t * TOP_K + N_EXPERTS * (EXPERT_TILE - 1)
    cap = (bound + EXPERT_TILE - 1) // EXPERT_TILE * EXPERT_TILE
    n_tiles = cap // EXPERT_TILE
    tile_expert = jnp.sum(jnp.arange(n_tiles, dtype=I32)[:, None] * EXPERT_TILE >= rend[None, :], axis=1)
    tile_expert = jnp.minimum(tile_expert, N_EXPERTS - 1).astype(I32)
    n_used = (rend[-1:] // EXPERT_TILE).astype(I32)
    filled = (total + SUBLANES - 1) // SUBLANES * SUBLANES
    flush = jnp.where(total % SUBLANES != 0, rstart + total // SUBLANES * SUBLANES, -1).astype(I32)
    za = jnp.concatenate([rstart + filled, rend[-1:]]).astype(I32)
    zb = jnp.concatenate([rend, jnp.full((1,), cap, rend.dtype)]).astype(I32)
    staged = staged.astype(I32).reshape(-1)
    full = full.astype(I32).reshape(-1)

    xs, rows4 = _dispatch(start, off, staged, full, tot_full, flush, za, zb, hf, e4, cap)
    ys = _experts(tile_expert, n_used, xs,
                  w_gate_up, b_gate_up.reshape(N_EXPERTS, 1, 2 * D_EXPERT),
                  w_down, b_down.reshape(N_EXPERTS, 1, D_MODEL))
    out = _combine(start, off, staged, tot_staged, x1, rows4, g4, ys)
    return out.reshape(b, seq, D_MODEL)


def kernel(x, positions, norm1_g, w_in, q_norm_g, k_norm_g, sgu_ln_g, sgu_ln_b, w_spatial, b_spatial, w_branch_a, w_branch_b, w_out, norm2_g, w_router, b_router, w_gate_up, b_gate_up, w_down, b_down):
    for layer in range(norm1_g.shape[0]):
        x = _layer(x, positions, norm1_g[layer], w_in[layer], q_norm_g[layer], k_norm_g[layer],
                   sgu_ln_g[layer], sgu_ln_b[layer], w_spatial[layer], b_spatial[layer],
                   w_branch_a[layer], w_branch_b[layer], w_out[layer], norm2_g[layer],
                   w_router[layer], b_router[layer], w_gate_up[layer], b_gate_up[layer],
                   w_down[layer], b_down[layer])
    return x
```
